```python
import jax, jax.numpy as jnp
from jax import lax
import numpy as np

D_MODEL = 1024
BATCH = 8
SEQ = 2048
DEPTH = 2

CTX_LEN = 256
GRID_W = 64
EXPAND = 2
D_INNER = EXPAND * D_MODEL
HG_KEY_DIM = 128
HG_HEADS = D_INNER // HG_KEY_DIM
HG_VAL_DIM = D_INNER // HG_HEADS
HG_CHUNK = 64
RG_BLOCK = 256
RG_HEADS = D_INNER // RG_BLOCK
RG_CONV_W = 4
RG_C = 8.0
N_HG = (DEPTH + 1) // 2
N_RG = DEPTH // 2
EPS = 1e-6

kernel_name = 'hybrid_hgrn2_rglru_prefix_dit'

F32 = jnp.float32


def _rms(x, g):
    xf = x.astype(F32)
    return xf * lax.rsqrt(jnp.mean(xf * xf, axis=-1, keepdims=True) + EPS) * g.astype(F32)


def _heads(a):
    B, T, _ = a.shape
    return a.reshape(B, T, HG_HEADS, -1).transpose(0, 2, 1, 3)


def _hgrn2_chunk_scan(q, k, logf, v, s0):
    B, H, T, K = q.shape
    V = v.shape[-1]
    n = T // HG_CHUNK

    def chunks(a):
        return a.reshape(B, H, n, HG_CHUNK, a.shape[-1]).transpose(2, 0, 1, 3, 4)

    lower_tri = jnp.tril(jnp.ones((HG_CHUNK, HG_CHUNK), dtype=bool))[:, :, None]

    def step(S, inp):
        qc, kc, gc, vc = inp
        b = jnp.cumsum(gc, axis=2)
        diff = b[:, :, :, None, :] - b[:, :, None, :, :]
        decay = jnp.exp(jnp.where(lower_tri, diff, -jnp.inf))
        scores = jnp.einsum('bhtk,bhsk,bhtsk->bhts', qc, kc, decay)
        o = (jnp.einsum('bhts,bhsv->bhtv', scores, vc)
             + jnp.einsum('bhtk,bhkv->bhtv', qc * jnp.exp(b), S))
        b_last = b[:, :, -1:, :]
        S = (jnp.exp(b_last[:, :, 0, :])[..., None] * S
             + jnp.einsum('bhsk,bhsv->bhkv', kc * jnp.exp(b_last - b), vc))
        return S, o

    S, o = lax.scan(step, s0, (chunks(q), chunks(k), chunks(logf), chunks(v)))
    o = o.transpose(1, 2, 0, 3, 4).reshape(B, H, T, V)
    return o, S


def hgrn2_mixer(h_ctx, h_lat, w_in, lb, norm_g, w_out, need_ctx):
    def prep(hh):
        q, v, zf, zb, g = jnp.split(hh @ w_in, 5, axis=-1)
        q = jax.nn.silu(q.astype(F32))

        def forget(z):
            f = lb + (1.0 - lb) * jax.nn.sigmoid(z.astype(F32))
            return _heads(jnp.log(f)), _heads(1.0 - f)

        return _heads(q), _heads(v.astype(F32)), forget(zf), forget(zb), g

    def run(q, k, logf, v, s0, reverse):
        if reverse:
            q, k, logf, v = (jnp.flip(a, axis=2) for a in (q, k, logf, v))
        o, s = _hgrn2_chunk_scan(q, k, logf, v, s0)
        if reverse:
            o = jnp.flip(o, axis=2)
        return o, s

    def readout(o, g):
        B, H, T, V = o.shape
        o = _rms(o.transpose(0, 2, 1, 3), norm_g.reshape(H, V)).reshape(B, T, H * V)
        o = o * jax.nn.silu(g.astype(F32))
        return o.astype(w_out.dtype) @ w_out

    B = h_lat.shape[0]
    s0 = jnp.zeros((B, HG_HEADS, HG_KEY_DIM, HG_VAL_DIM), F32)
    qc, vc, (gcf, kcf), (gcb, kcb), g_c = prep(h_ctx)
    ocf, sf = run(qc, kcf, gcf, vc, s0, False)
    ocb, sb = run(qc, kcb, gcb, vc, s0, True)
    ql, vl, (glf, klf), (glb, klb), g_l = prep(h_lat)
    olf, _ = run(ql, klf, glf, vl, sf, False)
    olb, _ = run(ql, klb, glb, vl, sb, True)
    y = readout(olf + olb, g_l)
    yc = readout(ocf + ocb, g_c) if need_ctx else None
    return yc, y


def _conv_centred(xb, conv_w, conv_b):
    E = xb.shape[-1]
    y = lax.conv_general_dilated(xb, conv_w[:, None, :].astype(xb.dtype), window_strides=(1,),
                                 padding=[(2, 1)], dimension_numbers=('NWC', 'WIO', 'NWC'),
                                 feature_group_count=E)
    return y + conv_b


def _linear_scan(a, u, h0):
    def combine(l, r):
        return (l[0] * r[0], r[0] * l[1] + r[1])
    A, Bv = lax.associative_scan(combine, (a, u), axis=1)
    return A * h0[:, None, :] + Bv


def rglru_mixer(h_ctx, h_lat, w_in, conv_w, conv_b, w_a, b_a, w_x, b_x, lam, w_out, need_ctx):
    B, T, _ = h_lat.shape
    rows = T // GRID_W
    h_lat = h_lat.reshape(B, rows, GRID_W, -1).transpose(0, 2, 1, 3).reshape(B, T, -1)

    def prep(hh):
        xb, g = jnp.split(hh @ w_in, 2, axis=-1)
        return _conv_centred(xb, conv_w, conv_b).astype(F32), g

    def coeffs(xb, d):
        Bq, Tq, _ = xb.shape
        xh = xb.reshape(Bq, Tq, RG_HEADS, RG_BLOCK)
        r = jax.nn.sigmoid(jnp.einsum('bthi,hij->bthj', xh, w_a[d].astype(F32)).reshape(Bq, Tq, D_INNER)
                           + b_a[d].astype(F32))
        ig = jax.nn.sigmoid(jnp.einsum('bthi,hij->bthj', xh, w_x[d].astype(F32)).reshape(Bq, Tq, D_INNER)
                            + b_x[d].astype(F32))
        log_a = -RG_C * r * jax.nn.softplus(-lam[d].astype(F32))
        a = jnp.exp(log_a)
        u = jnp.sqrt(-jnp.expm1(2.0 * log_a)) * (ig * xb)
        return a, u

    def run(a, u, h0, reverse):
        if reverse:
            a, u = jnp.flip(a, axis=1), jnp.flip(u, axis=1)
        h = _linear_scan(a, u, h0)
        last = h[:, -1]
        if reverse:
            h = jnp.flip(h, axis=1)
        return h, last

    def readout(hsum, g):
        return (hsum * jax.nn.silu(g.astype(F32))).astype(w_out.dtype) @ w_out

    h0 = jnp.zeros((B, D_INNER), F32)
    xc, g_c = prep(h_ctx)
    ycf, hf = run(*coeffs(xc, 0), h0, False)
    ycb, hb = run(*coeffs(xc, 1), h0, True)
    xl, g_l = prep(h_lat)
    ylf, _ = run(*coeffs(xl, 0), hf, False)
    ylb, _ = run(*coeffs(xl, 1), hb, True)
    y = readout(ylf + ylb, g_l)
    y = y.reshape(B, GRID_W, rows, -1).transpose(0, 2, 1, 3).reshape(B, T, -1)
    yc = readout(ycf + ycb, g_c) if need_ctx else None
    return yc, y


def setup_inputs(seed: int = 0) -> dict:
    key = jax.random.key(seed)
    ks = jax.random.split(key, 24)
    n = jax.random.normal
    D, E = D_MODEL, D_INNER
    u = jax.random.uniform(ks[21], (N_RG, 2, E), minval=0.9, maxval=0.999)
    s = u ** (1.0 / RG_C)
    return {
        'x': n(ks[0], (BATCH, SEQ, D), F32),
        'c': n(ks[1], (BATCH, D), F32),
        'ctx': n(ks[2], (BATCH, CTX_LEN, D), F32),
        'c_ctx': n(ks[3], (D,), F32),
        'ada_w': n(ks[4], (DEPTH, D, 3 * D), F32) * (0.5 * D ** -0.5),
        'ada_b': n(ks[5], (DEPTH, 3 * D), F32) * 0.02,
        'norm_g': 1.0 + 0.02 * n(ks[6], (DEPTH, D), F32),
        'final_norm_g': 1.0 + 0.02 * n(ks[7], (D,), F32),
        'hg_w_in': n(ks[8], (N_HG, D, 5 * E), F32) * D ** -0.5,
        'hg_lower_bounds': n(ks[9], (DEPTH + 1, E), F32) * 0.5,
        'hg_norm_g': 1.0 + 0.02 * n(ks[10], (N_HG, E), F32),
        'hg_w_out': n(ks[11], (N_HG, E, D), F32) * E ** -0.5,
        'rg_w_in': n(ks[12], (N_RG, D, 2 * E), F32) * D ** -0.5,
        'rg_conv_w': n(ks[13], (N_RG, RG_CONV_W, E), F32) * RG_CONV_W ** -0.5,
        'rg_conv_b': n(ks[14], (N_RG, E), F32) * 0.02,
        'rg_w_a': n(ks[15], (N_RG, 2, RG_HEADS, RG_BLOCK, RG_BLOCK), F32) * RG_BLOCK ** -0.5,
        'rg_b_a': n(ks[16], (N_RG, 2, E), F32) * 0.02,
        'rg_w_x': n(ks[17], (N_RG, 2, RG_HEADS, RG_BLOCK, RG_BLOCK), F32) * RG_BLOCK ** -0.5,
        'rg_b_x': n(ks[18], (N_RG, 2, E), F32) * 0.02,
        'rg_lambda': jnp.log(s) - jnp.log1p(-s),
        'rg_w_out': n(ks[19], (N_RG, E, D), F32) * E ** -0.5,
    }


def reference(x, c, ctx, c_ctx, ada_w, ada_b, norm_g, final_norm_g, hg_w_in, hg_lower_bounds,
              hg_norm_g, hg_w_out, rg_w_in, rg_conv_w, rg_conv_b, rg_w_a, rg_b_a, rg_w_x, rg_b_x,
              rg_lambda, rg_w_out):
    sc = jax.nn.silu(c)
    scc = jax.nn.silu(c_ctx)
    lb_all = jnp.cumsum(jax.nn.softmax(hg_lower_bounds.astype(F32), axis=0), axis=0)
    for i in range(DEPTH):
        shift, scale, gate = jnp.split(sc @ ada_w[i] + ada_b[i], 3, axis=-1)
        shift_c, scale_c, gate_c = jnp.split(scc @ ada_w[i] + ada_b[i], 3, axis=-1)
        h = (_rms(x, norm_g[i]) * (1.0 + scale[:, None]) + shift[:, None]).astype(x.dtype)
        hc = (_rms(ctx, norm_g[i]) * (1.0 + scale_c) + shift_c).astype(ctx.dtype)
        need_ctx = i < DEPTH - 1
        j = i // 2
        if i % 2 == 0:
            yc, y = hgrn2_mixer(hc, h, hg_w_in[j], lb_all[i], hg_norm_g[j], hg_w_out[j], need_ctx)
        else:
            yc, y = rglru_mixer(hc, h, rg_w_in[j], rg_conv_w[j], rg_conv_b[j], rg_w_a[j], rg_b_a[j],
                                rg_w_x[j], rg_b_x[j], rg_lambda[j], rg_w_out[j], need_ctx)
        x = (x + gate[:, None] * y).astype(x.dtype)
        if need_ctx:
            ctx = (ctx + gate_c * yc).astype(ctx.dtype)
    return _rms(x, final_norm_g).astype(x.dtype)
```

```python
import functools

import jax
import jax.numpy as jnp
from jax import lax
from jax.experimental import pallas as pl
from jax.experimental.pallas import tpu as pltpu

F32 = jnp.float32
BF16 = jnp.bfloat16

EPS = 1e-6
GRID_W = 64
HG_KEY_DIM = 128
HG_CHUNK = 64
RG_BLOCK = 256
RG_C = 8.0
MOD_ROWS = 16
ROW_BLOCK = 256
VMEM_LIMIT = 56 * 1024 * 1024

_NT = (((1,), (1,)), ((), ()))
_TN = (((0,), (0,)), ((), ()))


def _silu(z):
    return z * jax.nn.sigmoid(z)


def _dot(a, b):
    return jnp.dot(a, b, preferred_element_type=F32)


def _ada_kernel(c_ref, w_ref, b_ref, o_ref):
    s = _silu(c_ref[...])
    o_ref[0] = jnp.dot(s, w_ref[0], preferred_element_type=F32,
                       precision=lax.Precision.HIGHEST) + b_ref[0]


def _ada(cond, ada_w, ada_b):
    depth, d, n3 = ada_w.shape
    bn = d
    return pl.pallas_call(
        _ada_kernel,
        grid=(depth, n3 // bn),
        in_specs=[
            pl.BlockSpec((MOD_ROWS, d), lambda i, n: (0, 0)),
            pl.BlockSpec((1, d, bn), lambda i, n: (i, 0, n)),
            pl.BlockSpec((1, 1, bn), lambda i, n: (i, 0, n)),
        ],
        out_specs=pl.BlockSpec((1, MOD_ROWS, bn), lambda i, n: (i, 0, n)),
        out_shape=jax.ShapeDtypeStruct((depth, MOD_ROWS, n3), F32),
        name="ada",
    )(cond, ada_w, ada_b.reshape(depth, 1, n3))


def _norm_mod(xv, g, shift, scale):
    ms = jnp.mean(xv * xv, axis=-1, keepdims=True)
    return xv * lax.rsqrt(ms + EPS) * g * (1.0 + scale) + shift


def _prenorm_kernel(x_ref, ctx_ref, mod_ref, g_ref, hx_ref, hc_ref, *, d, ctx_row):
    b = pl.program_id(0)
    g = g_ref[...]
    row = mod_ref[pl.ds(b, 1), :]
    hx_ref[0] = _norm_mod(x_ref[0], g, row[:, :d], row[:, d:2 * d]).astype(BF16)

    @pl.when(pl.program_id(1) == 0)
    def _():
        rowc = mod_ref[ctx_row:ctx_row + 1, :]
        hc_ref[0] = _norm_mod(ctx_ref[0], g, rowc[:, :d], rowc[:, d:2 * d]).astype(BF16)


def _prenorm(x, ctx, mod, g):
    bsz, t, d = x.shape
    tc = ctx.shape[1]
    rb = 512
    return pl.pallas_call(
        functools.partial(_prenorm_kernel, d=d, ctx_row=bsz),
        grid=(bsz, t // rb),
        in_specs=[
            pl.BlockSpec((1, rb, d), lambda b, r: (b, r, 0)),
            pl.BlockSpec((1, tc, d), lambda b, r: (b, 0, 0)),
            pl.BlockSpec((MOD_ROWS, 3 * d), lambda b, r: (0, 0)),
            pl.BlockSpec((1, d), lambda b, r: (0, 0)),
        ],
        out_specs=[
            pl.BlockSpec((1, rb, d), lambda b, r: (b, r, 0)),
            pl.BlockSpec((1, tc, d), lambda b, r: (b, 0, 0)),
        ],
        out_shape=[jax.ShapeDtypeStruct((bsz, t, d), BF16),
                   jax.ShapeDtypeStruct((bsz, tc, d), BF16)],
        name="prenorm",
    )(x, ctx, mod, g.reshape(1, d))


def _hgrn_kernel(hx_ref, hc_ref, wq_ref, wv_ref, wf_ref, wb_ref, wg_ref, lbp_ref, ng_ref, tri_ref,
                 ox_ref, oc_ref,
                 q_s, v_s, lff_s, lfb_s, kf_s, kb_s, sg_s, of_s, ob_s, st_s,
                 *, layer, tc, t, heads):
    n_rows = tc + t
    c = HG_CHUNK
    kd = HG_KEY_DIM
    lf_refs = (lff_s, lfb_s)
    k_refs = (kf_s, kb_s)
    o_refs = (of_s, ob_s)

    lbp = lbp_ref[...]
    ex = jnp.exp(lbp - jnp.max(lbp, axis=0, keepdims=True))
    lb = jnp.sum(ex[:layer + 1], axis=0, keepdims=True) / jnp.sum(ex, axis=0, keepdims=True)

    def project(hrows, r0):
        rows = pl.ds(r0, ROW_BLOCK)
        q_s[rows, :] = _silu(_dot(hrows, wq_ref[...]))
        v_s[rows, :] = _dot(hrows, wv_ref[...]).astype(BF16)
        for w_ref, lf_ref, k_ref in ((wf_ref, lff_s, kf_s), (wb_ref, lfb_s, kb_s)):
            f = lb + (1.0 - lb) * jax.nn.sigmoid(_dot(hrows, w_ref[...]))
            lf_ref[rows, :] = jnp.log(f)
            k_ref[rows, :] = 1.0 - f
        sg_s[rows, :] = _silu(_dot(hrows, wg_ref[...])).astype(BF16)

    for i in range(tc // ROW_BLOCK):
        project(hc_ref[0, i * ROW_BLOCK:(i + 1) * ROW_BLOCK, :], i * ROW_BLOCK)

    def proj_body(i, carry):
        r0 = pl.multiple_of(i * ROW_BLOCK, ROW_BLOCK)
        project(hx_ref[0, pl.ds(r0, ROW_BLOCK), :], tc + r0)
        return carry

    lax.fori_loop(0, t // ROW_BLOCK, proj_body, 0)

    st_s[...] = jnp.zeros_like(st_s)
    rid = lax.broadcasted_iota(jnp.int32, (c, c), 0)
    cid = lax.broadcasted_iota(jnp.int32, (c, c), 1)
    masks = (rid >= cid, rid <= cid)
    n_ctx = tc // c
    n_all = n_rows // c

    def chunk(d, r0):
        rows = pl.ds(r0, c)
        lf = lf_refs[d][rows, :]
        hi = lf.astype(BF16)
        r1 = lf - hi.astype(F32)
        mid = r1.astype(BF16)
        lo = (r1 - mid.astype(F32)).astype(BF16)
        bsum = _dot(tri_ref[d], jnp.concatenate([hi, mid, lo], axis=0))
        for h in range(heads):
            cs = slice(h * kd, (h + 1) * kd)
            bb = bsum[:, cs]
            m = 0.5 * (bb[c - 1:c, :] if d == 0 else bb[0:1, :])
            em = jnp.exp(m)
            qt = (q_s[rows, cs] * jnp.exp(bb - m)).astype(BF16)
            kt = (k_refs[d][rows, cs] * jnp.exp(m - bb)).astype(BF16)
            sc = lax.dot_general(qt, kt, _NT, preferred_element_type=F32)
            sc = jnp.where(masks[d], sc, 0.0).astype(BF16)
            vv = v_s[rows, cs]
            sm = st_s[d, h] * em
            o = _dot(sc, vv) + lax.dot_general(qt, sm.astype(BF16), _NT, preferred_element_type=F32)
            o_refs[d][rows, cs] = o
            ut = lax.dot_general(vv, kt, _TN, preferred_element_type=F32)
            st_s[d, h] = em * (sm + ut)

    def scan_body(i, carry):
        chunk(0, pl.multiple_of(i * c, c))
        jb = jnp.where(i < n_ctx, n_ctx - 1 - i, n_all + n_ctx - 1 - i)
        chunk(1, pl.multiple_of(jb * c, c))
        return carry

    lax.fori_loop(0, n_all, scan_body, 0)

    ng = ng_ref[...]

    def readout(r0):
        rows = pl.ds(r0, ROW_BLOCK)
        o = of_s[rows, :] + ob_s[rows, :]
        parts = []
        for h in range(heads):
            cs = slice(h * kd, (h + 1) * kd)
            oh = o[:, cs]
            ms = jnp.mean(oh * oh, axis=-1, keepdims=True)
            parts.append(oh * lax.rsqrt(ms + EPS) * ng[:, cs])
        return (jnp.concatenate(parts, axis=-1) * sg_s[rows, :].astype(F32)).astype(BF16)

    for i in range(tc // ROW_BLOCK):
        oc_ref[0, i * ROW_BLOCK:(i + 1) * ROW_BLOCK, :] = readout(i * ROW_BLOCK)

    def out_body(i, carry):
        r0 = pl.multiple_of(i * ROW_BLOCK, ROW_BLOCK)
        ox_ref[0, pl.ds(r0, ROW_BLOCK), :] = readout(tc + r0)
        return carry

    lax.fori_loop(0, t // ROW_BLOCK, out_body, 0)


def _hgrn(hx, hc, w_in, lower_bounds, norm_g, layer):
    bsz, t, d = hx.shape
    tc = hc.shape[1]
    e = w_in.shape[1] // 5
    heads = 2
    gw = heads * HG_KEY_DIM
    ng_blocks = e // gw
    n_rows = tc + t
    c = HG_CHUNK

    low = jnp.tril(jnp.ones((c, c), F32))
    tri = jnp.stack([jnp.tile(low, (1, 3)), jnp.tile(low.T, (1, 3))]).astype(BF16)

    def wspec(sec):
        return pl.BlockSpec((d, gw), lambda b, j, sec=sec: (0, sec * ng_blocks + j))

    seq_f32 = pltpu.VMEM((n_rows, gw), F32)
    seq_bf16 = pltpu.VMEM((n_rows, gw), BF16)
    return pl.pallas_call(
        functools.partial(_hgrn_kernel, layer=layer, tc=tc, t=t, heads=heads),
        grid=(bsz, ng_blocks),
        in_specs=[
            pl.BlockSpec((1, t, d), lambda b, j: (b, 0, 0)),
            pl.BlockSpec((1, tc, d), lambda b, j: (b, 0, 0)),
            wspec(0), wspec(1), wspec(2), wspec(3), wspec(4),
            pl.BlockSpec((lower_bounds.shape[0], gw), lambda b, j: (0, j)),
            pl.BlockSpec((1, gw), lambda b, j: (0, j)),
            pl.BlockSpec((2, c, 3 * c), lambda b, j: (0, 0, 0)),
        ],
        out_specs=[
            pl.BlockSpec((1, t, gw), lambda b, j: (b, 0, j)),
            pl.BlockSpec((1, tc, gw), lambda b, j: (b, 0, j)),
        ],
        out_shape=[jax.ShapeDtypeStruct((bsz, t, e), BF16),
                   jax.ShapeDtypeStruct((bsz, tc, e), BF16)],
        scratch_shapes=[
            seq_f32,
            seq_bf16,
            seq_f32, seq_f32,
            seq_f32, seq_f32,
            seq_bf16,
            seq_f32, seq_f32,
            pltpu.VMEM((2, heads, HG_KEY_DIM, HG_KEY_DIM), F32),
        ],
        compiler_params=pltpu.CompilerParams(
            dimension_semantics=("arbitrary", "arbitrary"), vmem_limit_bytes=VMEM_LIMIT),
        name="hgrn2",
    )(hx, hc, w_in, w_in, w_in, w_in, w_in, lower_bounds, norm_g.reshape(1, e), tri)


def _outproj_mid_kernel(ogx_ref, ogc_ref, x_ref, ctx_ref, w_ref, mod0_ref, mod1_ref, g1_ref,
                        x1_ref, hx_ref, hc_ref, *, d, ctx_row):
    b = pl.program_id(0)
    g1 = g1_ref[...]
    row0 = mod0_ref[pl.ds(b, 1), :]
    row1 = mod1_ref[pl.ds(b, 1), :]
    x1 = x_ref[0] + row0[:, 2 * d:] * _dot(ogx_ref[0], w_ref[...])
    x1_ref[0] = x1
    hx_ref[0] = _norm_mod(x1, g1, row1[:, :d], row1[:, d:2 * d]).astype(BF16)

    @pl.when(pl.program_id(1) == 0)
    def _():
        c0 = mod0_ref[ctx_row:ctx_row + 1, :]
        c1 = mod1_ref[ctx_row:ctx_row + 1, :]
        ctx1 = ctx_ref[0] + c0[:, 2 * d:] * _dot(ogc_ref[0], w_ref[...])
        hc_ref[0] = _norm_mod(ctx1, g1, c1[:, :d], c1[:, d:2 * d]).astype(BF16)


def _outproj_mid(ogx, ogc, x, ctx, w_out, mod0, mod1, g1):
    bsz, t, d = x.shape
    tc = ctx.shape[1]
    e = w_out.shape[0]
    rb = 512
    return pl.pallas_call(
        functools.partial(_outproj_mid_kernel, d=d, ctx_row=bsz),
        grid=(bsz, t // rb),
        in_specs=[
            pl.BlockSpec((1, rb, e), lambda b, r: (b, r, 0)),
            pl.BlockSpec((1, tc, e), lambda b, r: (b, 0, 0)),
            pl.BlockSpec((1, rb, d), lambda b, r: (b, r, 0)),
            pl.BlockSpec((1, tc, d), lambda b, r: (b, 0, 0)),
            pl.BlockSpec((e, d), lambda b, r: (0, 0)),
            pl.BlockSpec((MOD_ROWS, 3 * d), lambda b, r: (0, 0)),
            pl.BlockSpec((MOD_ROWS, 3 * d), lambda b, r: (0, 0)),
            pl.BlockSpec((1, d), lambda b, r: (0, 0)),
        ],
        out_specs=[
            pl.BlockSpec((1, rb, d), lambda b, r: (b, r, 0)),
            pl.BlockSpec((1, rb, d), lambda b, r: (b, r, 0)),
            pl.BlockSpec((1, tc, d), lambda b, r: (b, 0, 0)),
        ],
        out_shape=[jax.ShapeDtypeStruct((bsz, t, d), F32),
                   jax.ShapeDtypeStruct((bsz, t, d), BF16),
                   jax.ShapeDtypeStruct((bsz, tc, d), BF16)],
        compiler_params=pltpu.CompilerParams(
            dimension_semantics=("arbitrary", "arbitrary"), vmem_limit_bytes=VMEM_LIMIT),
        name="outproj0",
    )(ogx, ogc, x, ctx, w_out, mod0, mod1, g1.reshape(1, d))


def _outproj_final_kernel(yg_ref, x_ref, w_ref, mod_ref, g_ref, o_ref, *, d):
    b = pl.program_id(0)
    row = mod_ref[pl.ds(b, 1), :]
    x2 = x_ref[0] + row[:, 2 * d:] * _dot(yg_ref[0], w_ref[...])
    ms = jnp.mean(x2 * x2, axis=-1, keepdims=True)
    o_ref[0] = x2 * lax.rsqrt(ms + EPS) * g_ref[...]


def _outproj_final(yg, x1, w_out, mod, g):
    bsz, t, d = x1.shape
    e = w_out.shape[0]
    rb = 512
    return pl.pallas_call(
        functools.partial(_outproj_final_kernel, d=d),
        grid=(bsz, t // rb),
        in_specs=[
            pl.BlockSpec((1, rb, e), lambda b, r: (b, r, 0)),
            pl.BlockSpec((1, rb, d), lambda b, r: (b, r, 0)),
            pl.BlockSpec((e, d), lambda b, r: (0, 0)),
            pl.BlockSpec((MOD_ROWS, 3 * d), lambda b, r: (0, 0)),
            pl.BlockSpec((1, d), lambda b, r: (0, 0)),
        ],
        out_specs=pl.BlockSpec((1, rb, d), lambda b, r: (b, r, 0)),
        out_shape=jax.ShapeDtypeStruct((bsz, t, d), F32),
        compiler_params=pltpu.CompilerParams(
            dimension_semantics=("arbitrary", "arbitrary"), vmem_limit_bytes=VMEM_LIMIT),
        name="outproj1",
    )(yg, x1, w_out, mod, g.reshape(1, d))


def _shift_rows(xv, k):
    n = xv.shape[0]
    rid = lax.broadcasted_iota(jnp.int32, xv.shape, 0)
    rolled = pltpu.roll(xv, k % n, 0)
    valid = (rid >= k) if k > 0 else (rid < n + k)
    return jnp.where(valid, rolled, 0.0)


def _tile_scan(a, u, reverse):
    n = a.shape[0]
    rid = lax.broadcasted_iota(jnp.int32, a.shape, 0)
    s = 1
    while s < n:
        valid = (rid < n - s) if reverse else (rid >= s)
        sh = (n - s) if reverse else s
        a_sh = jnp.where(valid, pltpu.roll(a, sh, 0), 1.0)
        u_sh = jnp.where(valid, pltpu.roll(u, sh, 0), 0.0)
        u = u + a * u_sh
        a = a * a_sh
        s *= 2
    return a, u


def _rglru_kernel(hx_ref, hc_ref, wx_ref, wg_ref, cw_ref, cb_ref, wa_ref, wi_ref, ba_ref, bi_ref,
                  lam_ref, y_ref,
                  xp_s, xc_s, sg_s, af_s, uf_s, ab_s, ub_s, cf_s, cr_s,
                  *, tc, t):
    gw = RG_BLOCK
    gr = GRID_W
    n_tr = t // gr
    a_refs = (af_s, ab_s)
    u_refs = (uf_s, ub_s)
    cw = cw_ref[...]
    cb = cb_ref[...]

    def conv_taps(xm2, xm1, x0, xp1):
        return cw[0:1] * xm2 + cw[1:2] * xm1 + cw[2:3] * x0 + cw[3:4] * xp1 + cb

    xcx = _dot(hc_ref[0], wx_ref[...])
    xc_s[0:tc, :] = conv_taps(_shift_rows(xcx, 2), _shift_rows(xcx, 1), xcx, _shift_rows(xcx, -1))

    pad = 2 * gr

    def proj_body(i, carry):
        r0 = pl.multiple_of(i * ROW_BLOCK, ROW_BLOCK)
        hrows = hx_ref[0, pl.ds(r0, ROW_BLOCK), :]
        xp_s[pl.ds(pad + r0, ROW_BLOCK), :] = _dot(hrows, wx_ref[...])
        sg_s[pl.ds(r0, ROW_BLOCK), :] = _silu(_dot(hrows, wg_ref[...])).astype(BF16)
        return carry

    lax.fori_loop(0, t // ROW_BLOCK, proj_body, 0)

    xp_s[0:gr, :] = _shift_rows(xp_s[pad + (n_tr - 2) * gr:pad + (n_tr - 1) * gr, :], 1)
    xp_s[gr:pad, :] = _shift_rows(xp_s[pad + (n_tr - 1) * gr:pad + n_tr * gr, :], 1)
    xp_s[pad + n_tr * gr:pad + (n_tr + 1) * gr, :] = _shift_rows(xp_s[pad:pad + gr, :], -1)

    def conv_body(r, carry):
        r0 = pl.multiple_of(r * gr, gr)
        xc_s[pl.ds(tc + r0, gr), :] = conv_taps(
            xp_s[pl.ds(r0, gr), :], xp_s[pl.ds(r0 + gr, gr), :],
            xp_s[pl.ds(r0 + 2 * gr, gr), :], xp_s[pl.ds(r0 + 3 * gr, gr), :])
        return carry

    lax.fori_loop(0, n_tr, conv_body, 0)

    nsp = -RG_C * jax.nn.softplus(-lam_ref[...])

    def gate_body(i, carry):
        r0 = pl.multiple_of(i * ROW_BLOCK, ROW_BLOCK)
        rows = pl.ds(r0, ROW_BLOCK)
        xc = xc_s[rows, :]
        xcb = xc.astype(BF16)
        for d in range(2):
            rg = jax.nn.sigmoid(_dot(xcb, wa_ref[d, 0]) + ba_ref[d:d + 1, :])
            ig = jax.nn.sigmoid(_dot(xcb, wi_ref[d, 0]) + bi_ref[d:d + 1, :])
            log_a = rg * nsp[d:d + 1, :]
            a = jnp.exp(log_a)
            a_refs[d][rows, :] = a
            u_refs[d][rows, :] = jnp.sqrt(-jnp.tanh(log_a) * (a * a + 1.0)) * (ig * xc)
        return carry

    lax.fori_loop(0, (tc + t) // ROW_BLOCK, gate_body, 0)

    n_ct = tc // 8

    def ctx_body(i, carry):
        hf, hb = carry
        rf = pl.ds(pl.multiple_of(i * 8, 8), 8)
        a, u = _tile_scan(af_s[rf, :], uf_s[rf, :], False)
        hf = a[7:8] * hf + u[7:8]
        rb = pl.ds(pl.multiple_of((n_ct - 1 - i) * 8, 8), 8)
        a, u = _tile_scan(ab_s[rb, :], ub_s[rb, :], True)
        hb = a[0:1] * hb + u[0:1]
        return hf, hb

    zero = jnp.zeros((1, gw), F32)
    hf, hb = lax.fori_loop(0, n_ct, ctx_body, (zero, zero))

    def local_scan(d):
        def body(i, carry):
            hh, aa = carry
            r = i if d == 0 else n_tr - 1 - i
            rows = pl.ds(tc + pl.multiple_of(r * gr, gr), gr)
            a = a_refs[d][rows, :]
            hh = a * hh + u_refs[d][rows, :]
            aa = a * aa
            u_refs[d][rows, :] = hh
            a_refs[d][rows, :] = aa
            return hh, aa

        lax.fori_loop(0, n_tr, body, (jnp.zeros((gr, gw), F32), jnp.ones((gr, gw), F32)))

    local_scan(0)
    local_scan(1)

    def carry_body(i, carry):
        cf, cr = carry
        cf_s[pl.ds(i, 1), :] = cf
        last = tc + (n_tr - 1) * gr
        cf = uf_s[pl.ds(last + i, 1), :] + af_s[pl.ds(last + i, 1), :] * cf
        w = gr - 1 - i
        cr_s[pl.ds(w, 1), :] = cr
        cr = ub_s[pl.ds(tc + w, 1), :] + ab_s[pl.ds(tc + w, 1), :] * cr
        return cf, cr

    lax.fori_loop(0, gr, carry_body, (hf, hb))

    def out_body(r, carry):
        r0 = pl.multiple_of(r * gr, gr)
        rows = pl.ds(tc + r0, gr)
        h = (uf_s[rows, :] + af_s[rows, :] * cf_s[...]) + (ub_s[rows, :] + ab_s[rows, :] * cr_s[...])
        y_ref[0, pl.ds(r0, gr), :] = (h * sg_s[pl.ds(r0, gr), :].astype(F32)).astype(BF16)
        return carry

    lax.fori_loop(0, n_tr, out_body, 0)


def _rglru(hx, hc, w_in, conv_w, conv_b, w_a, b_a, w_x, b_x, lam):
    bsz, t, d = hx.shape
    tc = hc.shape[1]
    e = w_in.shape[1] // 2
    gw = RG_BLOCK
    nb = e // gw
    n_rows = tc + t
    seq_f32 = pltpu.VMEM((n_rows, gw), F32)
    vec = lambda rows: pl.BlockSpec((rows, gw), lambda b, j: (0, j))
    gate_w = pl.BlockSpec((2, 1, gw, gw), lambda b, j: (0, j, 0, 0))
    return pl.pallas_call(
        functools.partial(_rglru_kernel, tc=tc, t=t),
        grid=(bsz, nb),
        in_specs=[
            pl.BlockSpec((1, t, d), lambda b, j: (b, 0, 0)),
            pl.BlockSpec((1, tc, d), lambda b, j: (b, 0, 0)),
            pl.BlockSpec((d, gw), lambda b, j: (0, j)),
            pl.BlockSpec((d, gw), lambda b, j: (0, nb + j)),
            vec(conv_w.shape[0]), vec(1),
            gate_w, gate_w, vec(2), vec(2), vec(2),
        ],
        out_specs=pl.BlockSpec((1, t, gw), lambda b, j: (b, 0, j)),
        out_shape=jax.ShapeDtypeStruct((bsz, t, e), BF16),
        scratch_shapes=[
            pltpu.VMEM((t + 3 * GRID_W, gw), F32),
            seq_f32,
            pltpu.VMEM((t, gw), BF16),
            seq_f32, seq_f32, seq_f32, seq_f32,
            pltpu.VMEM((GRID_W, gw), F32), pltpu.VMEM((GRID_W, gw), F32),
        ],
        compiler_params=pltpu.CompilerParams(
            dimension_semantics=("arbitrary", "arbitrary"), vmem_limit_bytes=VMEM_LIMIT),
        name="rglru",
    )(hx, hc, w_in, w_in, conv_w, conv_b.reshape(1, e), w_a, w_x, b_a, b_x, lam)


def kernel(x, c, ctx, c_ctx, ada_w, ada_b, norm_g, final_norm_g, hg_w_in, hg_lower_bounds,
           hg_norm_g, hg_w_out, rg_w_in, rg_conv_w, rg_conv_b, rg_w_a, rg_b_a, rg_w_x, rg_b_x,
           rg_lambda, rg_w_out):
    bsz, t, d = x.shape
    assert ada_w.shape[0] == 2 and bsz < MOD_ROWS and t % GRID_W == 0
    cond = jnp.concatenate(
        [c, c_ctx[None, :], jnp.zeros((MOD_ROWS - bsz - 1, d), F32)], axis=0)
    mod = _ada(cond, ada_w, ada_b)

    hx, hc = _prenorm(x, ctx, mod[0], norm_g[0])
    ogx, ogc = _hgrn(hx, hc, hg_w_in[0].astype(BF16), hg_lower_bounds, hg_norm_g[0], layer=0)
    x1, hx1, hc1 = _outproj_mid(ogx, ogc, x, ctx, hg_w_out[0].astype(BF16), mod[0], mod[1], norm_g[1])
    yg = _rglru(hx1, hc1, rg_w_in[0].astype(BF16), rg_conv_w[0], rg_conv_b[0],
                rg_w_a[0].astype(BF16), rg_b_a[0], rg_w_x[0].astype(BF16), rg_b_x[0], rg_lambda[0])
    return _outproj_final(yg, x1, rg_w_out[0].astype(BF16), mod[1], final_norm_g)
```

```python
import functools

import jax
import jax.numpy as jnp
from jax import lax
from jax.experimental import pallas as pl
from jax.experimental.pallas import tpu as pltpu

F32 = jnp.float32
BF16 = jnp.bfloat16

EPS = 1e-6
GRID_W = 64
HG_KEY_DIM = 128
HG_CHUNK = 64
RG_BLOCK = 256
RG_C = 8.0
MOD_ROWS = 16
ROW_BLOCK = 256
SCAN_STEPS = 6
VMEM_LIMIT = 56 * 1024 * 1024

_NT = (((1,), (1,)), ((), ()))
_TN = (((0,), (0,)), ((), ()))


def _silu(z):
    return z * jax.nn.sigmoid(z)


def _dot(a, b):
    return jnp.dot(a, b, preferred_element_type=F32)


def _ada_kernel(c_ref, w_ref, b_ref, o_ref):
    s = _silu(c_ref[...])
    o_ref[0] = jnp.dot(s, w_ref[0], preferred_element_type=F32,
                       precision=lax.Precision.HIGHEST) + b_ref[0]


def _ada(cond, ada_w, ada_b):
    depth, d, n3 = ada_w.shape
    bn = d
    return pl.pallas_call(
        _ada_kernel,
        grid=(depth, n3 // bn),
        in_specs=[
            pl.BlockSpec((MOD_ROWS, d), lambda i, n: (0, 0)),
            pl.BlockSpec((1, d, bn), lambda i, n: (i, 0, n)),
            pl.BlockSpec((1, 1, bn), lambda i, n: (i, 0, n)),
        ],
        out_specs=pl.BlockSpec((1, MOD_ROWS, bn), lambda i, n: (i, 0, n)),
        out_shape=jax.ShapeDtypeStruct((depth, MOD_ROWS, n3), F32),
        name="ada",
    )(cond, ada_w, ada_b.reshape(depth, 1, n3))


def _norm_mod(xv, g, shift, scale):
    ms = jnp.mean(xv * xv, axis=-1, keepdims=True)
    return xv * lax.rsqrt(ms + EPS) * g * (1.0 + scale) + shift


def _prenorm_kernel(x_ref, ctx_ref, mod_ref, g_ref, hx_ref, hc_ref, *, d, ctx_row):
    b = pl.program_id(0)
    g = g_ref[...]
    row = mod_ref[pl.ds(b, 1), :]
    hx_ref[0] = _norm_mod(x_ref[0], g, row[:, :d], row[:, d:2 * d]).astype(BF16)

    @pl.when(pl.program_id(1) == 0)
    def _():
        rowc = mod_ref[ctx_row:ctx_row + 1, :]
        hc_ref[0] = _norm_mod(ctx_ref[0], g, rowc[:, :d], rowc[:, d:2 * d]).astype(BF16)


def _prenorm(x, ctx, mod, g):
    bsz, t, d = x.shape
    tc = ctx.shape[1]
    rb = 512
    return pl.pallas_call(
        functools.partial(_prenorm_kernel, d=d, ctx_row=bsz),
        grid=(bsz, t // rb),
        in_specs=[
            pl.BlockSpec((1, rb, d), lambda b, r: (b, r, 0)),
            pl.BlockSpec((1, tc, d), lambda b, r: (b, 0, 0)),
            pl.BlockSpec((MOD_ROWS, 3 * d), lambda b, r: (0, 0)),
            pl.BlockSpec((1, d), lambda b, r: (0, 0)),
        ],
        out_specs=[
            pl.BlockSpec((1, rb, d), lambda b, r: (b, r, 0)),
            pl.BlockSpec((1, tc, d), lambda b, r: (b, 0, 0)),
        ],
        out_shape=[jax.ShapeDtypeStruct((bsz, t, d), BF16),
                   jax.ShapeDtypeStruct((bsz, tc, d), BF16)],
        name="prenorm",
    )(x, ctx, mod, g.reshape(1, d))


def _hgrn_kernel(hx_ref, hc_ref, wq_ref, wv_ref, wf_ref, wb_ref, wg_ref, lbp_ref, ng_ref, tri_ref,
                 ox_ref, oc_ref,
                 q_b, v_b, lff_b, lfb_b, kf_b, kb_b,
                 qtf_s, qtb_s, sg_s, o_s, oif_s, oib_s, ut_s, em_s, st_s,
                 *, layer, tc, t, heads):
    n_rows = tc + t
    c = HG_CHUNK
    kd = HG_KEY_DIM
    cpb = ROW_BLOCK // c
    lf_b = (lff_b, lfb_b)
    k_b = (kf_b, kb_b)
    qt_s = (qtf_s, qtb_s)
    oi_s = (oif_s, oib_s)

    lbp = lbp_ref[...]
    ex = jnp.exp(lbp - jnp.max(lbp, axis=0, keepdims=True))
    lb = jnp.sum(ex[:layer + 1], axis=0, keepdims=True) / jnp.sum(ex, axis=0, keepdims=True)

    rid = lax.broadcasted_iota(jnp.int32, (c, c), 0)
    cid = lax.broadcasted_iota(jnp.int32, (c, c), 1)
    masks = (rid >= cid, rid <= cid)

    def split3(xv):
        hi = xv.astype(BF16)
        r1 = xv - hi.astype(F32)
        mid = r1.astype(BF16)
        lo = (r1 - mid.astype(F32)).astype(BF16)
        return jnp.concatenate([hi, mid, lo], axis=0)

    def project(hrows, r0, c0):
        rows = pl.ds(r0, ROW_BLOCK)
        q_b[...] = _silu(_dot(hrows, wq_ref[...]))
        v_b[...] = _dot(hrows, wv_ref[...]).astype(BF16)
        for w_ref, lf_ref, k_ref in ((wf_ref, lff_b, kf_b), (wb_ref, lfb_b, kb_b)):
            f = lb + (1.0 - lb) * jax.nn.sigmoid(_dot(hrows, w_ref[...]))
            lf_ref[...] = jnp.log(f)
            k_ref[...] = 1.0 - f
        sg_s[rows, :] = _silu(_dot(hrows, wg_ref[...])).astype(BF16)

        crs = [slice(cc * c, (cc + 1) * c) for cc in range(cpb)]
        bsums = [[_dot(tri_ref[d], split3(lf_b[d][cr, :])) for d in range(2)] for cr in crs]
        qts, kts = [], []
        for cc, cr in enumerate(crs):
            qv = q_b[cr, :]
            qt, kt = [], []
            for d in range(2):
                bsum = bsums[cc][d]
                m = 0.5 * (bsum[c - 1:c, :] if d == 0 else bsum[0:1, :])
                em_s[d, pl.ds(c0 + cc, 1), :] = jnp.exp(m)
                qt.append((qv * jnp.exp(bsum - m)).astype(BF16))
                kt.append((k_b[d][cr, :] * jnp.exp(m - bsum)).astype(BF16))
                qt_s[d][pl.ds(r0 + cc * c, c), :] = qt[d]
            qts.append(qt)
            kts.append(kt)
        scs = []
        for cc in range(cpb):
            for h in range(heads):
                cs = slice(h * kd, (h + 1) * kd)
                sf = lax.dot_general(qts[cc][0][:, cs], kts[cc][0][:, cs], _NT, preferred_element_type=F32)
                sb = lax.dot_general(qts[cc][1][:, cs], kts[cc][1][:, cs], _NT, preferred_element_type=F32)
                scs.append((jnp.where(masks[0], sf, 0.0) + jnp.where(masks[1], sb, 0.0)).astype(BF16))
        for cc, cr in enumerate(crs):
            for h in range(heads):
                cs = slice(h * kd, (h + 1) * kd)
                vv = v_b[cr, cs]
                o_s[pl.ds(r0 + cc * c, c), cs] = _dot(scs[cc * heads + h], vv)
                kk = jnp.concatenate([kts[cc][0][:, cs], kts[cc][1][:, cs]], axis=1)
                ut_s[c0 + cc, h] = lax.dot_general(vv, kk, _TN, preferred_element_type=F32)

    for i in range(tc // ROW_BLOCK):
        project(hc_ref[0, i * ROW_BLOCK:(i + 1) * ROW_BLOCK, :], i * ROW_BLOCK, i * cpb)

    def proj_body(i, carry):
        r0 = pl.multiple_of(i * ROW_BLOCK, ROW_BLOCK)
        project(hx_ref[0, pl.ds(r0, ROW_BLOCK), :], tc + r0, tc // c + i * cpb)
        return carry

    lax.fori_loop(0, t // ROW_BLOCK, proj_body, 0)

    st_s[...] = jnp.zeros_like(st_s)
    n_ctx = tc // c
    n_all = n_rows // c

    def scan_body(i, carry):
        units = [(d, h) for d in range(2) for h in range(heads)]
        st = {u: st_s[u[0], u[1]] for u in units}
        pending = []
        for k in range(SCAN_STEPS):
            s = i * SCAN_STEPS + k
            ci = (s, jnp.where(s < n_ctx, n_ctx - 1 - s, n_all + n_ctx - 1 - s))
            em_rows = [em_s[d, pl.ds(ci[d], 1), :] for d in range(2)]
            for d, h in units:
                cs = slice(h * kd, (h + 1) * kd)
                em = em_rows[d][:, cs]
                sm = st[(d, h)] * em
                st[(d, h)] = em * (sm + ut_s[ci[d], h][:, d * kd:(d + 1) * kd])
                pending.append((d, pl.ds(pl.multiple_of(ci[d] * c, c), c), cs, sm.astype(BF16)))
        outs = [lax.dot_general(qt_s[d][rows, cs], smb, _NT, preferred_element_type=F32)
                for d, rows, cs, smb in pending]
        for (d, rows, cs, _), oi in zip(pending, outs):
            oi_s[d][rows, cs] = oi
        for d, h in units:
            st_s[d, h] = st[(d, h)]
        return carry

    lax.fori_loop(0, n_all // SCAN_STEPS, scan_body, 0)

    ng = ng_ref[...]

    def readout(r0):
        rows = pl.ds(r0, ROW_BLOCK)
        o = o_s[rows, :] + oif_s[rows, :] + oib_s[rows, :]
        parts = []
        for h in range(heads):
            cs = slice(h * kd, (h + 1) * kd)
            oh = o[:, cs]
            ms = jnp.mean(oh * oh, axis=-1, keepdims=True)
            parts.append(oh * lax.rsqrt(ms + EPS) * ng[:, cs])
        return (jnp.concatenate(parts, axis=-1) * sg_s[rows, :].astype(F32)).astype(BF16)

    for i in range(tc // ROW_BLOCK):
        oc_ref[0, i * ROW_BLOCK:(i + 1) * ROW_BLOCK, :] = readout(i * ROW_BLOCK)

    def out_body(i, carry):
        r0 = pl.multiple_of(i * ROW_BLOCK, ROW_BLOCK)
        ox_ref[0, pl.ds(r0, ROW_BLOCK), :] = readout(tc + r0)
        return carry

    lax.fori_loop(0, t // ROW_BLOCK, out_body, 0)


def _hgrn(hx, hc, w_in, lower_bounds, norm_g, layer):
    bsz, t, d = hx.shape
    tc = hc.shape[1]
    e = w_in.shape[1] // 5
    heads = 2
    gw = heads * HG_KEY_DIM
    ng_blocks = e // gw
    n_rows = tc + t
    c = HG_CHUNK

    low = jnp.tril(jnp.ones((c, c), F32))
    tri = jnp.stack([jnp.tile(low, (1, 3)), jnp.tile(low.T, (1, 3))]).astype(BF16)

    def wspec(sec):
        return pl.BlockSpec((d, gw), lambda b, j, sec=sec: (0, sec * ng_blocks + j))

    seq_f32 = pltpu.VMEM((n_rows, gw), F32)
    seq_bf16 = pltpu.VMEM((n_rows, gw), BF16)
    blk_f32 = pltpu.VMEM((ROW_BLOCK, gw), F32)
    n_chunks = n_rows // c
    return pl.pallas_call(
        functools.partial(_hgrn_kernel, layer=layer, tc=tc, t=t, heads=heads),
        grid=(bsz, ng_blocks),
        in_specs=[
            pl.BlockSpec((1, t, d), lambda b, j: (b, 0, 0)),
            pl.BlockSpec((1, tc, d), lambda b, j: (b, 0, 0)),
            wspec(0), wspec(1), wspec(2), wspec(3), wspec(4),
            pl.BlockSpec((lower_bounds.shape[0], gw), lambda b, j: (0, j)),
            pl.BlockSpec((1, gw), lambda b, j: (0, j)),
            pl.BlockSpec((2, c, 3 * c), lambda b, j: (0, 0, 0)),
        ],
        out_specs=[
            pl.BlockSpec((1, t, gw), lambda b, j: (b, 0, j)),
            pl.BlockSpec((1, tc, gw), lambda b, j: (b, 0, j)),
        ],
        out_shape=[jax.ShapeDtypeStruct((bsz, t, e), BF16),
                   jax.ShapeDtypeStruct((bsz, tc, e), BF16)],
        scratch_shapes=[
            blk_f32, pltpu.VMEM((ROW_BLOCK, gw), BF16),
            blk_f32, blk_f32, blk_f32, blk_f32,
            seq_bf16, seq_bf16,
            seq_bf16,
            seq_f32, seq_f32, seq_f32,
            pltpu.VMEM((n_chunks, heads, HG_KEY_DIM, 2 * HG_KEY_DIM), F32),
            pltpu.VMEM((2, n_chunks, gw), F32),
            pltpu.VMEM((2, heads, HG_KEY_DIM, HG_KEY_DIM), F32),
        ],
        compiler_params=pltpu.CompilerParams(
            dimension_semantics=("arbitrary", "arbitrary"), vmem_limit_bytes=VMEM_LIMIT),
        name="hgrn2",
    )(hx, hc, w_in, w_in, w_in, w_in, w_in, lower_bounds, norm_g.reshape(1, e), tri)


def _outproj_mid_kernel(ogx_ref, ogc_ref, x_ref, ctx_ref, w_ref, mod0_ref, mod1_ref, g1_ref,
                        x1_ref, hx_ref, hc_ref, *, d, ctx_row):
    b = pl.program_id(0)
    g1 = g1_ref[...]
    row0 = mod0_ref[pl.ds(b, 1), :]
    row1 = mod1_ref[pl.ds(b, 1), :]
    x1 = x_ref[0] + row0[:, 2 * d:] * _dot(ogx_ref[0], w_ref[...])
    x1_ref[0] = x1
    hx_ref[0] = _norm_mod(x1, g1, row1[:, :d], row1[:, d:2 * d]).astype(BF16)

    @pl.when(pl.program_id(1) == 0)
    def _():
        c0 = mod0_ref[ctx_row:ctx_row + 1, :]
        c1 = mod1_ref[ctx_row:ctx_row + 1, :]
        ctx1 = ctx_ref[0] + c0[:, 2 * d:] * _dot(ogc_ref[0], w_ref[...])
        hc_ref[0] = _norm_mod(ctx1, g1, c1[:, :d], c1[:, d:2 * d]).astype(BF16)


def _outproj_mid(ogx, ogc, x, ctx, w_out, mod0, mod1, g1):
    bsz, t, d = x.shape
    tc = ctx.shape[1]
    e = w_out.shape[0]
    rb = 512
    return pl.pallas_call(
        functools.partial(_outproj_mid_kernel, d=d, ctx_row=bsz),
        grid=(bsz, t // rb),
        in_specs=[
            pl.BlockSpec((1, rb, e), lambda b, r: (b, r, 0)),
            pl.BlockSpec((1, tc, e), lambda b, r: (b, 0, 0)),
            pl.BlockSpec((1, rb, d), lambda b, r: (b, r, 0)),
            pl.BlockSpec((1, tc, d), lambda b, r: (b, 0, 0)),
            pl.BlockSpec((e, d), lambda b, r: (0, 0)),
            pl.BlockSpec((MOD_ROWS, 3 * d), lambda b, r: (0, 0)),
            pl.BlockSpec((MOD_ROWS, 3 * d), lambda b, r: (0, 0)),
            pl.BlockSpec((1, d), lambda b, r: (0, 0)),
        ],
        out_specs=[
            pl.BlockSpec((1, rb, d), lambda b, r: (b, r, 0)),
            pl.BlockSpec((1, rb, d), lambda b, r: (b, r, 0)),
            pl.BlockSpec((1, tc, d), lambda b, r: (b, 0, 0)),
        ],
        out_shape=[jax.ShapeDtypeStruct((bsz, t, d), F32),
                   jax.ShapeDtypeStruct((bsz, t, d), BF16),
                   jax.ShapeDtypeStruct((bsz, tc, d), BF16)],
        compiler_params=pltpu.CompilerParams(
            dimension_semantics=("arbitrary", "arbitrary"), vmem_limit_bytes=VMEM_LIMIT),
        name="outproj0",
    )(ogx, ogc, x, ctx, w_out, mod0, mod1, g1.reshape(1, d))


def _outproj_final_kernel(yg_ref, x_ref, w_ref, mod_ref, g_ref, o_ref, *, d):
    b = pl.program_id(0)
    row = mod_ref[pl.ds(b, 1), :]
    x2 = x_ref[0] + row[:, 2 * d:] * _dot(yg_ref[0], w_ref[...])
    ms = jnp.mean(x2 * x2, axis=-1, keepdims=True)
    o_ref[0] = x2 * lax.rsqrt(ms + EPS) * g_ref[...]


def _outproj_final(yg, x1, w_out, mod, g):
    bsz, t, d = x1.shape
    e = w_out.shape[0]
    rb = 512
    return pl.pallas_call(
        functools.partial(_outproj_final_kernel, d=d),
        grid=(bsz, t // rb),
        in_specs=[
            pl.BlockSpec((1, rb, e), lambda b, r: (b, r, 0)),
            pl.BlockSpec((1, rb, d), lambda b, r: (b, r, 0)),
            pl.BlockSpec((e, d), lambda b, r: (0, 0)),
            pl.BlockSpec((MOD_ROWS, 3 * d), lambda b, r: (0, 0)),
            pl.BlockSpec((1, d), lambda b, r: (0, 0)),
        ],
        out_specs=pl.BlockSpec((1, rb, d), lambda b, r: (b, r, 0)),
        out_shape=jax.ShapeDtypeStruct((bsz, t, d), F32),
        compiler_params=pltpu.CompilerParams(
            dimension_semantics=("arbitrary", "arbitrary"), vmem_limit_bytes=VMEM_LIMIT),
        name="outproj1",
    )(yg, x1, w_out, mod, g.reshape(1, d))


def _shift_rows(xv, k):
    n = xv.shape[0]
    rid = lax.broadcasted_iota(jnp.int32, xv.shape, 0)
    rolled = pltpu.roll(xv, k % n, 0)
    valid = (rid >= k) if k > 0 else (rid < n + k)
    return jnp.where(valid, rolled, 0.0)


def _tile_scan(a, u, reverse):
    n = a.shape[0]
    rid = lax.broadcasted_iota(jnp.int32, a.shape, 0)
    s = 1
    while s < n:
        valid = (rid < n - s) if reverse else (rid >= s)
        sh = (n - s) if reverse else s
        a_sh = jnp.where(valid, pltpu.roll(a, sh, 0), 1.0)
        u_sh = jnp.where(valid, pltpu.roll(u, sh, 0), 0.0)
        u = u + a * u_sh
        a = a * a_sh
        s *= 2
    return a, u


def _rglru_kernel(hx_ref, hc_ref, wx_ref, wg_ref, cw_ref, cb_ref, wa_ref, wi_ref, ba_ref, bi_ref,
                  lam_ref, y_ref,
                  xp_s, xc_s, sg_s, af_s, uf_s, ab_s, ub_s, cf_s, cr_s,
                  *, tc, t):
    gw = RG_BLOCK
    gr = GRID_W
    n_tr = t // gr
    a_refs = (af_s, ab_s)
    u_refs = (uf_s, ub_s)
    cw = cw_ref[...]
    cb = cb_ref[...]

    def conv_taps(xm2, xm1, x0, xp1):
        return cw[0:1] * xm2 + cw[1:2] * xm1 + cw[2:3] * x0 + cw[3:4] * xp1 + cb

    xcx = _dot(hc_ref[0], wx_ref[...])
    xc_s[0:tc, :] = conv_taps(_shift_rows(xcx, 2), _shift_rows(xcx, 1), xcx, _shift_rows(xcx, -1))

    pad = 2 * gr

    def proj_body(i, carry):
        r0 = pl.multiple_of(i * ROW_BLOCK, ROW_BLOCK)
        hrows = hx_ref[0, pl.ds(r0, ROW_BLOCK), :]
        xp_s[pl.ds(pad + r0, ROW_BLOCK), :] = _dot(hrows, wx_ref[...])
        sg_s[pl.ds(r0, ROW_BLOCK), :] = _silu(_dot(hrows, wg_ref[...])).astype(BF16)
        return carry

    lax.fori_loop(0, t // ROW_BLOCK, proj_body, 0)

    xp_s[0:gr, :] = _shift_rows(xp_s[pad + (n_tr - 2) * gr:pad + (n_tr - 1) * gr, :], 1)
    xp_s[gr:pad, :] = _shift_rows(xp_s[pad + (n_tr - 1) * gr:pad + n_tr * gr, :], 1)
    xp_s[pad + n_tr * gr:pad + (n_tr + 1) * gr, :] = _shift_rows(xp_s[pad:pad + gr, :], -1)

    def conv_body(r, carry):
        r0 = pl.multiple_of(r * gr, gr)
        xc_s[pl.ds(tc + r0, gr), :] = conv_taps(
            xp_s[pl.ds(r0, gr), :], xp_s[pl.ds(r0 + gr, gr), :],
            xp_s[pl.ds(r0 + 2 * gr, gr), :], xp_s[pl.ds(r0 + 3 * gr, gr), :])
        return carry

    lax.fori_loop(0, n_tr, conv_body, 0)

    nsp = -RG_C * jax.nn.softplus(-lam_ref[...])

    def gate_body(i, carry):
        r0 = pl.multiple_of(i * ROW_BLOCK, ROW_BLOCK)
        rows = pl.ds(r0, ROW_BLOCK)
        xc = xc_s[rows, :]
        xcb = xc.astype(BF16)
        for d in range(2):
            rg = jax.nn.sigmoid(_dot(xcb, wa_ref[d, 0]) + ba_ref[d:d + 1, :])
            ig = jax.nn.sigmoid(_dot(xcb, wi_ref[d, 0]) + bi_ref[d:d + 1, :])
            log_a = rg * nsp[d:d + 1, :]
            a = jnp.exp(log_a)
            a_refs[d][rows, :] = a
            u_refs[d][rows, :] = jnp.sqrt(-jnp.tanh(log_a) * (a * a + 1.0)) * (ig * xc)
        return carry

    lax.fori_loop(0, (tc + t) // ROW_BLOCK, gate_body, 0)

    n_ct = tc // 8

    def ctx_body(i, carry):
        hf, hb = carry
        rf = pl.ds(pl.multiple_of(i * 8, 8), 8)
        a, u = _tile_scan(af_s[rf, :], uf_s[rf, :], False)
        hf = a[7:8] * hf + u[7:8]
        rb = pl.ds(pl.multiple_of((n_ct - 1 - i) * 8, 8), 8)
        a, u = _tile_scan(ab_s[rb, :], ub_s[rb, :], True)
        hb = a[0:1] * hb + u[0:1]
        return hf, hb

    zero = jnp.zeros((1, gw), F32)
    hf, hb = lax.fori_loop(0, n_ct, ctx_body, (zero, zero))

    def local_scan(d):
        def body(i, carry):
            hh, aa = carry
            r = i if d == 0 else n_tr - 1 - i
            rows = pl.ds(tc + pl.multiple_of(r * gr, gr), gr)
            a = a_refs[d][rows, :]
            hh = a * hh + u_refs[d][rows, :]
            aa = a * aa
            u_refs[d][rows, :] = hh
            a_refs[d][rows, :] = aa
            return hh, aa

        lax.fori_loop(0, n_tr, body, (jnp.zeros((gr, gw), F32), jnp.ones((gr, gw), F32)))

    local_scan(0)
    local_scan(1)

    def carry_body(i, carry):
        cf, cr = carry
        cf_s[pl.ds(i, 1), :] = cf
        last = tc + (n_tr - 1) * gr
        cf = uf_s[pl.ds(last + i, 1), :] + af_s[pl.ds(last + i, 1), :] * cf
        w = gr - 1 - i
        cr_s[pl.ds(w, 1), :] = cr
        cr = ub_s[pl.ds(tc + w, 1), :] + ab_s[pl.ds(tc + w, 1), :] * cr
        return cf, cr

    lax.fori_loop(0, gr, carry_body, (hf, hb))

    def out_body(r, carry):
        r0 = pl.multiple_of(r * gr, gr)
        rows = pl.ds(tc + r0, gr)
        h = (uf_s[rows, :] + af_s[rows, :] * cf_s[...]) + (ub_s[rows, :] + ab_s[rows, :] * cr_s[...])
        y_ref[0, pl.ds(r0, gr), :] = (h * sg_s[pl.ds(r0, gr), :].astype(F32)).astype(BF16)
        return carry

    lax.fori_loop(0, n_tr, out_body, 0)


def _rglru(hx, hc, w_in, conv_w, conv_b, w_a, b_a, w_x, b_x, lam):
    bsz, t, d = hx.shape
    tc = hc.shape[1]
    e = w_in.shape[1] // 2
    gw = RG_BLOCK
    nb = e // gw
    n_rows = tc + t
    seq_f32 = pltpu.VMEM((n_rows, gw), F32)
    vec = lambda rows: pl.BlockSpec((rows, gw), lambda b, j: (0, j))
    gate_w = pl.BlockSpec((2, 1, gw, gw), lambda b, j: (0, j, 0, 0))
    return pl.pallas_call(
        functools.partial(_rglru_kernel, tc=tc, t=t),
        grid=(bsz, nb),
        in_specs=[
            pl.BlockSpec((1, t, d), lambda b, j: (b, 0, 0)),
            pl.BlockSpec((1, tc, d), lambda b, j: (b, 0, 0)),
            pl.BlockSpec((d, gw), lambda b, j: (0, j)),
            pl.BlockSpec((d, gw), lambda b, j: (0, nb + j)),
            vec(conv_w.shape[0]), vec(1),
            gate_w, gate_w, vec(2), vec(2), vec(2),
        ],
        out_specs=pl.BlockSpec((1, t, gw), lambda b, j: (b, 0, j)),
        out_shape=jax.ShapeDtypeStruct((bsz, t, e), BF16),
        scratch_shapes=[
            pltpu.VMEM((t + 3 * GRID_W, gw), F32),
            seq_f32,
            pltpu.VMEM((t, gw), BF16),
            seq_f32, seq_f32, seq_f32, seq_f32,
            pltpu.VMEM((GRID_W, gw), F32), pltpu.VMEM((GRID_W, gw), F32),
        ],
        compiler_params=pltpu.CompilerParams(
            dimension_semantics=("arbitrary", "arbitrary"), vmem_limit_bytes=VMEM_LIMIT),
        name="rglru",
    )(hx, hc, w_in, w_in, conv_w, conv_b.reshape(1, e), w_a, w_x, b_a, b_x, lam)


def kernel(x, c, ctx, c_ctx, ada_w, ada_b, norm_g, final_norm_g, hg_w_in, hg_lower_bounds,
           hg_norm_g, hg_w_out, rg_w_in, rg_conv_w, rg_conv_b, rg_w_a, rg_b_a, rg_w_x, rg_b_x,
           rg_lambda, rg_w_out):
    bsz, t, d = x.shape
    assert ada_w.shape[0] == 2 and bsz < MOD_ROWS and t % GRID_W == 0
    cond = jnp.concatenate(
        [c, c_ctx[None, :], jnp.zeros((MOD_ROWS - bsz - 1, d), F32)], axis=0)
    mod = _ada(cond, ada_w, ada_b)

    hx, hc = _prenorm(x, ctx, mod[0], norm_g[0])
    ogx, ogc = _hgrn(hx, hc, hg_w_in[0].astype(BF16), hg_lower_bounds, hg_norm_g[0], layer=0)
    x1, hx1, hc1 = _outproj_mid(ogx, ogc, x, ctx, hg_w_out[0].astype(BF16), mod[0], mod[1], norm_g[1])
    yg = _rglru(hx1, hc1, rg_w_in[0].astype(BF16), rg_conv_w[0], rg_conv_b[0],
                rg_w_a[0].astype(BF16), rg_b_a[0], rg_w_x[0].astype(BF16), rg_b_x[0], rg_lambda[0])
    return _outproj_final(yg, x1, rg_w_out[0].astype(BF16), mod[1], final_norm_g)
```

```python
import functools

import jax
import jax.numpy as jnp
from jax import lax
from jax.experimental import pallas as pl
from jax.experimental.pallas import tpu as pltpu

F32 = jnp.float32
BF16 = jnp.bfloat16

EPS = 1e-6
F32_TINY = 1.1754944e-38
GRID_W = 64
HG_KEY_DIM = 128
HG_CHUNK = 64
RG_BLOCK = 256
RG_C = 8.0
MOD_ROWS = 16
ROW_BLOCK = 256
HG_BLOCK = 512
SCAN_STEPS = 6
VMEM_LIMIT = 56 * 1024 * 1024

_NT = (((1,), (1,)), ((), ()))
_TN = (((0,), (0,)), ((), ()))


def _half_tanh(z):
    return jnp.tanh(0.5 * z)


def _silu(z):
    hz = 0.5 * z
    return hz + hz * jnp.tanh(hz)


def _dot(a, b):
    return jnp.dot(a, b, preferred_element_type=F32)


def _ada_kernel(c_ref, w_ref, b_ref, o_ref):
    s = _silu(c_ref[...])
    o_ref[0] = jnp.dot(s, w_ref[0], preferred_element_type=F32,
                       precision=lax.Precision.HIGHEST) + b_ref[0]


def _ada(cond, ada_w, ada_b):
    depth, d, n3 = ada_w.shape
    bn = d
    return pl.pallas_call(
        _ada_kernel,
        grid=(depth, n3 // bn),
        in_specs=[
            pl.BlockSpec((MOD_ROWS, d), lambda i, n: (0, 0)),
            pl.BlockSpec((1, d, bn), lambda i, n: (i, 0, n)),
            pl.BlockSpec((1, 1, bn), lambda i, n: (i, 0, n)),
        ],
        out_specs=pl.BlockSpec((1, MOD_ROWS, bn), lambda i, n: (i, 0, n)),
        out_shape=jax.ShapeDtypeStruct((depth, MOD_ROWS, n3), F32),
        name="ada",
    )(cond, ada_w, ada_b.reshape(depth, 1, n3))


def _norm_mod(xv, g, shift, scale):
    ms = jnp.mean(xv * xv, axis=-1, keepdims=True)
    return xv * lax.rsqrt(ms + EPS) * g * (1.0 + scale) + shift


def _prenorm_kernel(x_ref, ctx_ref, mod_ref, g_ref, hx_ref, hc_ref, *, d, ctx_row):
    b = pl.program_id(0)
    g = g_ref[...]
    row = mod_ref[pl.ds(b, 1), :]
    hx_ref[0] = _norm_mod(x_ref[0], g, row[:, :d], row[:, d:2 * d]).astype(BF16)

    @pl.when(pl.program_id(1) == 0)
    def _():
        rowc = mod_ref[ctx_row:ctx_row + 1, :]
        hc_ref[0] = _norm_mod(ctx_ref[0], g, rowc[:, :d], rowc[:, d:2 * d]).astype(BF16)


def _prenorm(x, ctx, mod, g):
    bsz, t, d = x.shape
    tc = ctx.shape[1]
    rb = 512
    return pl.pallas_call(
        functools.partial(_prenorm_kernel, d=d, ctx_row=bsz),
        grid=(bsz, t // rb),
        in_specs=[
            pl.BlockSpec((1, rb, d), lambda b, r: (b, r, 0)),
            pl.BlockSpec((1, tc, d), lambda b, r: (b, 0, 0)),
            pl.BlockSpec((MOD_ROWS, 3 * d), lambda b, r: (0, 0)),
            pl.BlockSpec((1, d), lambda b, r: (0, 0)),
        ],
        out_specs=[
            pl.BlockSpec((1, rb, d), lambda b, r: (b, r, 0)),
            pl.BlockSpec((1, tc, d), lambda b, r: (b, 0, 0)),
        ],
        out_shape=[jax.ShapeDtypeStruct((bsz, t, d), BF16),
                   jax.ShapeDtypeStruct((bsz, tc, d), BF16)],
        name="prenorm",
    )(x, ctx, mod, g.reshape(1, d))


def _hgrn_kernel(hx_ref, hc_ref, wq_ref, wv_ref, wf_ref, wb_ref, wg_ref, lbp_ref, ng_ref, tri_ref,
                 ox_ref, oc_ref,
                 qa_b, va_b, lffa_b, lfba_b, kfa_b, kba_b, qb_b, vb_b, lffb_b, lfbb_b, kfb_b, kbb_b,
                 qtf_s, qtb_s, sg_s, o_s, oif_s, oib_s, ut_s, em_s, st_s,
                 *, layer, tc, t, heads):
    n_rows = tc + t
    c = HG_CHUNK
    kd = HG_KEY_DIM
    qt_s = (qtf_s, qtb_s)
    oi_s = (oif_s, oib_s)

    lbp = lbp_ref[...]
    ex = jnp.exp(lbp - jnp.max(lbp, axis=0, keepdims=True))
    lb = jnp.sum(ex[:layer + 1], axis=0, keepdims=True) / jnp.sum(ex, axis=0, keepdims=True)
    f_mid = 0.5 * (1.0 + lb)
    f_half = 0.5 * (1.0 - lb)

    rid = lax.broadcasted_iota(jnp.int32, (c, c), 0)
    cid = lax.broadcasted_iota(jnp.int32, (c, c), 1)
    masks = (rid >= cid, rid <= cid)

    def split3(xv):
        hi = xv.astype(BF16)
        r1 = xv - hi.astype(F32)
        mid = r1.astype(BF16)
        lo = (r1 - mid.astype(F32)).astype(BF16)
        return jnp.concatenate([hi, mid, lo], axis=0)

    def proj_steps(hrows, buf, r0):
        nr = hrows.shape[0]
        q_b, v_b, lf_b, k_b = buf

        def step_q():
            q_b[0:nr, :] = _silu(_dot(hrows, wq_ref[...]))

        def step_v():
            v_b[0:nr, :] = _dot(hrows, wv_ref[...]).astype(BF16)

        def step_f(d):
            f = f_mid + f_half * _half_tanh(_dot(hrows, (wf_ref, wb_ref)[d][...]))
            lf_b[d][0:nr, :] = jnp.log(f)
            k_b[d][0:nr, :] = 1.0 - f

        def step_g():
            sg_s[pl.ds(r0, nr), :] = _silu(_dot(hrows, wg_ref[...])).astype(BF16)

        return [step_q, step_v, functools.partial(step_f, 0), functools.partial(step_f, 1), step_g]

    def chunk_steps(buf, nr, r0, c0):
        q_b, v_b, lf_b, k_b = buf
        crs = [slice(cc * c, (cc + 1) * c) for cc in range(nr // c)]
        vals = {}

        def step_cumsum():
            vals["b"] = [[_dot(tri_ref[d], split3(lf_b[d][cr, :])) for d in range(2)] for cr in crs]

        def step_scale():
            qts, kts = [], []
            for cc, cr in enumerate(crs):
                qv = q_b[cr, :]
                qt, kt = [], []
                for d in range(2):
                    bsum = vals["b"][cc][d]
                    m = 0.5 * (bsum[c - 1:c, :] if d == 0 else bsum[0:1, :])
                    em_s[d, pl.ds(c0 + cc, 1), :] = jnp.exp(m)
                    qt.append((qv * jnp.exp(bsum - m)).astype(BF16))
                    kt.append((k_b[d][cr, :] * jnp.exp(m - bsum)).astype(BF16))
                    qt_s[d][pl.ds(r0 + cc * c, c), :] = qt[d]
                qts.append(qt)
                kts.append(kt)
            vals["qt"], vals["kt"] = qts, kts

        def step_scores():
            scs = []
            for cc in range(len(crs)):
                for h in range(heads):
                    cs = slice(h * kd, (h + 1) * kd)
                    qt, kt = vals["qt"][cc], vals["kt"][cc]
                    sf = lax.dot_general(qt[0][:, cs], kt[0][:, cs], _NT, preferred_element_type=F32)
                    sb = lax.dot_general(qt[1][:, cs], kt[1][:, cs], _NT, preferred_element_type=F32)
                    scs.append((jnp.where(masks[0], sf, 0.0) + jnp.where(masks[1], sb, 0.0)).astype(BF16))
            vals["sc"] = scs

        def step_out():
            for cc, cr in enumerate(crs):
                for h in range(heads):
                    cs = slice(h * kd, (h + 1) * kd)
                    vv = v_b[cr, cs]
                    kt = vals["kt"][cc]
                    o_s[pl.ds(r0 + cc * c, c), cs] = _dot(vals["sc"][cc * heads + h], vv)
                    kk = jnp.concatenate([kt[0][:, cs], kt[1][:, cs]], axis=1)
                    ut_s[c0 + cc, h] = lax.dot_general(vv, kk, _TN, preferred_element_type=F32)

        return [step_cumsum, step_scale, step_scores, step_out]

    def interleave(proj, chunks):
        for k, step in enumerate(proj):
            step()
            if k < len(chunks):
                chunks[k]()

    buf_a = (qa_b, va_b, (lffa_b, lfba_b), (kfa_b, kba_b))
    buf_b = (qb_b, vb_b, (lffb_b, lfbb_b), (kfb_b, kbb_b))
    nb = t // HG_BLOCK
    assert nb % 2 == 0

    def lat_rows(i):
        return hx_ref[0, pl.ds(pl.multiple_of(i * HG_BLOCK, HG_BLOCK), HG_BLOCK), :]

    def lat_r0(i):
        return tc + pl.multiple_of(i * HG_BLOCK, HG_BLOCK)

    interleave(proj_steps(hc_ref[0], buf_a, 0), [])
    interleave(proj_steps(lat_rows(0), buf_b, lat_r0(0)), chunk_steps(buf_a, tc, 0, 0))

    def pipe_body(i, carry):
        i1 = 2 * i + 1
        interleave(proj_steps(lat_rows(i1), buf_a, lat_r0(i1)),
                   chunk_steps(buf_b, HG_BLOCK, lat_r0(i1 - 1), lat_r0(i1 - 1) // c))
        interleave(proj_steps(lat_rows(i1 + 1), buf_b, lat_r0(i1 + 1)),
                   chunk_steps(buf_a, HG_BLOCK, lat_r0(i1), lat_r0(i1) // c))
        return carry

    lax.fori_loop(0, nb // 2 - 1, pipe_body, 0)
    last = nb - 1
    interleave(proj_steps(lat_rows(last), buf_a, lat_r0(last)),
               chunk_steps(buf_b, HG_BLOCK, lat_r0(last - 1), lat_r0(last - 1) // c))
    for step in chunk_steps(buf_a, HG_BLOCK, lat_r0(last), lat_r0(last) // c):
        step()

    st_s[...] = jnp.zeros_like(st_s)
    n_ctx = tc // c
    n_all = n_rows // c

    def scan_body(i, carry):
        units = [(d, h) for d in range(2) for h in range(heads)]
        st = {u: st_s[u[0], u[1]] for u in units}
        pending = []
        for k in range(SCAN_STEPS):
            s = i * SCAN_STEPS + k
            ci = (s, jnp.where(s < n_ctx, n_ctx - 1 - s, n_all + n_ctx - 1 - s))
            em_rows = [em_s[d, pl.ds(ci[d], 1), :] for d in range(2)]
            for d, h in units:
                cs = slice(h * kd, (h + 1) * kd)
                em = em_rows[d][:, cs]
                sm = st[(d, h)] * em
                st[(d, h)] = em * (sm + ut_s[ci[d], h][:, d * kd:(d + 1) * kd])
                rows = pl.ds(pl.multiple_of(ci[d] * c, c), c)
                pending.append((d, rows, cs, lax.dot_general(
                    qt_s[d][rows, cs], sm.astype(BF16), _NT, preferred_element_type=F32)))
        for d, rows, cs, oi in pending:
            oi_s[d][rows, cs] = oi
        for d, h in units:
            st_s[d, h] = st[(d, h)]
        return carry

    lax.fori_loop(0, n_all // SCAN_STEPS, scan_body, 0)

    ng = ng_ref[...]

    def readout(r0):
        rows = pl.ds(r0, ROW_BLOCK)
        o = o_s[rows, :] + oif_s[rows, :] + oib_s[rows, :]
        parts = []
        for h in range(heads):
            cs = slice(h * kd, (h + 1) * kd)
            oh = o[:, cs]
            ms = jnp.mean(oh * oh, axis=-1, keepdims=True)
            parts.append(oh * lax.rsqrt(ms + EPS) * ng[:, cs])
        return (jnp.concatenate(parts, axis=-1) * sg_s[rows, :].astype(F32)).astype(BF16)

    for i in range(tc // ROW_BLOCK):
        oc_ref[0, i * ROW_BLOCK:(i + 1) * ROW_BLOCK, :] = readout(i * ROW_BLOCK)

    def out_body(i, carry):
        r0 = pl.multiple_of(i * ROW_BLOCK, ROW_BLOCK)
        ox_ref[0, pl.ds(r0, ROW_BLOCK), :] = readout(tc + r0)
        return carry

    lax.fori_loop(0, t // ROW_BLOCK, out_body, 0)


def _hgrn(hx, hc, w_in, lower_bounds, norm_g, layer):
    bsz, t, d = hx.shape
    tc = hc.shape[1]
    e = w_in.shape[1] // 5
    heads = 2
    gw = heads * HG_KEY_DIM
    ng_blocks = e // gw
    n_rows = tc + t
    c = HG_CHUNK

    low = jnp.tril(jnp.ones((c, c), F32))
    tri = jnp.stack([jnp.tile(low, (1, 3)), jnp.tile(low.T, (1, 3))]).astype(BF16)

    def wspec(sec):
        return pl.BlockSpec((d, gw), lambda b, j, sec=sec: (0, sec * ng_blocks + j))

    seq_f32 = pltpu.VMEM((n_rows, gw), F32)
    seq_bf16 = pltpu.VMEM((n_rows, gw), BF16)
    blk_f32 = pltpu.VMEM((HG_BLOCK, gw), F32)
    blk_bf16 = pltpu.VMEM((HG_BLOCK, gw), BF16)
    n_chunks = n_rows // c
    return pl.pallas_call(
        functools.partial(_hgrn_kernel, layer=layer, tc=tc, t=t, heads=heads),
        grid=(bsz, ng_blocks),
        in_specs=[
            pl.BlockSpec((1, t, d), lambda b, j: (b, 0, 0)),
            pl.BlockSpec((1, tc, d), lambda b, j: (b, 0, 0)),
            wspec(0), wspec(1), wspec(2), wspec(3), wspec(4),
            pl.BlockSpec((lower_bounds.shape[0], gw), lambda b, j: (0, j)),
            pl.BlockSpec((1, gw), lambda b, j: (0, j)),
            pl.BlockSpec((2, c, 3 * c), lambda b, j: (0, 0, 0)),
        ],
        out_specs=[
            pl.BlockSpec((1, t, gw), lambda b, j: (b, 0, j)),
            pl.BlockSpec((1, tc, gw), lambda b, j: (b, 0, j)),
        ],
        out_shape=[jax.ShapeDtypeStruct((bsz, t, e), BF16),
                   jax.ShapeDtypeStruct((bsz, tc, e), BF16)],
        scratch_shapes=[
            blk_f32, blk_bf16, blk_f32, blk_f32, blk_f32, blk_f32,
            blk_f32, blk_bf16, blk_f32, blk_f32, blk_f32, blk_f32,
            seq_bf16, seq_bf16,
            seq_bf16,
            seq_f32, seq_f32, seq_f32,
            pltpu.VMEM((n_chunks, heads, HG_KEY_DIM, 2 * HG_KEY_DIM), F32),
            pltpu.VMEM((2, n_chunks, gw), F32),
            pltpu.VMEM((2, heads, HG_KEY_DIM, HG_KEY_DIM), F32),
        ],
        compiler_params=pltpu.CompilerParams(
            dimension_semantics=("arbitrary", "arbitrary"), vmem_limit_bytes=VMEM_LIMIT),
        name="hgrn2",
    )(hx, hc, w_in, w_in, w_in, w_in, w_in, lower_bounds, norm_g.reshape(1, e), tri)


def _outproj_mid_kernel(ogx_ref, ogc_ref, x_ref, ctx_ref, w_ref, mod0_ref, mod1_ref, g1_ref,
                        x1_ref, hx_ref, hc_ref, *, d, ctx_row):
    b = pl.program_id(0)
    g1 = g1_ref[...]
    row0 = mod0_ref[pl.ds(b, 1), :]
    row1 = mod1_ref[pl.ds(b, 1), :]
    x1 = x_ref[0] + row0[:, 2 * d:] * _dot(ogx_ref[0], w_ref[...])
    x1_ref[0] = x1
    hx_ref[0] = _norm_mod(x1, g1, row1[:, :d], row1[:, d:2 * d]).astype(BF16)

    @pl.when(pl.program_id(1) == 0)
    def _():
        c0 = mod0_ref[ctx_row:ctx_row + 1, :]
        c1 = mod1_ref[ctx_row:ctx_row + 1, :]
        ctx1 = ctx_ref[0] + c0[:, 2 * d:] * _dot(ogc_ref[0], w_ref[...])
        hc_ref[0] = _norm_mod(ctx1, g1, c1[:, :d], c1[:, d:2 * d]).astype(BF16)


def _outproj_mid(ogx, ogc, x, ctx, w_out, mod0, mod1, g1):
    bsz, t, d = x.shape
    tc = ctx.shape[1]
    e = w_out.shape[0]
    rb = 512
    return pl.pallas_call(
        functools.partial(_outproj_mid_kernel, d=d, ctx_row=bsz),
        grid=(bsz, t // rb),
        in_specs=[
            pl.BlockSpec((1, rb, e), lambda b, r: (b, r, 0)),
            pl.BlockSpec((1, tc, e), lambda b, r: (b, 0, 0)),
            pl.BlockSpec((1, rb, d), lambda b, r: (b, r, 0)),
            pl.BlockSpec((1, tc, d), lambda b, r: (b, 0, 0)),
            pl.BlockSpec((e, d), lambda b, r: (0, 0)),
            pl.BlockSpec((MOD_ROWS, 3 * d), lambda b, r: (0, 0)),
            pl.BlockSpec((MOD_ROWS, 3 * d), lambda b, r: (0, 0)),
            pl.BlockSpec((1, d), lambda b, r: (0, 0)),
        ],
        out_specs=[
            pl.BlockSpec((1, rb, d), lambda b, r: (b, r, 0)),
            pl.BlockSpec((1, rb, d), lambda b, r: (b, r, 0)),
            pl.BlockSpec((1, tc, d), lambda b, r: (b, 0, 0)),
        ],
        out_shape=[jax.ShapeDtypeStruct((bsz, t, d), F32),
                   jax.ShapeDtypeStruct((bsz, t, d), BF16),
                   jax.ShapeDtypeStruct((bsz, tc, d), BF16)],
        compiler_params=pltpu.CompilerParams(
            dimension_semantics=("arbitrary", "arbitrary"), vmem_limit_bytes=VMEM_LIMIT),
        name="outproj0",
    )(ogx, ogc, x, ctx, w_out, mod0, mod1, g1.reshape(1, d))


def _outproj_final_kernel(yg_ref, x_ref, w_ref, mod_ref, g_ref, o_ref, *, d):
    b = pl.program_id(0)
    row = mod_ref[pl.ds(b, 1), :]
    x2 = x_ref[0] + row[:, 2 * d:] * _dot(yg_ref[0], w_ref[...])
    ms = jnp.mean(x2 * x2, axis=-1, keepdims=True)
    o_ref[0] = x2 * lax.rsqrt(ms + EPS) * g_ref[...]


def _outproj_final(yg, x1, w_out, mod, g):
    bsz, t, d = x1.shape
    e = w_out.shape[0]
    rb = 512
    return pl.pallas_call(
        functools.partial(_outproj_final_kernel, d=d),
        grid=(bsz, t // rb),
        in_specs=[
            pl.BlockSpec((1, rb, e), lambda b, r: (b, r, 0)),
            pl.BlockSpec((1, rb, d), lambda b, r: (b, r, 0)),
            pl.BlockSpec((e, d), lambda b, r: (0, 0)),
            pl.BlockSpec((MOD_ROWS, 3 * d), lambda b, r: (0, 0)),
            pl.BlockSpec((1, d), lambda b, r: (0, 0)),
        ],
        out_specs=pl.BlockSpec((1, rb, d), lambda b, r: (b, r, 0)),
        out_shape=jax.ShapeDtypeStruct((bsz, t, d), F32),
        compiler_params=pltpu.CompilerParams(
            dimension_semantics=("arbitrary", "arbitrary"), vmem_limit_bytes=VMEM_LIMIT),
        name="outproj1",
    )(yg, x1, w_out, mod, g.reshape(1, d))


def _shift_rows(xv, k):
    n = xv.shape[0]
    rid = lax.broadcasted_iota(jnp.int32, xv.shape, 0)
    rolled = pltpu.roll(xv, k % n, 0)
    valid = (rid >= k) if k > 0 else (rid < n + k)
    return jnp.where(valid, rolled, 0.0)


def _tile_scan(a, u, reverse):
    n = a.shape[0]
    rid = lax.broadcasted_iota(jnp.int32, a.shape, 0)
    s = 1
    while s < n:
        valid = (rid < n - s) if reverse else (rid >= s)
        sh = (n - s) if reverse else s
        a_sh = jnp.where(valid, pltpu.roll(a, sh, 0), 1.0)
        u_sh = jnp.where(valid, pltpu.roll(u, sh, 0), 0.0)
        u = u + a * u_sh
        a = a * a_sh
        s *= 2
    return a, u


def _rglru_kernel(hx_ref, hc_ref, wx_ref, wg_ref, cw_ref, cb_ref, wa_ref, wi_ref, ba_ref, bi_ref,
                  lam_ref, y_ref,
                  xp_s, xc_s, sg_s, af_s, uf_s, ab_s, ub_s, cf_s, cr_s,
                  *, tc, t):
    gw = RG_BLOCK
    gr = GRID_W
    n_tr = t // gr
    a_refs = (af_s, ab_s)
    u_refs = (uf_s, ub_s)
    cw = cw_ref[...]
    cb = cb_ref[...]

    def conv_taps(xm2, xm1, x0, xp1):
        return cw[0:1] * xm2 + cw[1:2] * xm1 + cw[2:3] * x0 + cw[3:4] * xp1 + cb

    xcx = _dot(hc_ref[0], wx_ref[...])
    xc_s[0:tc, :] = conv_taps(_shift_rows(xcx, 2), _shift_rows(xcx, 1), xcx, _shift_rows(xcx, -1))

    pad = 2 * gr

    def proj_body(i, carry):
        r0 = pl.multiple_of(i * ROW_BLOCK, ROW_BLOCK)
        hrows = hx_ref[0, pl.ds(r0, ROW_BLOCK), :]
        xp_s[pl.ds(pad + r0, ROW_BLOCK), :] = _dot(hrows, wx_ref[...])
        sg_s[pl.ds(r0, ROW_BLOCK), :] = _silu(_dot(hrows, wg_ref[...])).astype(BF16)
        return carry

    lax.fori_loop(0, t // ROW_BLOCK, proj_body, 0)

    xp_s[0:gr, :] = _shift_rows(xp_s[pad + (n_tr - 2) * gr:pad + (n_tr - 1) * gr, :], 1)
    xp_s[gr:pad, :] = _shift_rows(xp_s[pad + (n_tr - 1) * gr:pad + n_tr * gr, :], 1)
    xp_s[pad + n_tr * gr:pad + (n_tr + 1) * gr, :] = _shift_rows(xp_s[pad:pad + gr, :], -1)

    def conv_body(r, carry):
        r0 = pl.multiple_of(r * gr, gr)
        xc_s[pl.ds(tc + r0, gr), :] = conv_taps(
            xp_s[pl.ds(r0, gr), :], xp_s[pl.ds(r0 + gr, gr), :],
            xp_s[pl.ds(r0 + 2 * gr, gr), :], xp_s[pl.ds(r0 + 3 * gr, gr), :])
        return carry

    lax.fori_loop(0, n_tr, conv_body, 0)

    hsp = (0.5 * RG_C) * jax.nn.softplus(-lam_ref[...])
    hba = 0.5 * ba_ref[...]
    hbi = 0.5 * bi_ref[...]

    def gate_body(i, carry):
        r0 = pl.multiple_of(i * ROW_BLOCK, ROW_BLOCK)
        rows = pl.ds(r0, ROW_BLOCK)
        xc = xc_s[rows, :]
        xcb = xc.astype(BF16)
        hxc = 0.5 * xc
        for d in range(2):
            tr = jnp.tanh(_dot(xcb, wa_ref[d, 0]) + hba[d:d + 1, :])
            ti = jnp.tanh(_dot(xcb, wi_ref[d, 0]) + hbi[d:d + 1, :])
            nla = hsp[d:d + 1, :] + hsp[d:d + 1, :] * tr
            a = jnp.exp(-nla)
            a_refs[d][rows, :] = a
            z = jnp.tanh(nla) * (a * a + 1.0)
            u_refs[d][rows, :] = (z * lax.rsqrt(jnp.maximum(z, F32_TINY))) * (hxc + hxc * ti)
        return carry

    lax.fori_loop(0, (tc + t) // ROW_BLOCK, gate_body, 0)

    n_ct = tc // 8

    def ctx_body(i, carry):
        hf, hb = carry
        rf = pl.ds(pl.multiple_of(i * 8, 8), 8)
        a, u = _tile_scan(af_s[rf, :], uf_s[rf, :], False)
        hf = a[7:8] * hf + u[7:8]
        rb = pl.ds(pl.multiple_of((n_ct - 1 - i) * 8, 8), 8)
        a, u = _tile_scan(ab_s[rb, :], ub_s[rb, :], True)
        hb = a[0:1] * hb + u[0:1]
        return hf, hb

    zero = jnp.zeros((1, gw), F32)
    hf, hb = lax.fori_loop(0, n_ct, ctx_body, (zero, zero))

    def local_scan(d):
        def body(i, carry):
            hh, aa = carry
            r = i if d == 0 else n_tr - 1 - i
            rows = pl.ds(tc + pl.multiple_of(r * gr, gr), gr)
            a = a_refs[d][rows, :]
            hh = a * hh + u_refs[d][rows, :]
            aa = a * aa
            u_refs[d][rows, :] = hh
            a_refs[d][rows, :] = aa
            return hh, aa

        lax.fori_loop(0, n_tr, body, (jnp.zeros((gr, gw), F32), jnp.ones((gr, gw), F32)))

    local_scan(0)
    local_scan(1)

    def carry_body(i, carry):
        cf, cr = carry
        cf_s[pl.ds(i, 1), :] = cf
        last = tc + (n_tr - 1) * gr
        cf = uf_s[pl.ds(last + i, 1), :] + af_s[pl.ds(last + i, 1), :] * cf
        w = gr - 1 - i
        cr_s[pl.ds(w, 1), :] = cr
        cr = ub_s[pl.ds(tc + w, 1), :] + ab_s[pl.ds(tc + w, 1), :] * cr
        return cf, cr

    lax.fori_loop(0, gr, carry_body, (hf, hb))

    def out_body(r, carry):
        r0 = pl.multiple_of(r * gr, gr)
        rows = pl.ds(tc + r0, gr)
        h = (uf_s[rows, :] + af_s[rows, :] * cf_s[...]) + (ub_s[rows, :] + ab_s[rows, :] * cr_s[...])
        y_ref[0, pl.ds(r0, gr), :] = (h * sg_s[pl.ds(r0, gr), :].astype(F32)).astype(BF16)
        return carry

    lax.fori_loop(0, n_tr, out_body, 0)


def _rglru(hx, hc, w_in, conv_w, conv_b, w_a, b_a, w_x, b_x, lam):
    bsz, t, d = hx.shape
    tc = hc.shape[1]
    e = w_in.shape[1] // 2
    gw = RG_BLOCK
    nb = e // gw
    n_rows = tc + t
    seq_f32 = pltpu.VMEM((n_rows, gw), F32)
    vec = lambda rows: pl.BlockSpec((rows, gw), lambda b, j: (0, j))
    gate_w = pl.BlockSpec((2, 1, gw, gw), lambda b, j: (0, j, 0, 0))
    return pl.pallas_call(
        functools.partial(_rglru_kernel, tc=tc, t=t),
        grid=(bsz, nb),
        in_specs=[
            pl.BlockSpec((1, t, d), lambda b, j: (b, 0, 0)),
            pl.BlockSpec((1, tc, d), lambda b, j: (b, 0, 0)),
            pl.BlockSpec((d, gw), lambda b, j: (0, j)),
            pl.BlockSpec((d, gw), lambda b, j: (0, nb + j)),
            vec(conv_w.shape[0]), vec(1),
            gate_w, gate_w, vec(2), vec(2), vec(2),
        ],
        out_specs=pl.BlockSpec((1, t, gw), lambda b, j: (b, 0, j)),
        out_shape=jax.ShapeDtypeStruct((bsz, t, e), BF16),
        scratch_shapes=[
            pltpu.VMEM((t + 3 * GRID_W, gw), F32),
            seq_f32,
            pltpu.VMEM((t, gw), BF16),
            seq_f32, seq_f32, seq_f32, seq_f32,
            pltpu.VMEM((GRID_W, gw), F32), pltpu.VMEM((GRID_W, gw), F32),
        ],
        compiler_params=pltpu.CompilerParams(
            dimension_semantics=("arbitrary", "arbitrary"), vmem_limit_bytes=VMEM_LIMIT),
        name="rglru",
    )(hx, hc, w_in, w_in, conv_w, conv_b.reshape(1, e), w_a, w_x, b_a, b_x, lam)


def kernel(x, c, ctx, c_ctx, ada_w, ada_b, norm_g, final_norm_g, hg_w_in, hg_lower_bounds,
           hg_norm_g, hg_w_out, rg_w_in, rg_conv_w, rg_conv_b, rg_w_a, rg_b_a, rg_w_x, rg_b_x,
           rg_lambda, rg_w_out):
    bsz, t, d = x.shape
    assert ada_w.shape[0] == 2 and bsz < MOD_ROWS and t % GRID_W == 0
    cond = jnp.concatenate(
        [c, c_ctx[None, :], jnp.zeros((MOD_ROWS - bsz - 1, d), F32)], axis=0)
    mod = _ada(cond, ada_w, ada_b)

    hx, hc = _prenorm(x, ctx, mod[0], norm_g[0])
    ogx, ogc = _hgrn(hx, hc, hg_w_in[0].astype(BF16), hg_lower_bounds, hg_norm_g[0], layer=0)
    x1, hx1, hc1 = _outproj_mid(ogx, ogc, x, ctx, hg_w_out[0].astype(BF16), mod[0], mod[1], norm_g[1])
    yg = _rglru(hx1, hc1, rg_w_in[0].astype(BF16), rg_conv_w[0], rg_conv_b[0],
                (0.5 * rg_w_a[0]).astype(BF16), rg_b_a[0], (0.5 * rg_w_x[0]).astype(BF16), rg_b_x[0],
                rg_lambda[0])
    return _outproj_final(yg, x1, rg_w_out[0].astype(BF16), mod[1], final_norm_g)
```

```python
import functools

import jax
import jax.numpy as jnp
from jax import lax
from jax.experimental import pallas as pl
from jax.experimental.pallas import tpu as pltpu

F32 = jnp.float32
BF16 = jnp.bfloat16

EPS = 1e-6
F32_TINY = 1.1754944e-38
GRID_W = 64
HG_KEY_DIM = 128
HG_CHUNK = 64
RG_BLOCK = 256
RG_C = 8.0
MOD_ROWS = 16
ROW_BLOCK = 256
GATE_ROWS = 64
HG_BLOCK = 512
VMEM_LIMIT = 56 * 1024 * 1024

_NT = (((1,), (1,)), ((), ()))
_TN = (((0,), (0,)), ((), ()))


def _half_tanh(z):
    return jnp.tanh(0.5 * z)


def _silu(z):
    hz = 0.5 * z
    return hz + hz * jnp.tanh(hz)


def _dot(a, b):
    return jnp.dot(a, b, preferred_element_type=F32)


def _ada_kernel(c_ref, w_ref, b_ref, o_ref):
    s = _silu(c_ref[...])
    o_ref[0] = jnp.dot(s, w_ref[0], preferred_element_type=F32,
                       precision=lax.Precision.HIGHEST) + b_ref[0]


def _ada(cond, ada_w, ada_b):
    depth, d, n3 = ada_w.shape
    bn = d
    return pl.pallas_call(
        _ada_kernel,
        grid=(depth, n3 // bn),
        in_specs=[
            pl.BlockSpec((MOD_ROWS, d), lambda i, n: (0, 0)),
            pl.BlockSpec((1, d, bn), lambda i, n: (i, 0, n)),
            pl.BlockSpec((1, 1, bn), lambda i, n: (i, 0, n)),
        ],
        out_specs=pl.BlockSpec((1, MOD_ROWS, bn), lambda i, n: (i, 0, n)),
        out_shape=jax.ShapeDtypeStruct((depth, MOD_ROWS, n3), F32),
        name="ada",
    )(cond, ada_w, ada_b.reshape(depth, 1, n3))


def _norm_mod(xv, g, shift, scale):
    ms = jnp.mean(xv * xv, axis=-1, keepdims=True)
    return xv * lax.rsqrt(ms + EPS) * g * (1.0 + scale) + shift


def _prenorm_kernel(x_ref, ctx_ref, mod_ref, g_ref, hx_ref, hc_ref, *, d, ctx_row):
    b = pl.program_id(0)
    g = g_ref[...]
    row = mod_ref[pl.ds(b, 1), :]
    hx_ref[0] = _norm_mod(x_ref[0], g, row[:, :d], row[:, d:2 * d]).astype(BF16)

    @pl.when(pl.program_id(1) == 0)
    def _():
        rowc = mod_ref[ctx_row:ctx_row + 1, :]
        hc_ref[0] = _norm_mod(ctx_ref[0], g, rowc[:, :d], rowc[:, d:2 * d]).astype(BF16)


def _prenorm(x, ctx, mod, g):
    bsz, t, d = x.shape
    tc = ctx.shape[1]
    rb = 512
    return pl.pallas_call(
        functools.partial(_prenorm_kernel, d=d, ctx_row=bsz),
        grid=(bsz, t // rb),
        in_specs=[
            pl.BlockSpec((1, rb, d), lambda b, r: (b, r, 0)),
            pl.BlockSpec((1, tc, d), lambda b, r: (b, 0, 0)),
            pl.BlockSpec((MOD_ROWS, 3 * d), lambda b, r: (0, 0)),
            pl.BlockSpec((1, d), lambda b, r: (0, 0)),
        ],
        out_specs=[
            pl.BlockSpec((1, rb, d), lambda b, r: (b, r, 0)),
            pl.BlockSpec((1, tc, d), lambda b, r: (b, 0, 0)),
        ],
        out_shape=[jax.ShapeDtypeStruct((bsz, t, d), BF16),
                   jax.ShapeDtypeStruct((bsz, tc, d), BF16)],
        name="prenorm",
    )(x, ctx, mod, g.reshape(1, d))


def _hgrn_kernel(hx_ref, hc_ref, wq_ref, wv_ref, wf_ref, wb_ref, wg_ref, lbp_ref, ng_ref, tri_ref,
                 ox_ref, oc_ref,
                 qa_b, va_b, lffa_b, lfba_b, kfa_b, kba_b, qb_b, vb_b, lffb_b, lfbb_b, kfb_b, kbb_b,
                 qtf_s, qtb_s, sg_s, o_s, oif_s, oib_s, ut_s, em_s,
                 *, layer, tc, t, heads):
    n_rows = tc + t
    c = HG_CHUNK
    kd = HG_KEY_DIM
    qt_s = (qtf_s, qtb_s)
    oi_s = (oif_s, oib_s)

    lbp = lbp_ref[...]
    ex = jnp.exp(lbp - jnp.max(lbp, axis=0, keepdims=True))
    lb = jnp.sum(ex[:layer + 1], axis=0, keepdims=True) / jnp.sum(ex, axis=0, keepdims=True)
    f_mid = 0.5 * (1.0 + lb)
    f_half = 0.5 * (1.0 - lb)

    rid = lax.broadcasted_iota(jnp.int32, (c, c), 0)
    cid = lax.broadcasted_iota(jnp.int32, (c, c), 1)
    masks = (rid >= cid, rid <= cid)

    def split3(xv):
        hi = xv.astype(BF16)
        r1 = xv - hi.astype(F32)
        mid = r1.astype(BF16)
        lo = (r1 - mid.astype(F32)).astype(BF16)
        return jnp.concatenate([hi, mid, lo], axis=0)

    def proj_steps(hrows, buf, r0):
        nr = hrows.shape[0]
        q_b, v_b, lf_b, k_b = buf

        def step_q():
            q_b[0:nr, :] = _silu(_dot(hrows, wq_ref[...]))

        def step_v():
            v_b[0:nr, :] = _dot(hrows, wv_ref[...]).astype(BF16)

        def step_f(d):
            f = f_mid + f_half * _half_tanh(_dot(hrows, (wf_ref, wb_ref)[d][...]))
            lf_b[d][0:nr, :] = jnp.log(f)
            k_b[d][0:nr, :] = 1.0 - f

        def step_g():
            sg_s[pl.ds(r0, nr), :] = _silu(_dot(hrows, wg_ref[...])).astype(BF16)

        return [step_q, step_v, functools.partial(step_f, 0), functools.partial(step_f, 1), step_g]

    def chunk_steps(buf, nr, r0, c0):
        q_b, v_b, lf_b, k_b = buf
        crs = [slice(cc * c, (cc + 1) * c) for cc in range(nr // c)]
        vals = {}

        def step_cumsum():
            vals["b"] = [[_dot(tri_ref[d], split3(lf_b[d][cr, :])) for d in range(2)] for cr in crs]

        def step_scale():
            qts, kts = [], []
            for cc, cr in enumerate(crs):
                qv = q_b[cr, :]
                qt, kt = [], []
                for d in range(2):
                    bsum = vals["b"][cc][d]
                    m = 0.5 * (bsum[c - 1:c, :] if d == 0 else bsum[0:1, :])
                    em_s[d, pl.ds(c0 + cc, 1), :] = jnp.exp(m)
                    qt.append((qv * jnp.exp(bsum - m)).astype(BF16))
                    kt.append((k_b[d][cr, :] * jnp.exp(m - bsum)).astype(BF16))
                    qt_s[d][pl.ds(r0 + cc * c, c), :] = qt[d]
                qts.append(qt)
                kts.append(kt)
            vals["qt"], vals["kt"] = qts, kts

        def step_scores():
            scs = []
            for cc in range(len(crs)):
                for h in range(heads):
                    cs = slice(h * kd, (h + 1) * kd)
                    qt, kt = vals["qt"][cc], vals["kt"][cc]
                    sf = lax.dot_general(qt[0][:, cs], kt[0][:, cs], _NT, preferred_element_type=F32)
                    sb = lax.dot_general(qt[1][:, cs], kt[1][:, cs], _NT, preferred_element_type=F32)
                    scs.append((jnp.where(masks[0], sf, 0.0) + jnp.where(masks[1], sb, 0.0)).astype(BF16))
            vals["sc"] = scs

        def step_out():
            for cc, cr in enumerate(crs):
                for h in range(heads):
                    cs = slice(h * kd, (h + 1) * kd)
                    vv = v_b[cr, cs]
                    kt = vals["kt"][cc]
                    o_s[pl.ds(r0 + cc * c, c), cs] = _dot(vals["sc"][cc * heads + h], vv)
                    kk = jnp.concatenate([kt[0][:, cs], kt[1][:, cs]], axis=1)
                    ut_s[c0 + cc, h] = lax.dot_general(vv, kk, _TN, preferred_element_type=F32)

        return [step_cumsum, step_scale, step_scores, step_out]

    def interleave(proj, chunks):
        for k, step in enumerate(proj):
            step()
            if k < len(chunks):
                chunks[k]()

    bufs = ((qa_b, va_b, (lffa_b, lfba_b), (kfa_b, kba_b)),
            (qb_b, vb_b, (lffb_b, lfbb_b), (kfb_b, kbb_b)))
    blocks = [(hc_ref[0], 0, tc)]
    blocks += [(hx_ref[0, i * HG_BLOCK:(i + 1) * HG_BLOCK, :], tc + i * HG_BLOCK, HG_BLOCK)
               for i in range(t // HG_BLOCK)]
    prev = []
    for n, (hrows, r0, nr) in enumerate(blocks):
        interleave(proj_steps(hrows, bufs[n % 2], r0), prev)
        prev = chunk_steps(bufs[n % 2], nr, r0, r0 // c)
    for step in prev:
        step()

    n_ctx = tc // c
    n_all = n_rows // c
    units = [(d, h) for d in range(2) for h in range(heads)]
    st = {u: jnp.zeros((kd, kd), F32) for u in units}
    for s in range(n_all):
        ci = (s, n_ctx - 1 - s if s < n_ctx else n_all + n_ctx - 1 - s)
        for d, h in units:
            cs = slice(h * kd, (h + 1) * kd)
            rows = slice(ci[d] * c, (ci[d] + 1) * c)
            em = em_s[d, ci[d]:ci[d] + 1, cs]
            sm = st[(d, h)] * em
            st[(d, h)] = em * (sm + ut_s[ci[d], h][:, d * kd:(d + 1) * kd])
            oi_s[d][rows, cs] = lax.dot_general(
                qt_s[d][rows, cs], sm.astype(BF16), _NT, preferred_element_type=F32)

    ng = ng_ref[...]

    def readout(r0):
        rows = pl.ds(r0, ROW_BLOCK)
        o = o_s[rows, :] + oif_s[rows, :] + oib_s[rows, :]
        parts = []
        for h in range(heads):
            cs = slice(h * kd, (h + 1) * kd)
            oh = o[:, cs]
            ms = jnp.mean(oh * oh, axis=-1, keepdims=True)
            parts.append(oh * lax.rsqrt(ms + EPS) * ng[:, cs])
        return (jnp.concatenate(parts, axis=-1) * sg_s[rows, :].astype(F32)).astype(BF16)

    for i in range(tc // ROW_BLOCK):
        oc_ref[0, i * ROW_BLOCK:(i + 1) * ROW_BLOCK, :] = readout(i * ROW_BLOCK)

    def out_body(i, carry):
        r0 = pl.multiple_of(i * ROW_BLOCK, ROW_BLOCK)
        ox_ref[0, pl.ds(r0, ROW_BLOCK), :] = readout(tc + r0)
        return carry

    lax.fori_loop(0, t // ROW_BLOCK, out_body, 0)


def _hgrn(hx, hc, w_in, lower_bounds, norm_g, layer):
    bsz, t, d = hx.shape
    tc = hc.shape[1]
    e = w_in.shape[1] // 5
    heads = 2
    gw = heads * HG_KEY_DIM
    ng_blocks = e // gw
    n_rows = tc + t
    c = HG_CHUNK

    low = jnp.tril(jnp.ones((c, c), F32))
    tri = jnp.stack([jnp.tile(low, (1, 3)), jnp.tile(low.T, (1, 3))]).astype(BF16)

    def wspec(sec):
        return pl.BlockSpec((d, gw), lambda b, j, sec=sec: (0, sec * ng_blocks + j))

    seq_f32 = pltpu.VMEM((n_rows, gw), F32)
    seq_bf16 = pltpu.VMEM((n_rows, gw), BF16)
    blk_f32 = pltpu.VMEM((HG_BLOCK, gw), F32)
    blk_bf16 = pltpu.VMEM((HG_BLOCK, gw), BF16)
    n_chunks = n_rows // c
    return pl.pallas_call(
        functools.partial(_hgrn_kernel, layer=layer, tc=tc, t=t, heads=heads),
        grid=(bsz, ng_blocks),
        in_specs=[
            pl.BlockSpec((1, t, d), lambda b, j: (b, 0, 0)),
            pl.BlockSpec((1, tc, d), lambda b, j: (b, 0, 0)),
            wspec(0), wspec(1), wspec(2), wspec(3), wspec(4),
            pl.BlockSpec((lower_bounds.shape[0], gw), lambda b, j: (0, j)),
            pl.BlockSpec((1, gw), lambda b, j: (0, j)),
            pl.BlockSpec((2, c, 3 * c), lambda b, j: (0, 0, 0)),
        ],
        out_specs=[
            pl.BlockSpec((1, t, gw), lambda b, j: (b, 0, j)),
            pl.BlockSpec((1, tc, gw), lambda b, j: (b, 0, j)),
        ],
        out_shape=[jax.ShapeDtypeStruct((bsz, t, e), BF16),
                   jax.ShapeDtypeStruct((bsz, tc, e), BF16)],
        scratch_shapes=[
            blk_f32, blk_bf16, blk_f32, blk_f32, blk_f32, blk_f32,
            blk_f32, blk_bf16, blk_f32, blk_f32, blk_f32, blk_f32,
            seq_bf16, seq_bf16,
            seq_bf16,
            seq_f32, seq_f32, seq_f32,
            pltpu.VMEM((n_chunks, heads, HG_KEY_DIM, 2 * HG_KEY_DIM), F32),
            pltpu.VMEM((2, n_chunks, gw), F32),
        ],
        compiler_params=pltpu.CompilerParams(
            dimension_semantics=("arbitrary", "arbitrary"), vmem_limit_bytes=VMEM_LIMIT),
        name="hgrn2",
    )(hx, hc, w_in, w_in, w_in, w_in, w_in, lower_bounds, norm_g.reshape(1, e), tri)


def _outproj_mid_kernel(ogx_ref, ogc_ref, x_ref, ctx_ref, w_ref, mod0_ref, mod1_ref, g1_ref,
                        x1_ref, hx_ref, hc_ref, *, d, ctx_row):
    b = pl.program_id(0)
    g1 = g1_ref[...]
    row0 = mod0_ref[pl.ds(b, 1), :]
    row1 = mod1_ref[pl.ds(b, 1), :]
    x1 = x_ref[0] + row0[:, 2 * d:] * _dot(ogx_ref[0], w_ref[...])
    x1_ref[0] = x1
    hx_ref[0] = _norm_mod(x1, g1, row1[:, :d], row1[:, d:2 * d]).astype(BF16)

    @pl.when(pl.program_id(1) == 0)
    def _():
        c0 = mod0_ref[ctx_row:ctx_row + 1, :]
        c1 = mod1_ref[ctx_row:ctx_row + 1, :]
        ctx1 = ctx_ref[0] + c0[:, 2 * d:] * _dot(ogc_ref[0], w_ref[...])
        hc_ref[0] = _norm_mod(ctx1, g1, c1[:, :d], c1[:, d:2 * d]).astype(BF16)


def _outproj_mid(ogx, ogc, x, ctx, w_out, mod0, mod1, g1):
    bsz, t, d = x.shape
    tc = ctx.shape[1]
    e = w_out.shape[0]
    rb = 512
    return pl.pallas_call(
        functools.partial(_outproj_mid_kernel, d=d, ctx_row=bsz),
        grid=(bsz, t // rb),
        in_specs=[
            pl.BlockSpec((1, rb, e), lambda b, r: (b, r, 0)),
            pl.BlockSpec((1, tc, e), lambda b, r: (b, 0, 0)),
            pl.BlockSpec((1, rb, d), lambda b, r: (b, r, 0)),
            pl.BlockSpec((1, tc, d), lambda b, r: (b, 0, 0)),
            pl.BlockSpec((e, d), lambda b, r: (0, 0)),
            pl.BlockSpec((MOD_ROWS, 3 * d), lambda b, r: (0, 0)),
            pl.BlockSpec((MOD_ROWS, 3 * d), lambda b, r: (0, 0)),
            pl.BlockSpec((1, d), lambda b, r: (0, 0)),
        ],
        out_specs=[
            pl.BlockSpec((1, rb, d), lambda b, r: (b, r, 0)),
            pl.BlockSpec((1, rb, d), lambda b, r: (b, r, 0)),
            pl.BlockSpec((1, tc, d), lambda b, r: (b, 0, 0)),
        ],
        out_shape=[jax.ShapeDtypeStruct((bsz, t, d), F32),
                   jax.ShapeDtypeStruct((bsz, t, d), BF16),
                   jax.ShapeDtypeStruct((bsz, tc, d), BF16)],
        compiler_params=pltpu.CompilerParams(
            dimension_semantics=("arbitrary", "arbitrary"), vmem_limit_bytes=VMEM_LIMIT),
        name="outproj0",
    )(ogx, ogc, x, ctx, w_out, mod0, mod1, g1.reshape(1, d))


def _outproj_final_kernel(yg_ref, x_ref, w_ref, mod_ref, g_ref, o_ref, *, d):
    b = pl.program_id(0)
    row = mod_ref[pl.ds(b, 1), :]
    x2 = x_ref[0] + row[:, 2 * d:] * _dot(yg_ref[0], w_ref[...])
    ms = jnp.mean(x2 * x2, axis=-1, keepdims=True)
    o_ref[0] = x2 * lax.rsqrt(ms + EPS) * g_ref[...]


def _outproj_final(yg, x1, w_out, mod, g):
    bsz, t, d = x1.shape
    e = w_out.shape[0]
    rb = 512
    return pl.pallas_call(
        functools.partial(_outproj_final_kernel, d=d),
        grid=(bsz, t // rb),
        in_specs=[
            pl.BlockSpec((1, rb, e), lambda b, r: (b, r, 0)),
            pl.BlockSpec((1, rb, d), lambda b, r: (b, r, 0)),
            pl.BlockSpec((e, d), lambda b, r: (0, 0)),
            pl.BlockSpec((MOD_ROWS, 3 * d), lambda b, r: (0, 0)),
            pl.BlockSpec((1, d), lambda b, r: (0, 0)),
        ],
        out_specs=pl.BlockSpec((1, rb, d), lambda b, r: (b, r, 0)),
        out_shape=jax.ShapeDtypeStruct((bsz, t, d), F32),
        compiler_params=pltpu.CompilerParams(
            dimension_semantics=("arbitrary", "arbitrary"), vmem_limit_bytes=VMEM_LIMIT),
        name="outproj1",
    )(yg, x1, w_out, mod, g.reshape(1, d))


def _shift_rows(xv, k):
    n = xv.shape[0]
    rid = lax.broadcasted_iota(jnp.int32, xv.shape, 0)
    rolled = pltpu.roll(xv, k % n, 0)
    valid = (rid >= k) if k > 0 else (rid < n + k)
    return jnp.where(valid, rolled, 0.0)


def _tile_scan(a, u, reverse):
    n = a.shape[0]
    rid = lax.broadcasted_iota(jnp.int32, a.shape, 0)
    s = 1
    while s < n:
        valid = (rid < n - s) if reverse else (rid >= s)
        sh = (n - s) if reverse else s
        a_sh = jnp.where(valid, pltpu.roll(a, sh, 0), 1.0)
        u_sh = jnp.where(valid, pltpu.roll(u, sh, 0), 0.0)
        u = u + a * u_sh
        a = a * a_sh
        s *= 2
    return a, u


def _rglru_kernel(hx_ref, hc_ref, wx_ref, wg_ref, cw_ref, cb_ref, wa_ref, wi_ref, ba_ref, bi_ref,
                  lam_ref, y_ref,
                  xp_s, xc_s, sg_s, af_s, uf_s, ab_s, ub_s,
                  *, tc, t):
    gw = RG_BLOCK
    gr = GRID_W
    n_tr = t // gr
    a_refs = (af_s, ab_s)
    u_refs = (uf_s, ub_s)
    cw = cw_ref[...]
    cb = cb_ref[...]

    def conv_taps(xm2, xm1, x0, xp1):
        return cw[0:1] * xm2 + cw[1:2] * xm1 + cw[2:3] * x0 + cw[3:4] * xp1 + cb

    xcx = _dot(hc_ref[0], wx_ref[...])
    xc_s[0:tc, :] = conv_taps(_shift_rows(xcx, 2), _shift_rows(xcx, 1), xcx, _shift_rows(xcx, -1))

    pad = 2 * gr
    nb = t // ROW_BLOCK
    hsp = (0.5 * RG_C) * jax.nn.softplus(-lam_ref[...])
    hba = 0.5 * ba_ref[...]
    hbi = 0.5 * bi_ref[...]

    def lat_r0(i):
        return i * ROW_BLOCK if isinstance(i, int) else pl.multiple_of(i * ROW_BLOCK, ROW_BLOCK)

    def proj_x(i):
        r0 = lat_r0(i)
        xp_s[pl.ds(pad + r0, ROW_BLOCK), :] = _dot(hx_ref[0, pl.ds(r0, ROW_BLOCK), :], wx_ref[...])

    def proj_g(i):
        r0 = lat_r0(i)
        sg_s[pl.ds(r0, ROW_BLOCK), :] = _silu(
            _dot(hx_ref[0, pl.ds(r0, ROW_BLOCK), :], wg_ref[...])).astype(BF16)

    def conv_block(i):
        for k in range(ROW_BLOCK // gr):
            r0 = lat_r0(i) + k * gr
            xc_s[pl.ds(tc + r0, gr), :] = conv_taps(
                xp_s[pl.ds(r0, gr), :], xp_s[pl.ds(r0 + gr, gr), :],
                xp_s[pl.ds(r0 + 2 * gr, gr), :], xp_s[pl.ds(r0 + 3 * gr, gr), :])

    tot = {"H": jnp.zeros((gr, gw), F32), "A": jnp.ones((gr, gw), F32),
           "E": jnp.zeros((gr, gw), F32), "G": jnp.ones((gr, gw), F32)}

    def gates(r0, d, latent):
        for part in range(ROW_BLOCK // GATE_ROWS):
            gates_part(r0 + part * GATE_ROWS, d, latent)

    def gates_part(r0, d, latent):
        rows = pl.ds(r0, GATE_ROWS)
        xc = xc_s[rows, :]
        xcb = xc.astype(BF16)
        hxc = 0.5 * xc
        tr = jnp.tanh(_dot(xcb, wa_ref[d, 0]) + hba[d:d + 1, :])
        ti = jnp.tanh(_dot(xcb, wi_ref[d, 0]) + hbi[d:d + 1, :])
        nla = hsp[d:d + 1, :] + hsp[d:d + 1, :] * tr
        a = jnp.exp(-nla)
        a_refs[d][rows, :] = a
        z = jnp.tanh(nla) * (a * a + 1.0)
        u = (z * lax.rsqrt(jnp.maximum(z, F32_TINY))) * (hxc + hxc * ti)
        u_refs[d][rows, :] = u
        if not latent:
            return
        for k in range(GATE_ROWS // gr):
            ak, uk = a[k * gr:(k + 1) * gr, :], u[k * gr:(k + 1) * gr, :]
            if d == 0:
                tot["H"] = ak * tot["H"] + uk
                tot["A"] = ak * tot["A"]
            else:
                tot["E"] = tot["E"] + tot["G"] * uk
                tot["G"] = tot["G"] * ak

    proj_x(nb - 1)
    for i in range(tc // ROW_BLOCK):
        gates(i * ROW_BLOCK, 0, False)
    proj_g(nb - 1)
    for i in range(tc // ROW_BLOCK):
        gates(i * ROW_BLOCK, 1, False)
    xp_s[0:gr, :] = _shift_rows(xp_s[pad + (n_tr - 2) * gr:pad + (n_tr - 1) * gr, :], 1)
    xp_s[gr:pad, :] = _shift_rows(xp_s[pad + (n_tr - 1) * gr:pad + n_tr * gr, :], 1)
    proj_x(0)
    proj_g(0)
    xp_s[pad + n_tr * gr:pad + (n_tr + 1) * gr, :] = _shift_rows(xp_s[pad:pad + gr, :], -1)

    def pipe_body(k):
        proj_x(k + 1)
        conv_block(k)
        gates(tc + lat_r0(k), 0, True)
        proj_g(k + 1)
        gates(tc + lat_r0(k), 1, True)

    for k in range(nb - 2):
        pipe_body(k)
    for i in (nb - 2, nb - 1):
        conv_block(i)
        gates(tc + lat_r0(i), 0, True)
        gates(tc + lat_r0(i), 1, True)

    n_ct = tc // 8

    def ctx_body(i, carry):
        hf, hb = carry
        rf = pl.ds(pl.multiple_of(i * 8, 8), 8)
        a, u = _tile_scan(af_s[rf, :], uf_s[rf, :], False)
        hf = a[7:8] * hf + u[7:8]
        rb = pl.ds(pl.multiple_of((n_ct - 1 - i) * 8, 8), 8)
        a, u = _tile_scan(ab_s[rb, :], ub_s[rb, :], True)
        hb = a[0:1] * hb + u[0:1]
        return hf, hb

    zero = jnp.zeros((1, gw), F32)
    hf, hb = lax.fori_loop(0, n_ct, ctx_body, (zero, zero))

    def column_carries(e_tot, p_tot, h0, reverse):
        rid = lax.broadcasted_iota(jnp.int32, (8, gw), 0)
        order = range(gr // 8 - 1, -1, -1) if reverse else range(gr // 8)
        carry, out = h0, [None] * (gr // 8)
        for i in order:
            a_t, u_t = _tile_scan(p_tot[i * 8:(i + 1) * 8, :], e_tot[i * 8:(i + 1) * 8, :], reverse)
            h = u_t + a_t * carry
            if reverse:
                out[i] = jnp.where(rid == 7, carry, pltpu.roll(h, 7, 0))
                carry = h[0:1]
            else:
                out[i] = jnp.where(rid == 0, carry, pltpu.roll(h, 1, 0))
                carry = h[7:8]
        return jnp.concatenate(out, axis=0)

    h = column_carries(tot["H"], tot["A"], hf, False)
    for r in range(n_tr):
        rows = slice(tc + r * gr, tc + (r + 1) * gr)
        h = af_s[rows, :] * h + uf_s[rows, :]
        uf_s[rows, :] = h
    h = column_carries(tot["E"], tot["G"], hb, True)
    for r in range(n_tr - 1, -1, -1):
        rows = slice(tc + r * gr, tc + (r + 1) * gr)
        h = ab_s[rows, :] * h + ub_s[rows, :]
        gate = sg_s[r * gr:(r + 1) * gr, :].astype(F32)
        y_ref[0, r * gr:(r + 1) * gr, :] = ((uf_s[rows, :] + h) * gate).astype(BF16)


def _rglru(hx, hc, w_in, conv_w, conv_b, w_a, b_a, w_x, b_x, lam):
    bsz, t, d = hx.shape
    tc = hc.shape[1]
    e = w_in.shape[1] // 2
    gw = RG_BLOCK
    nb = e // gw
    n_rows = tc + t
    seq_f32 = pltpu.VMEM((n_rows, gw), F32)
    vec = lambda rows: pl.BlockSpec((rows, gw), lambda b, j: (0, j))
    gate_w = pl.BlockSpec((2, 1, gw, gw), lambda b, j: (0, j, 0, 0))
    return pl.pallas_call(
        functools.partial(_rglru_kernel, tc=tc, t=t),
        grid=(bsz, nb),
        in_specs=[
            pl.BlockSpec((1, t, d), lambda b, j: (b, 0, 0)),
            pl.BlockSpec((1, tc, d), lambda b, j: (b, 0, 0)),
            pl.BlockSpec((d, gw), lambda b, j: (0, j)),
            pl.BlockSpec((d, gw), lambda b, j: (0, nb + j)),
            vec(conv_w.shape[0]), vec(1),
            gate_w, gate_w, vec(2), vec(2), vec(2),
        ],
        out_specs=pl.BlockSpec((1, t, gw), lambda b, j: (b, 0, j)),
        out_shape=jax.ShapeDtypeStruct((bsz, t, e), BF16),
        scratch_shapes=[
            pltpu.VMEM((t + 3 * GRID_W, gw), F32),
            seq_f32,
            pltpu.VMEM((t, gw), BF16),
            seq_f32, seq_f32, seq_f32, seq_f32,
        ],
        compiler_params=pltpu.CompilerParams(
            dimension_semantics=("arbitrary", "arbitrary"), vmem_limit_bytes=VMEM_LIMIT),
        name="rglru",
    )(hx, hc, w_in, w_in, conv_w, conv_b.reshape(1, e), w_a, w_x, b_a, b_x, lam)


def kernel(x, c, ctx, c_ctx, ada_w, ada_b, norm_g, final_norm_g, hg_w_in, hg_lower_bounds,
           hg_norm_g, hg_w_out, rg_w_in, rg_conv_w, rg_conv_b, rg_w_a, rg_b_a, rg_w_x, rg_b_x,
           rg_lambda, rg_w_out):
    bsz, t, d = x.shape
    assert ada_w.shape[0] == 2 and bsz < MOD_ROWS and t % GRID_W == 0
    cond = jnp.concatenate(
        [c, c_ctx[None, :], jnp.zeros((MOD_ROWS - bsz - 1, d), F32)], axis=0)
    mod = _ada(cond, ada_w, ada_b)

    hx, hc = _prenorm(x, ctx, mod[0], norm_g[0])
    ogx, ogc = _hgrn(hx, hc, hg_w_in[0].astype(BF16), hg_lower_bounds, hg_norm_g[0], layer=0)
    x1, hx1, hc1 = _outproj_mid(ogx, ogc, x, ctx, hg_w_out[0].astype(BF16), mod[0], mod[1], norm_g[1])
    yg = _rglru(hx1, hc1, rg_w_in[0].astype(BF16), rg_conv_w[0], rg_conv_b[0],
                (0.5 * rg_w_a[0]).astype(BF16), rg_b_a[0], (0.5 * rg_w_x[0]).astype(BF16), rg_b_x[0],
                rg_lambda[0])
    return _outproj_final(yg, x1, rg_w_out[0].astype(BF16), mod[1], final_norm_g)
```

```python
import functools

import jax
import jax.numpy as jnp
from jax import lax
from jax.experimental import pallas as pl
from jax.experimental.pallas import tpu as pltpu

F32 = jnp.float32
BF16 = jnp.bfloat16

EPS = 1e-6
F32_TINY = 1.1754944e-38
GRID_W = 64
HG_KEY_DIM = 128
HG_CHUNK = 64
RG_BLOCK = 256
RG_C = 8.0
MOD_ROWS = 16
ROW_BLOCK = 256
GATE_ROWS = 64
HG_BLOCK = 512
VMEM_LIMIT = 56 * 1024 * 1024

_NT = (((1,), (1,)), ((), ()))
_TN = (((0,), (0,)), ((), ()))


def _half_tanh(z):
    return jnp.tanh(0.5 * z)


def _silu(z):
    hz = 0.5 * z
    return hz + hz * jnp.tanh(hz)


def _dot(a, b):
    return jnp.dot(a, b, preferred_element_type=F32)


def _ada_kernel(c_ref, w_ref, b_ref, o_ref):
    s = _silu(c_ref[...])
    o_ref[0] = jnp.dot(s, w_ref[0], preferred_element_type=F32,
                       precision=lax.Precision.HIGHEST) + b_ref[0]


def _ada(cond, ada_w, ada_b):
    depth, d, n3 = ada_w.shape
    bn = d
    return pl.pallas_call(
        _ada_kernel,
        grid=(depth, n3 // bn),
        in_specs=[
            pl.BlockSpec((MOD_ROWS, d), lambda i, n: (0, 0)),
            pl.BlockSpec((1, d, bn), lambda i, n: (i, 0, n)),
            pl.BlockSpec((1, 1, bn), lambda i, n: (i, 0, n)),
        ],
        out_specs=pl.BlockSpec((1, MOD_ROWS, bn), lambda i, n: (i, 0, n)),
        out_shape=jax.ShapeDtypeStruct((depth, MOD_ROWS, n3), F32),
        name="ada",
    )(cond, ada_w, ada_b.reshape(depth, 1, n3))


def _norm_mod(xv, g, shift, scale):
    ms = jnp.mean(xv * xv, axis=-1, keepdims=True)
    return xv * lax.rsqrt(ms + EPS) * g * (1.0 + scale) + shift


def _prenorm_kernel(x_ref, ctx_ref, mod_ref, g_ref, hx_ref, hc_ref, *, d, ctx_row):
    b = pl.program_id(0)
    g = g_ref[...]
    row = mod_ref[pl.ds(b, 1), :]
    hx_ref[0] = _norm_mod(x_ref[0], g, row[:, :d], row[:, d:2 * d]).astype(BF16)

    @pl.when(pl.program_id(1) == 0)
    def _():
        rowc = mod_ref[ctx_row:ctx_row + 1, :]
        hc_ref[0] = _norm_mod(ctx_ref[0], g, rowc[:, :d], rowc[:, d:2 * d]).astype(BF16)


def _prenorm(x, ctx, mod, g):
    bsz, t, d = x.shape
    tc = ctx.shape[1]
    rb = 512
    return pl.pallas_call(
        functools.partial(_prenorm_kernel, d=d, ctx_row=bsz),
        grid=(bsz, t // rb),
        in_specs=[
            pl.BlockSpec((1, rb, d), lambda b, r: (b, r, 0)),
            pl.BlockSpec((1, tc, d), lambda b, r: (b, 0, 0)),
            pl.BlockSpec((MOD_ROWS, 3 * d), lambda b, r: (0, 0)),
            pl.BlockSpec((1, d), lambda b, r: (0, 0)),
        ],
        out_specs=[
            pl.BlockSpec((1, rb, d), lambda b, r: (b, r, 0)),
            pl.BlockSpec((1, tc, d), lambda b, r: (b, 0, 0)),
        ],
        out_shape=[jax.ShapeDtypeStruct((bsz, t, d), BF16),
                   jax.ShapeDtypeStruct((bsz, tc, d), BF16)],
        name="prenorm",
    )(x, ctx, mod, g.reshape(1, d))


def _hgrn_kernel(hx_ref, hc_ref, wq_ref, wv_ref, wf_ref, wb_ref, wg_ref, lbp_ref, ng_ref, tri_ref,
                 ox_ref, oc_ref,
                 qa_b, va_b, lffa_b, lfba_b, kfa_b, kba_b, qb_b, vb_b, lffb_b, lfbb_b, kfb_b, kbb_b,
                 qtf_s, qtb_s, sg_s, o_s, oif_s, oib_s, ut_s, em_s,
                 *, layer, tc, t, heads):
    n_rows = tc + t
    c = HG_CHUNK
    kd = HG_KEY_DIM
    qt_s = (qtf_s, qtb_s)
    oi_s = (oif_s, oib_s)

    lbp = lbp_ref[...]
    ex = jnp.exp(lbp - jnp.max(lbp, axis=0, keepdims=True))
    lb = jnp.sum(ex[:layer + 1], axis=0, keepdims=True) / jnp.sum(ex, axis=0, keepdims=True)
    f_mid = 0.5 * (1.0 + lb)
    f_half = 0.5 * (1.0 - lb)

    rid = lax.broadcasted_iota(jnp.int32, (c, c), 0)
    cid = lax.broadcasted_iota(jnp.int32, (c, c), 1)
    masks = (rid >= cid, rid <= cid)

    def split3(xv):
        hi = xv.astype(BF16)
        r1 = xv - hi.astype(F32)
        mid = r1.astype(BF16)
        lo = (r1 - mid.astype(F32)).astype(BF16)
        return jnp.concatenate([hi, mid, lo], axis=0)

    def proj_steps(hrows, buf, r0):
        nr = hrows.shape[0]
        q_b, v_b, lf_b, k_b = buf

        def step_q():
            q_b[0:nr, :] = _silu(_dot(hrows, wq_ref[...]))

        def step_v():
            v_b[0:nr, :] = _dot(hrows, wv_ref[...]).astype(BF16)

        def step_f(d):
            f = f_mid + f_half * _half_tanh(_dot(hrows, (wf_ref, wb_ref)[d][...]))
            lf_b[d][0:nr, :] = jnp.log(f)
            k_b[d][0:nr, :] = 1.0 - f

        def step_g():
            sg_s[pl.ds(r0, nr), :] = _silu(_dot(hrows, wg_ref[...])).astype(BF16)

        return [step_q, step_v, functools.partial(step_f, 0), functools.partial(step_f, 1), step_g]

    def chunk_steps(buf, nr, r0, c0):
        q_b, v_b, lf_b, k_b = buf
        crs = [slice(cc * c, (cc + 1) * c) for cc in range(nr // c)]
        vals = {}

        def step_cumsum():
            vals["b"] = [[_dot(tri_ref[d], split3(lf_b[d][cr, :])) for d in range(2)] for cr in crs]

        def step_scale():
            qts, kts = [], []
            for cc, cr in enumerate(crs):
                qv = q_b[cr, :]
                qt, kt = [], []
                for d in range(2):
                    bsum = vals["b"][cc][d]
                    m = 0.5 * (bsum[c - 1:c, :] if d == 0 else bsum[0:1, :])
                    em_s[d, pl.ds(c0 + cc, 1), :] = jnp.exp(m)
                    qt.append((qv * jnp.exp(bsum - m)).astype(BF16))
                    kt.append((k_b[d][cr, :] * jnp.exp(m - bsum)).astype(BF16))
                    qt_s[d][pl.ds(r0 + cc * c, c), :] = qt[d]
                qts.append(qt)
                kts.append(kt)
            vals["qt"], vals["kt"] = qts, kts

        def step_scores():
            scs = []
            for cc in range(len(crs)):
                for h in range(heads):
                    cs = slice(h * kd, (h + 1) * kd)
                    qt, kt = vals["qt"][cc], vals["kt"][cc]
                    sf = lax.dot_general(qt[0][:, cs], kt[0][:, cs], _NT, preferred_element_type=F32)
                    sb = lax.dot_general(qt[1][:, cs], kt[1][:, cs], _NT, preferred_element_type=F32)
                    scs.append((jnp.where(masks[0], sf, 0.0) + jnp.where(masks[1], sb, 0.0)).astype(BF16))
            vals["sc"] = scs

        def step_out():
            for cc, cr in enumerate(crs):
                for h in range(heads):
                    cs = slice(h * kd, (h + 1) * kd)
                    vv = v_b[cr, cs]
                    kt = vals["kt"][cc]
                    o_s[pl.ds(r0 + cc * c, c), cs] = _dot(vals["sc"][cc * heads + h], vv)
                    kk = jnp.concatenate([kt[0][:, cs], kt[1][:, cs]], axis=1)
                    ut_s[c0 + cc, h] = lax.dot_general(vv, kk, _TN, preferred_element_type=F32)

        return [step_cumsum, step_scale, step_scores, step_out]

    def interleave(proj, chunks):
        for k, step in enumerate(proj):
            step()
            if k < len(chunks):
                chunks[k]()

    bufs = ((qa_b, va_b, (lffa_b, lfba_b), (kfa_b, kba_b)),
            (qb_b, vb_b, (lffb_b, lfbb_b), (kfb_b, kbb_b)))
    blocks = [(hc_ref[0], 0, tc)]
    blocks += [(hx_ref[0, i * HG_BLOCK:(i + 1) * HG_BLOCK, :], tc + i * HG_BLOCK, HG_BLOCK)
               for i in range(t // HG_BLOCK)]
    units = [(d, h) for d in range(2) for h in range(heads)]
    st = {u: jnp.zeros((kd, kd), F32) for u in units}

    def scan_step(d, ci):
        rows = slice(ci * c, (ci + 1) * c)
        for h in range(heads):
            cs = slice(h * kd, (h + 1) * kd)
            em = em_s[d, ci:ci + 1, cs]
            sm = st[(d, h)] * em
            st[(d, h)] = em * (sm + ut_s[ci, h][:, d * kd:(d + 1) * kd])
            oi_s[d][rows, cs] = lax.dot_general(
                qt_s[d][rows, cs], sm.astype(BF16), _NT, preferred_element_type=F32)

    prev, fwd_now, fwd_next = [], [], []
    for n, (hrows, r0, nr) in enumerate(blocks):
        proj = proj_steps(hrows, bufs[n % 2], r0)
        share = -(-len(fwd_now) // len(proj))
        for k, step in enumerate(proj):
            step()
            if k < len(prev):
                prev[k]()
            for f in fwd_now[k * share:(k + 1) * share]:
                f()
        fwd_now = fwd_next
        fwd_next = [functools.partial(scan_step, 0, ci) for ci in range(r0 // c, (r0 + nr) // c)]
        prev = chunk_steps(bufs[n % 2], nr, r0, r0 // c)
    for f in fwd_now:
        f()
    for step in prev:
        step()
    for f in fwd_next:
        f()

    ng = ng_ref[...]

    def readout(r0):
        rows = slice(r0, r0 + ROW_BLOCK)
        o = o_s[rows, :] + oif_s[rows, :] + oib_s[rows, :]
        parts = []
        for h in range(heads):
            cs = slice(h * kd, (h + 1) * kd)
            oh = o[:, cs]
            ms = jnp.mean(oh * oh, axis=-1, keepdims=True)
            parts.append(oh * lax.rsqrt(ms + EPS) * ng[:, cs])
        y = (jnp.concatenate(parts, axis=-1) * sg_s[rows, :].astype(F32)).astype(BF16)
        if r0 < tc:
            oc_ref[0, r0:r0 + ROW_BLOCK, :] = y
        else:
            ox_ref[0, r0 - tc:r0 - tc + ROW_BLOCK, :] = y

    n_ctx = tc // c
    n_all = n_rows // c
    cpb = ROW_BLOCK // c
    order = list(range(n_ctx - 1, -1, -1)) + list(range(n_all - 1, n_ctx - 1, -1))

    @pl.when(pl.program_id(0) >= 0)
    def _():
        for ci in order:
            scan_step(1, ci)
            if ci % cpb == 0:
                readout(ci * c)


def _hgrn(hx, hc, w_in, lower_bounds, norm_g, layer):
    bsz, t, d = hx.shape
    tc = hc.shape[1]
    e = w_in.shape[1] // 5
    heads = 2
    gw = heads * HG_KEY_DIM
    ng_blocks = e // gw
    n_rows = tc + t
    c = HG_CHUNK

    low = jnp.tril(jnp.ones((c, c), F32))
    tri = jnp.stack([jnp.tile(low, (1, 3)), jnp.tile(low.T, (1, 3))]).astype(BF16)

    def wspec(sec):
        return pl.BlockSpec((d, gw), lambda b, j, sec=sec: (0, sec * ng_blocks + j))

    seq_f32 = pltpu.VMEM((n_rows, gw), F32)
    seq_bf16 = pltpu.VMEM((n_rows, gw), BF16)
    blk_f32 = pltpu.VMEM((HG_BLOCK, gw), F32)
    blk_bf16 = pltpu.VMEM((HG_BLOCK, gw), BF16)
    n_chunks = n_rows // c
    return pl.pallas_call(
        functools.partial(_hgrn_kernel, layer=layer, tc=tc, t=t, heads=heads),
        grid=(bsz, ng_blocks),
        in_specs=[
            pl.BlockSpec((1, t, d), lambda b, j: (b, 0, 0)),
            pl.BlockSpec((1, tc, d), lambda b, j: (b, 0, 0)),
            wspec(0), wspec(1), wspec(2), wspec(3), wspec(4),
            pl.BlockSpec((lower_bounds.shape[0], gw), lambda b, j: (0, j)),
            pl.BlockSpec((1, gw), lambda b, j: (0, j)),
            pl.BlockSpec((2, c, 3 * c), lambda b, j: (0, 0, 0)),
        ],
        out_specs=[
            pl.BlockSpec((1, t, gw), lambda b, j: (b, 0, j)),
            pl.BlockSpec((1, tc, gw), lambda b, j: (b, 0, j)),
        ],
        out_shape=[jax.ShapeDtypeStruct((bsz, t, e), BF16),
                   jax.ShapeDtypeStruct((bsz, tc, e), BF16)],
        scratch_shapes=[
            blk_f32, blk_bf16, blk_f32, blk_f32, blk_f32, blk_f32,
            blk_f32, blk_bf16, blk_f32, blk_f32, blk_f32, blk_f32,
            seq_bf16, seq_bf16,
            seq_bf16,
            seq_f32, seq_f32, seq_f32,
            pltpu.VMEM((n_chunks, heads, HG_KEY_DIM, 2 * HG_KEY_DIM), F32),
            pltpu.VMEM((2, n_chunks, gw), F32),
        ],
        compiler_params=pltpu.CompilerParams(
            dimension_semantics=("arbitrary", "arbitrary"), vmem_limit_bytes=VMEM_LIMIT),
        name="hgrn2",
    )(hx, hc, w_in, w_in, w_in, w_in, w_in, lower_bounds, norm_g.reshape(1, e), tri)


def _outproj_mid_kernel(ogx_ref, ogc_ref, x_ref, ctx_ref, w_ref, mod0_ref, mod1_ref, g1_ref,
                        x1_ref, hx_ref, hc_ref, *, d, ctx_row):
    b = pl.program_id(0)
    g1 = g1_ref[...]
    row0 = mod0_ref[pl.ds(b, 1), :]
    row1 = mod1_ref[pl.ds(b, 1), :]
    x1 = x_ref[0] + row0[:, 2 * d:] * _dot(ogx_ref[0], w_ref[...])
    x1_ref[0] = x1
    hx_ref[0] = _norm_mod(x1, g1, row1[:, :d], row1[:, d:2 * d]).astype(BF16)

    @pl.when(pl.program_id(1) == 0)
    def _():
        c0 = mod0_ref[ctx_row:ctx_row + 1, :]
        c1 = mod1_ref[ctx_row:ctx_row + 1, :]
        ctx1 = ctx_ref[0] + c0[:, 2 * d:] * _dot(ogc_ref[0], w_ref[...])
        hc_ref[0] = _norm_mod(ctx1, g1, c1[:, :d], c1[:, d:2 * d]).astype(BF16)


def _outproj_mid(ogx, ogc, x, ctx, w_out, mod0, mod1, g1):
    bsz, t, d = x.shape
    tc = ctx.shape[1]
    e = w_out.shape[0]
    rb = 512
    return pl.pallas_call(
        functools.partial(_outproj_mid_kernel, d=d, ctx_row=bsz),
        grid=(bsz, t // rb),
        in_specs=[
            pl.BlockSpec((1, rb, e), lambda b, r: (b, r, 0)),
            pl.BlockSpec((1, tc, e), lambda b, r: (b, 0, 0)),
            pl.BlockSpec((1, rb, d), lambda b, r: (b, r, 0)),
            pl.BlockSpec((1, tc, d), lambda b, r: (b, 0, 0)),
            pl.BlockSpec((e, d), lambda b, r: (0, 0)),
            pl.BlockSpec((MOD_ROWS, 3 * d), lambda b, r: (0, 0)),
            pl.BlockSpec((MOD_ROWS, 3 * d), lambda b, r: (0, 0)),
            pl.BlockSpec((1, d), lambda b, r: (0, 0)),
        ],
        out_specs=[
            pl.BlockSpec((1, rb, d), lambda b, r: (b, r, 0)),
            pl.BlockSpec((1, rb, d), lambda b, r: (b, r, 0)),
            pl.BlockSpec((1, tc, d), lambda b, r: (b, 0, 0)),
        ],
        out_shape=[jax.ShapeDtypeStruct((bsz, t, d), F32),
                   jax.ShapeDtypeStruct((bsz, t, d), BF16),
                   jax.ShapeDtypeStruct((bsz, tc, d), BF16)],
        compiler_params=pltpu.CompilerParams(
            dimension_semantics=("arbitrary", "arbitrary"), vmem_limit_bytes=VMEM_LIMIT),
        name="outproj0",
    )(ogx, ogc, x, ctx, w_out, mod0, mod1, g1.reshape(1, d))


def _outproj_final_kernel(yg_ref, x_ref, w_ref, mod_ref, g_ref, o_ref, *, d):
    b = pl.program_id(0)
    row = mod_ref[pl.ds(b, 1), :]
    x2 = x_ref[0] + row[:, 2 * d:] * _dot(yg_ref[0], w_ref[...])
    ms = jnp.mean(x2 * x2, axis=-1, keepdims=True)
    o_ref[0] = x2 * lax.rsqrt(ms + EPS) * g_ref[...]


def _outproj_final(yg, x1, w_out, mod, g):
    bsz, t, d = x1.shape
    e = w_out.shape[0]
    rb = 512
    return pl.pallas_call(
        functools.partial(_outproj_final_kernel, d=d),
        grid=(bsz, t // rb),
        in_specs=[
            pl.BlockSpec((1, rb, e), lambda b, r: (b, r, 0)),
            pl.BlockSpec((1, rb, d), lambda b, r: (b, r, 0)),
            pl.BlockSpec((e, d), lambda b, r: (0, 0)),
            pl.BlockSpec((MOD_ROWS, 3 * d), lambda b, r: (0, 0)),
            pl.BlockSpec((1, d), lambda b, r: (0, 0)),
        ],
        out_specs=pl.BlockSpec((1, rb, d), lambda b, r: (b, r, 0)),
        out_shape=jax.ShapeDtypeStruct((bsz, t, d), F32),
        compiler_params=pltpu.CompilerParams(
            dimension_semantics=("arbitrary", "arbitrary"), vmem_limit_bytes=VMEM_LIMIT),
        name="outproj1",
    )(yg, x1, w_out, mod, g.reshape(1, d))


def _shift_rows(xv, k):
    n = xv.shape[0]
    rid = lax.broadcasted_iota(jnp.int32, xv.shape, 0)
    rolled = pltpu.roll(xv, k % n, 0)
    valid = (rid >= k) if k > 0 else (rid < n + k)
    return jnp.where(valid, rolled, 0.0)


def _tile_scan(a, u, reverse):
    n = a.shape[0]
    rid = lax.broadcasted_iota(jnp.int32, a.shape, 0)
    s = 1
    while s < n:
        valid = (rid < n - s) if reverse else (rid >= s)
        sh = (n - s) if reverse else s
        a_sh = jnp.where(valid, pltpu.roll(a, sh, 0), 1.0)
        u_sh = jnp.where(valid, pltpu.roll(u, sh, 0), 0.0)
        u = u + a * u_sh
        a = a * a_sh
        s *= 2
    return a, u


def _rglru_kernel(hx_ref, hc_ref, wx_ref, wg_ref, cw_ref, cb_ref, wa_ref, wi_ref, ba_ref, bi_ref,
                  lam_ref, y_ref,
                  xp_s, xc_s, sg_s, af_s, uf_s, ab_s, ub_s,
                  *, tc, t):
    gw = RG_BLOCK
    gr = GRID_W
    n_tr = t // gr
    a_refs = (af_s, ab_s)
    u_refs = (uf_s, ub_s)
    cw = cw_ref[...]
    cb = cb_ref[...]

    def conv_taps(xm2, xm1, x0, xp1):
        return cw[0:1] * xm2 + cw[1:2] * xm1 + cw[2:3] * x0 + cw[3:4] * xp1 + cb

    xcx = _dot(hc_ref[0], wx_ref[...])
    xc_s[0:tc, :] = conv_taps(_shift_rows(xcx, 2), _shift_rows(xcx, 1), xcx, _shift_rows(xcx, -1))

    pad = 2 * gr
    nb = t // ROW_BLOCK
    hsp = (0.5 * RG_C) * jax.nn.softplus(-lam_ref[...])
    hba = 0.5 * ba_ref[...]
    hbi = 0.5 * bi_ref[...]

    def lat_r0(i):
        return i * ROW_BLOCK if isinstance(i, int) else pl.multiple_of(i * ROW_BLOCK, ROW_BLOCK)

    def proj_x(i):
        r0 = lat_r0(i)
        xp_s[pl.ds(pad + r0, ROW_BLOCK), :] = _dot(hx_ref[0, pl.ds(r0, ROW_BLOCK), :], wx_ref[...])

    def proj_g(i):
        r0 = lat_r0(i)
        sg_s[pl.ds(r0, ROW_BLOCK), :] = _silu(
            _dot(hx_ref[0, pl.ds(r0, ROW_BLOCK), :], wg_ref[...])).astype(BF16)

    def conv_block(i):
        for k in range(ROW_BLOCK // gr):
            r0 = lat_r0(i) + k * gr
            xc_s[pl.ds(tc + r0, gr), :] = conv_taps(
                xp_s[pl.ds(r0, gr), :], xp_s[pl.ds(r0 + gr, gr), :],
                xp_s[pl.ds(r0 + 2 * gr, gr), :], xp_s[pl.ds(r0 + 3 * gr, gr), :])

    tot = {"H": jnp.zeros((gr, gw), F32), "A": jnp.ones((gr, gw), F32),
           "E": jnp.zeros((gr, gw), F32), "G": jnp.ones((gr, gw), F32)}

    def gates(r0, d, latent):
        for part in range(ROW_BLOCK // GATE_ROWS):
            gates_part(r0 + part * GATE_ROWS, d, latent)

    def gates_part(r0, d, latent):
        rows = pl.ds(r0, GATE_ROWS)
        xc = xc_s[rows, :]
        xcb = xc.astype(BF16)
        hxc = 0.5 * xc
        tr = jnp.tanh(_dot(xcb, wa_ref[d, 0]) + hba[d:d + 1, :])
        ti = jnp.tanh(_dot(xcb, wi_ref[d, 0]) + hbi[d:d + 1, :])
        nla = hsp[d:d + 1, :] + hsp[d:d + 1, :] * tr
        a = jnp.exp(-nla)
        a_refs[d][rows, :] = a
        z = jnp.tanh(nla) * (a * a + 1.0)
        u = (z * lax.rsqrt(jnp.maximum(z, F32_TINY))) * (hxc + hxc * ti)
        u_refs[d][rows, :] = u
        if not latent:
            return
        for k in range(GATE_ROWS // gr):
            ak, uk = a[k * gr:(k + 1) * gr, :], u[k * gr:(k + 1) * gr, :]
            if d == 0:
                tot["H"] = ak * tot["H"] + uk
                tot["A"] = ak * tot["A"]
            else:
                tot["E"] = tot["E"] + tot["G"] * uk
                tot["G"] = tot["G"] * ak

    proj_x(nb - 1)
    for i in range(tc // ROW_BLOCK):
        gates(i * ROW_BLOCK, 0, False)
    proj_g(nb - 1)
    for i in range(tc // ROW_BLOCK):
        gates(i * ROW_BLOCK, 1, False)
    xp_s[0:gr, :] = _shift_rows(xp_s[pad + (n_tr - 2) * gr:pad + (n_tr - 1) * gr, :], 1)
    xp_s[gr:pad, :] = _shift_rows(xp_s[pad + (n_tr - 1) * gr:pad + n_tr * gr, :], 1)
    proj_x(0)
    proj_g(0)
    xp_s[pad + n_tr * gr:pad + (n_tr + 1) * gr, :] = _shift_rows(xp_s[pad:pad + gr, :], -1)

    def pipe_body(k):
        proj_x(k + 1)
        conv_block(k)
        gates(tc + lat_r0(k), 0, True)
        proj_g(k + 1)
        gates(tc + lat_r0(k), 1, True)

    for k in range(nb - 2):
        pipe_body(k)
    for i in (nb - 2, nb - 1):
        conv_block(i)
        gates(tc + lat_r0(i), 0, True)
        gates(tc + lat_r0(i), 1, True)

    n_ct = tc // 8

    def ctx_body(i, carry):
        hf, hb = carry
        rf = pl.ds(pl.multiple_of(i * 8, 8), 8)
        a, u = _tile_scan(af_s[rf, :], uf_s[rf, :], False)
        hf = a[7:8] * hf + u[7:8]
        rb = pl.ds(pl.multiple_of((n_ct - 1 - i) * 8, 8), 8)
        a, u = _tile_scan(ab_s[rb, :], ub_s[rb, :], True)
        hb = a[0:1] * hb + u[0:1]
        return hf, hb

    zero = jnp.zeros((1, gw), F32)
    hf, hb = lax.fori_loop(0, n_ct, ctx_body, (zero, zero))

    def column_carries(e_tot, p_tot, h0, reverse):
        rid = lax.broadcasted_iota(jnp.int32, (8, gw), 0)
        order = range(gr // 8 - 1, -1, -1) if reverse else range(gr // 8)
        carry, out = h0, [None] * (gr // 8)
        for i in order:
            a_t, u_t = _tile_scan(p_tot[i * 8:(i + 1) * 8, :], e_tot[i * 8:(i + 1) * 8, :], reverse)
            h = u_t + a_t * carry
            if reverse:
                out[i] = jnp.where(rid == 7, carry, pltpu.roll(h, 7, 0))
                carry = h[0:1]
            else:
                out[i] = jnp.where(rid == 0, carry, pltpu.roll(h, 1, 0))
                carry = h[7:8]
        return jnp.concatenate(out, axis=0)

    h = column_carries(tot["H"], tot["A"], hf, False)
    for r in range(n_tr):
        rows = slice(tc + r * gr, tc + (r + 1) * gr)
        h = af_s[rows, :] * h + uf_s[rows, :]
        uf_s[rows, :] = h
    h = column_carries(tot["E"], tot["G"], hb, True)
    for r in range(n_tr - 1, -1, -1):
        rows = slice(tc + r * gr, tc + (r + 1) * gr)
        h = ab_s[rows, :] * h + ub_s[rows, :]
        gate = sg_s[r * gr:(r + 1) * gr, :].astype(F32)
        y_ref[0, r * gr:(r + 1) * gr, :] = ((uf_s[rows, :] + h) * gate).astype(BF16)


def _rglru(hx, hc, w_in, conv_w, conv_b, w_a, b_a, w_x, b_x, lam):
    bsz, t, d = hx.shape
    tc = hc.shape[1]
    e = w_in.shape[1] // 2
    gw = RG_BLOCK
    nb = e // gw
    n_rows = tc + t
    seq_f32 = pltpu.VMEM((n_rows, gw), F32)
    vec = lambda rows: pl.BlockSpec((rows, gw), lambda b, j: (0, j))
    gate_w = pl.BlockSpec((2, 1, gw, gw), lambda b, j: (0, j, 0, 0))
    return pl.pallas_call(
        functools.partial(_rglru_kernel, tc=tc, t=t),
        grid=(bsz, nb),
        in_specs=[
            pl.BlockSpec((1, t, d), lambda b, j: (b, 0, 0)),
            pl.BlockSpec((1, tc, d), lambda b, j: (b, 0, 0)),
            pl.BlockSpec((d, gw), lambda b, j: (0, j)),
            pl.BlockSpec((d, gw), lambda b, j: (0, nb + j)),
            vec(conv_w.shape[0]), vec(1),
            gate_w, gate_w, vec(2), vec(2), vec(2),
        ],
        out_specs=pl.BlockSpec((1, t, gw), lambda b, j: (b, 0, j)),
        out_shape=jax.ShapeDtypeStruct((bsz, t, e), BF16),
        scratch_shapes=[
            pltpu.VMEM((t + 3 * GRID_W, gw), F32),
            seq_f32,
            pltpu.VMEM((t, gw), BF16),
            seq_f32, seq_f32, seq_f32, seq_f32,
        ],
        compiler_params=pltpu.CompilerParams(
            dimension_semantics=("arbitrary", "arbitrary"), vmem_limit_bytes=VMEM_LIMIT),
        name="rglru",
    )(hx, hc, w_in, w_in, conv_w, conv_b.reshape(1, e), w_a, w_x, b_a, b_x, lam)


def kernel(x, c, ctx, c_ctx, ada_w, ada_b, norm_g, final_norm_g, hg_w_in, hg_lower_bounds,
           hg_norm_g, hg_w_out, rg_w_in, rg_conv_w, rg_conv_b, rg_w_a, rg_b_a, rg_w_x, rg_b_x,
           rg_lambda, rg_w_out):
    bsz, t, d = x.shape
    assert ada_w.shape[0] == 2 and bsz < MOD_ROWS and t % GRID_W == 0
    cond = jnp.concatenate(
        [c, c_ctx[None, :], jnp.zeros((MOD_ROWS - bsz - 1, d), F32)], axis=0)
    mod = _ada(cond, ada_w, ada_b)

    hx, hc = _prenorm(x, ctx, mod[0], norm_g[0])
    ogx, ogc = _hgrn(hx, hc, hg_w_in[0].astype(BF16), hg_lower_bounds, hg_norm_g[0], layer=0)
    x1, hx1, hc1 = _outproj_mid(ogx, ogc, x, ctx, hg_w_out[0].astype(BF16), mod[0], mod[1], norm_g[1])
    yg = _rglru(hx1, hc1, rg_w_in[0].astype(BF16), rg_conv_w[0], rg_conv_b[0],
                (0.5 * rg_w_a[0]).astype(BF16), rg_b_a[0], (0.5 * rg_w_x[0]).astype(BF16), rg_b_x[0],
                rg_lambda[0])
    return _outproj_final(yg, x1, rg_w_out[0].astype(BF16), mod[1], final_norm_g)
```

```python
import functools

import jax
import jax.numpy as jnp
from jax import lax
from jax.experimental import pallas as pl
from jax.experimental.pallas import tpu as pltpu

F32 = jnp.float32
BF16 = jnp.bfloat16

EPS = 1e-6
F32_TINY = 1.1754944e-38
GRID_W = 64
HG_KEY_DIM = 128
HG_CHUNK = 64
HG_MAX_HALF_DECAY = 80.0
RG_BLOCK = 256
RG_C = 8.0
MOD_ROWS = 16
ROW_BLOCK = 256
GATE_ROWS = 64
HG_BLOCK = 512
VMEM_LIMIT = 56 * 1024 * 1024

_NT = (((1,), (1,)), ((), ()))
_TN = (((0,), (0,)), ((), ()))


def _half_tanh(z):
    return jnp.tanh(0.5 * z)


def _silu(z):
    hz = 0.5 * z
    return hz + hz * jnp.tanh(hz)


def _dot(a, b):
    return jnp.dot(a, b, preferred_element_type=F32)


def _ada_kernel(c_ref, w_ref, b_ref, o_ref):
    s = _silu(c_ref[...])
    o_ref[0] = jnp.dot(s, w_ref[0], preferred_element_type=F32,
                       precision=lax.Precision.HIGHEST) + b_ref[0]


def _ada(cond, ada_w, ada_b):
    depth, d, n3 = ada_w.shape
    bn = d
    return pl.pallas_call(
        _ada_kernel,
        grid=(depth, n3 // bn),
        in_specs=[
            pl.BlockSpec((MOD_ROWS, d), lambda i, n: (0, 0)),
            pl.BlockSpec((1, d, bn), lambda i, n: (i, 0, n)),
            pl.BlockSpec((1, 1, bn), lambda i, n: (i, 0, n)),
        ],
        out_specs=pl.BlockSpec((1, MOD_ROWS, bn), lambda i, n: (i, 0, n)),
        out_shape=jax.ShapeDtypeStruct((depth, MOD_ROWS, n3), F32),
        name="ada",
    )(cond, ada_w, ada_b.reshape(depth, 1, n3))


def _norm_mod(xv, g, shift, scale):
    ms = jnp.mean(xv * xv, axis=-1, keepdims=True)
    return xv * lax.rsqrt(ms + EPS) * g * (1.0 + scale) + shift


def _prenorm_kernel(x_ref, ctx_ref, mod_ref, g_ref, hx_ref, hc_ref, *, d, ctx_row):
    b = pl.program_id(0)
    g = g_ref[...]
    row = mod_ref[pl.ds(b, 1), :]
    hx_ref[0] = _norm_mod(x_ref[0], g, row[:, :d], row[:, d:2 * d]).astype(BF16)

    @pl.when(pl.program_id(1) == 0)
    def _():
        rowc = mod_ref[ctx_row:ctx_row + 1, :]
        hc_ref[0] = _norm_mod(ctx_ref[0], g, rowc[:, :d], rowc[:, d:2 * d]).astype(BF16)


def _prenorm(x, ctx, mod, g):
    bsz, t, d = x.shape
    tc = ctx.shape[1]
    rb = 512
    return pl.pallas_call(
        functools.partial(_prenorm_kernel, d=d, ctx_row=bsz),
        grid=(bsz, t // rb),
        in_specs=[
            pl.BlockSpec((1, rb, d), lambda b, r: (b, r, 0)),
            pl.BlockSpec((1, tc, d), lambda b, r: (b, 0, 0)),
            pl.BlockSpec((MOD_ROWS, 3 * d), lambda b, r: (0, 0)),
            pl.BlockSpec((1, d), lambda b, r: (0, 0)),
        ],
        out_specs=[
            pl.BlockSpec((1, rb, d), lambda b, r: (b, r, 0)),
            pl.BlockSpec((1, tc, d), lambda b, r: (b, 0, 0)),
        ],
        out_shape=[jax.ShapeDtypeStruct((bsz, t, d), BF16),
                   jax.ShapeDtypeStruct((bsz, tc, d), BF16)],
        name="prenorm",
    )(x, ctx, mod, g.reshape(1, d))


def _hgrn_kernel(hx_ref, hc_ref, wq_ref, wv_ref, wf_ref, wb_ref, wg_ref, lbp_ref, ng_ref, tri_ref,
                 ox_ref, oc_ref, decay_ref,
                 qa_b, va_b, lffa_b, lfba_b, kfa_b, kba_b, qb_b, vb_b, lffb_b, lfbb_b, kfb_b, kbb_b,
                 qtf_s, qtb_s, sg_s, o_s, oif_s, oib_s, ut_s, em_s,
                 *, layer, tc, t, heads):
    n_rows = tc + t
    c = HG_CHUNK
    kd = HG_KEY_DIM
    qt_s = (qtf_s, qtb_s)
    oi_s = (oif_s, oib_s)

    lbp = lbp_ref[...]
    ex = jnp.exp(lbp - jnp.max(lbp, axis=0, keepdims=True))
    lb = jnp.sum(ex[:layer + 1], axis=0, keepdims=True) / jnp.sum(ex, axis=0, keepdims=True)
    f_mid = 0.5 * (1.0 + lb)
    f_half = 0.5 * (1.0 - lb)

    rid = lax.broadcasted_iota(jnp.int32, (c, c), 0)
    cid = lax.broadcasted_iota(jnp.int32, (c, c), 1)
    masks = (rid >= cid, rid <= cid)

    def split3(xv):
        hi = xv.astype(BF16)
        r1 = xv - hi.astype(F32)
        mid = r1.astype(BF16)
        lo = (r1 - mid.astype(F32)).astype(BF16)
        return jnp.concatenate([hi, mid, lo], axis=0)

    def proj_steps(hrows, buf, r0):
        nr = hrows.shape[0]
        q_b, v_b, lf_b, k_b = buf

        def step_q():
            q_b[0:nr, :] = _silu(_dot(hrows, wq_ref[...]))

        def step_v():
            v_b[0:nr, :] = _dot(hrows, wv_ref[...]).astype(BF16)

        def step_f(d):
            f = f_mid + f_half * _half_tanh(_dot(hrows, (wf_ref, wb_ref)[d][...]))
            lf_b[d][0:nr, :] = jnp.log(f)
            k_b[d][0:nr, :] = 1.0 - f

        def step_g():
            sg_s[pl.ds(r0, nr), :] = _silu(_dot(hrows, wg_ref[...])).astype(BF16)

        return [step_q, step_v, functools.partial(step_f, 0), functools.partial(step_f, 1), step_g]

    def chunk_steps(buf, nr, r0, c0):
        q_b, v_b, lf_b, k_b = buf
        crs = [slice(cc * c, (cc + 1) * c) for cc in range(nr // c)]
        vals = {}

        def step_cumsum():
            vals["b"] = [[_dot(tri_ref[d], split3(lf_b[d][cr, :])) for d in range(2)] for cr in crs]

        def step_scale():
            qts, kts = [], []
            for cc, cr in enumerate(crs):
                qv = q_b[cr, :]
                qt, kt = [], []
                for d in range(2):
                    bsum = vals["b"][cc][d]
                    m = 0.5 * (bsum[c - 1:c, :] if d == 0 else bsum[0:1, :])
                    half_decay[0] = jnp.maximum(half_decay[0], jnp.abs(m))
                    em_s[d, pl.ds(c0 + cc, 1), :] = jnp.exp(m)
                    qt.append((qv * jnp.exp(bsum - m)).astype(BF16))
                    kt.append((k_b[d][cr, :] * jnp.exp(m - bsum)).astype(BF16))
                    qt_s[d][pl.ds(r0 + cc * c, c), :] = qt[d]
                qts.append(qt)
                kts.append(kt)
            vals["qt"], vals["kt"] = qts, kts

        def step_scores():
            scs = []
            for cc in range(len(crs)):
                for h in range(heads):
                    cs = slice(h * kd, (h + 1) * kd)
                    qt, kt = vals["qt"][cc], vals["kt"][cc]
                    sf = lax.dot_general(qt[0][:, cs], kt[0][:, cs], _NT, preferred_element_type=F32)
                    sb = lax.dot_general(qt[1][:, cs], kt[1][:, cs], _NT, preferred_element_type=F32)
                    scs.append((jnp.where(masks[0], sf, 0.0) + jnp.where(masks[1], sb, 0.0)).astype(BF16))
            vals["sc"] = scs

        def step_out():
            for cc, cr in enumerate(crs):
                for h in range(heads):
                    cs = slice(h * kd, (h + 1) * kd)
                    vv = v_b[cr, cs]
                    kt = vals["kt"][cc]
                    o_s[pl.ds(r0 + cc * c, c), cs] = _dot(vals["sc"][cc * heads + h], vv)
                    kk = jnp.concatenate([kt[0][:, cs], kt[1][:, cs]], axis=1)
                    ut_s[c0 + cc, h] = lax.dot_general(vv, kk, _TN, preferred_element_type=F32)

        return [step_cumsum, step_scale, step_scores, step_out]

    half_decay = [jnp.zeros((1, heads * kd), F32)]

    bufs = ((qa_b, va_b, (lffa_b, lfba_b), (kfa_b, kba_b)),
            (qb_b, vb_b, (lffb_b, lfbb_b), (kfb_b, kbb_b)))
    blocks = [(hc_ref[0], 0, tc)]
    blocks += [(hx_ref[0, i * HG_BLOCK:(i + 1) * HG_BLOCK, :], tc + i * HG_BLOCK, HG_BLOCK)
               for i in range(t // HG_BLOCK)]
    units = [(d, h) for d in range(2) for h in range(heads)]
    st = {u: jnp.zeros((kd, kd), F32) for u in units}

    def scan_step(d, ci):
        rows = slice(ci * c, (ci + 1) * c)
        for h in range(heads):
            cs = slice(h * kd, (h + 1) * kd)
            em = em_s[d, ci:ci + 1, cs]
            sm = st[(d, h)] * em
            st[(d, h)] = em * (sm + ut_s[ci, h][:, d * kd:(d + 1) * kd])
            oi_s[d][rows, cs] = lax.dot_general(
                qt_s[d][rows, cs], sm.astype(BF16), _NT, preferred_element_type=F32)

    prev, fwd_now, fwd_next = [], [], []
    for n, (hrows, r0, nr) in enumerate(blocks):
        proj = proj_steps(hrows, bufs[n % 2], r0)
        share = -(-len(fwd_now) // len(proj))
        for k, step in enumerate(proj):
            step()
            if k < len(prev):
                prev[k]()
            for f in fwd_now[k * share:(k + 1) * share]:
                f()
        fwd_now = fwd_next
        fwd_next = [functools.partial(scan_step, 0, ci) for ci in range(r0 // c, (r0 + nr) // c)]
        prev = chunk_steps(bufs[n % 2], nr, r0, r0 // c)
    for f in fwd_now:
        f()
    for step in prev:
        step()
    for f in fwd_next:
        f()
    decay_ref[0, 0] = half_decay[0]

    ng = ng_ref[...]

    def readout(r0):
        rows = slice(r0, r0 + ROW_BLOCK)
        o = o_s[rows, :] + oif_s[rows, :] + oib_s[rows, :]
        parts = []
        for h in range(heads):
            cs = slice(h * kd, (h + 1) * kd)
            oh = o[:, cs]
            ms = jnp.mean(oh * oh, axis=-1, keepdims=True)
            parts.append(oh * lax.rsqrt(ms + EPS) * ng[:, cs])
        y = (jnp.concatenate(parts, axis=-1) * sg_s[rows, :].astype(F32)).astype(BF16)
        if r0 < tc:
            oc_ref[0, r0:r0 + ROW_BLOCK, :] = y
        else:
            ox_ref[0, r0 - tc:r0 - tc + ROW_BLOCK, :] = y

    n_ctx = tc // c
    n_all = n_rows // c
    cpb = ROW_BLOCK // c
    order = list(range(n_ctx - 1, -1, -1)) + list(range(n_all - 1, n_ctx - 1, -1))

    @pl.when(pl.program_id(0) >= 0)
    def _():
        for ci in order:
            scan_step(1, ci)
            if ci % cpb == 0:
                readout(ci * c)


def _hgrn(hx, hc, w_in, lower_bounds, norm_g, layer):
    bsz, t, d = hx.shape
    tc = hc.shape[1]
    e = w_in.shape[1] // 5
    heads = 2
    gw = heads * HG_KEY_DIM
    ng_blocks = e // gw
    n_rows = tc + t
    c = HG_CHUNK

    low = jnp.tril(jnp.ones((c, c), F32))
    tri = jnp.stack([jnp.tile(low, (1, 3)), jnp.tile(low.T, (1, 3))]).astype(BF16)

    def wspec(sec):
        return pl.BlockSpec((d, gw), lambda b, j, sec=sec: (0, sec * ng_blocks + j))

    seq_f32 = pltpu.VMEM((n_rows, gw), F32)
    seq_bf16 = pltpu.VMEM((n_rows, gw), BF16)
    blk_f32 = pltpu.VMEM((HG_BLOCK, gw), F32)
    blk_bf16 = pltpu.VMEM((HG_BLOCK, gw), BF16)
    n_chunks = n_rows // c
    return pl.pallas_call(
        functools.partial(_hgrn_kernel, layer=layer, tc=tc, t=t, heads=heads),
        grid=(bsz, ng_blocks),
        in_specs=[
            pl.BlockSpec((1, t, d), lambda b, j: (b, 0, 0)),
            pl.BlockSpec((1, tc, d), lambda b, j: (b, 0, 0)),
            wspec(0), wspec(1), wspec(2), wspec(3), wspec(4),
            pl.BlockSpec((lower_bounds.shape[0], gw), lambda b, j: (0, j)),
            pl.BlockSpec((1, gw), lambda b, j: (0, j)),
            pl.BlockSpec((2, c, 3 * c), lambda b, j: (0, 0, 0)),
        ],
        out_specs=[
            pl.BlockSpec((1, t, gw), lambda b, j: (b, 0, j)),
            pl.BlockSpec((1, tc, gw), lambda b, j: (b, 0, j)),
            pl.BlockSpec((1, 1, 1, gw), lambda b, j: (b, j, 0, 0)),
        ],
        out_shape=[jax.ShapeDtypeStruct((bsz, t, e), BF16),
                   jax.ShapeDtypeStruct((bsz, tc, e), BF16),
                   jax.ShapeDtypeStruct((bsz, ng_blocks, 1, gw), F32)],
        scratch_shapes=[
            blk_f32, blk_bf16, blk_f32, blk_f32, blk_f32, blk_f32,
            blk_f32, blk_bf16, blk_f32, blk_f32, blk_f32, blk_f32,
            seq_bf16, seq_bf16,
            seq_bf16,
            seq_f32, seq_f32, seq_f32,
            pltpu.VMEM((n_chunks, heads, HG_KEY_DIM, 2 * HG_KEY_DIM), F32),
            pltpu.VMEM((2, n_chunks, gw), F32),
        ],
        compiler_params=pltpu.CompilerParams(
            dimension_semantics=("arbitrary", "arbitrary"), vmem_limit_bytes=VMEM_LIMIT),
        name="hgrn2",
    )(hx, hc, w_in, w_in, w_in, w_in, w_in, lower_bounds, norm_g.reshape(1, e), tri)


def _hgrn_exact_kernel(hx_ref, hc_ref, wq_ref, wv_ref, wf_ref, wb_ref, wg_ref, lbp_ref, ng_ref,
                       ox_ref, oc_ref, q_s, v_s, ff_s, fb_s, sg_s, o_s, st_s,
                       *, layer, tc, t, heads):
    n_rows = tc + t
    kd = HG_KEY_DIM
    f_refs = (ff_s, fb_s)

    lbp = lbp_ref[...]
    ex = jnp.exp(lbp - jnp.max(lbp, axis=0, keepdims=True))
    lb = jnp.sum(ex[:layer + 1], axis=0, keepdims=True) / jnp.sum(ex, axis=0, keepdims=True)
    f_mid = 0.5 * (1.0 + lb)
    f_half = 0.5 * (1.0 - lb)

    def project(hrows, r0):
        rows = pl.ds(r0, ROW_BLOCK)
        q_s[rows, :] = _silu(_dot(hrows, wq_ref[...]))
        v_s[rows, :] = _dot(hrows, wv_ref[...])
        ff_s[rows, :] = f_mid + f_half * _half_tanh(_dot(hrows, wf_ref[...]))
        fb_s[rows, :] = f_mid + f_half * _half_tanh(_dot(hrows, wb_ref[...]))
        sg_s[rows, :] = _silu(_dot(hrows, wg_ref[...]))

    for i in range(tc // ROW_BLOCK):
        project(hc_ref[0, i * ROW_BLOCK:(i + 1) * ROW_BLOCK, :], i * ROW_BLOCK)

    def proj_body(i, carry):
        r0 = pl.multiple_of(i * ROW_BLOCK, ROW_BLOCK)
        project(hx_ref[0, pl.ds(r0, ROW_BLOCK), :], tc + r0)
        return carry

    lax.fori_loop(0, t // ROW_BLOCK, proj_body, 0)

    o_s[...] = jnp.zeros_like(o_s)
    st_s[...] = jnp.zeros_like(st_s)
    rid = lax.broadcasted_iota(jnp.int32, (8, kd), 0)
    n_ct = tc // 8
    n_tiles = n_rows // 8

    def tile_body(i, carry):
        tiles = (i, jnp.where(i < n_ct, n_ct - 1 - i, n_tiles + n_ct - 1 - i))
        for d in range(2):
            rows = pl.ds(pl.multiple_of(tiles[d] * 8, 8), 8)
            for h in range(heads):
                cs = slice(h * kd, (h + 1) * kd)
                f8, v8, q8 = f_refs[d][rows, cs], v_s[rows, cs], q_s[rows, cs]
                k8 = 1.0 - f8
                s = st_s[d, h]
                acc = jnp.zeros((8, kd), F32)
                for r in (range(8) if d == 0 else range(7, -1, -1)):
                    sel = rid == r
                    outer = lax.dot_general(jnp.where(sel, v8, 0.0), k8, _TN, preferred_element_type=F32)
                    s = s * f8[r:r + 1, :] + outer
                    acc = acc + lax.dot_general(jnp.where(sel, q8, 0.0), s, _NT,
                                                preferred_element_type=F32)
                st_s[d, h] = s
                o_s[rows, cs] = o_s[rows, cs] + acc
        return carry

    lax.fori_loop(0, n_tiles, tile_body, 0)

    ng = ng_ref[...]

    def readout(r0):
        rows = pl.ds(r0, ROW_BLOCK)
        o = o_s[rows, :]
        parts = []
        for h in range(heads):
            cs = slice(h * kd, (h + 1) * kd)
            oh = o[:, cs]
            ms = jnp.mean(oh * oh, axis=-1, keepdims=True)
            parts.append(oh * lax.rsqrt(ms + EPS) * ng[:, cs])
        return (jnp.concatenate(parts, axis=-1) * sg_s[rows, :]).astype(BF16)

    for i in range(tc // ROW_BLOCK):
        oc_ref[0, i * ROW_BLOCK:(i + 1) * ROW_BLOCK, :] = readout(i * ROW_BLOCK)

    def out_body(i, carry):
        r0 = pl.multiple_of(i * ROW_BLOCK, ROW_BLOCK)
        ox_ref[0, pl.ds(r0, ROW_BLOCK), :] = readout(tc + r0)
        return carry

    lax.fori_loop(0, t // ROW_BLOCK, out_body, 0)


def _hgrn_exact(hx, hc, w_in, lower_bounds, norm_g, layer):
    bsz, t, d = hx.shape
    tc = hc.shape[1]
    e = w_in.shape[1] // 5
    heads = 2
    gw = heads * HG_KEY_DIM
    ng_blocks = e // gw
    seq_f32 = pltpu.VMEM((tc + t, gw), F32)

    def wspec(sec):
        return pl.BlockSpec((d, gw), lambda b, j, sec=sec: (0, sec * ng_blocks + j))

    return pl.pallas_call(
        functools.partial(_hgrn_exact_kernel, layer=layer, tc=tc, t=t, heads=heads),
        grid=(bsz, ng_blocks),
        in_specs=[
            pl.BlockSpec((1, t, d), lambda b, j: (b, 0, 0)),
            pl.BlockSpec((1, tc, d), lambda b, j: (b, 0, 0)),
            wspec(0), wspec(1), wspec(2), wspec(3), wspec(4),
            pl.BlockSpec((lower_bounds.shape[0], gw), lambda b, j: (0, j)),
            pl.BlockSpec((1, gw), lambda b, j: (0, j)),
        ],
        out_specs=[
            pl.BlockSpec((1, t, gw), lambda b, j: (b, 0, j)),
            pl.BlockSpec((1, tc, gw), lambda b, j: (b, 0, j)),
        ],
        out_shape=[jax.ShapeDtypeStruct((bsz, t, e), BF16),
                   jax.ShapeDtypeStruct((bsz, tc, e), BF16)],
        scratch_shapes=[
            seq_f32, seq_f32, seq_f32, seq_f32, seq_f32,
            seq_f32,
            pltpu.VMEM((2, heads, HG_KEY_DIM, HG_KEY_DIM), F32),
        ],
        compiler_params=pltpu.CompilerParams(
            dimension_semantics=("arbitrary", "arbitrary"), vmem_limit_bytes=VMEM_LIMIT),
        name="hgrn2_exact",
    )(hx, hc, w_in, w_in, w_in, w_in, w_in, lower_bounds, norm_g.reshape(1, e))


def _outproj_mid_kernel(ogx_ref, ogc_ref, x_ref, ctx_ref, w_ref, mod0_ref, mod1_ref, g1_ref,
                        x1_ref, hx_ref, hc_ref, *, d, ctx_row):
    b = pl.program_id(0)
    g1 = g1_ref[...]
    row0 = mod0_ref[pl.ds(b, 1), :]
    row1 = mod1_ref[pl.ds(b, 1), :]
    x1 = x_ref[0] + row0[:, 2 * d:] * _dot(ogx_ref[0], w_ref[...])
    x1_ref[0] = x1
    hx_ref[0] = _norm_mod(x1, g1, row1[:, :d], row1[:, d:2 * d]).astype(BF16)

    @pl.when(pl.program_id(1) == 0)
    def _():
        c0 = mod0_ref[ctx_row:ctx_row + 1, :]
        c1 = mod1_ref[ctx_row:ctx_row + 1, :]
        ctx1 = ctx_ref[0] + c0[:, 2 * d:] * _dot(ogc_ref[0], w_ref[...])
        hc_ref[0] = _norm_mod(ctx1, g1, c1[:, :d], c1[:, d:2 * d]).astype(BF16)


def _outproj_mid(ogx, ogc, x, ctx, w_out, mod0, mod1, g1):
    bsz, t, d = x.shape
    tc = ctx.shape[1]
    e = w_out.shape[0]
    rb = 512
    return pl.pallas_call(
        functools.partial(_outproj_mid_kernel, d=d, ctx_row=bsz),
        grid=(bsz, t // rb),
        in_specs=[
            pl.BlockSpec((1, rb, e), lambda b, r: (b, r, 0)),
            pl.BlockSpec((1, tc, e), lambda b, r: (b, 0, 0)),
            pl.BlockSpec((1, rb, d), lambda b, r: (b, r, 0)),
            pl.BlockSpec((1, tc, d), lambda b, r: (b, 0, 0)),
            pl.BlockSpec((e, d), lambda b, r: (0, 0)),
            pl.BlockSpec((MOD_ROWS, 3 * d), lambda b, r: (0, 0)),
            pl.BlockSpec((MOD_ROWS, 3 * d), lambda b, r: (0, 0)),
            pl.BlockSpec((1, d), lambda b, r: (0, 0)),
        ],
        out_specs=[
            pl.BlockSpec((1, rb, d), lambda b, r: (b, r, 0)),
            pl.BlockSpec((1, rb, d), lambda b, r: (b, r, 0)),
            pl.BlockSpec((1, tc, d), lambda b, r: (b, 0, 0)),
        ],
        out_shape=[jax.ShapeDtypeStruct((bsz, t, d), F32),
                   jax.ShapeDtypeStruct((bsz, t, d), BF16),
                   jax.ShapeDtypeStruct((bsz, tc, d), BF16)],
        compiler_params=pltpu.CompilerParams(
            dimension_semantics=("arbitrary", "arbitrary"), vmem_limit_bytes=VMEM_LIMIT),
        name="outproj0",
    )(ogx, ogc, x, ctx, w_out, mod0, mod1, g1.reshape(1, d))


def _outproj_final_kernel(yg_ref, x_ref, w_ref, mod_ref, g_ref, o_ref, *, d):
    b = pl.program_id(0)
    row = mod_ref[pl.ds(b, 1), :]
    x2 = x_ref[0] + row[:, 2 * d:] * _dot(yg_ref[0], w_ref[...])
    ms = jnp.mean(x2 * x2, axis=-1, keepdims=True)
    o_ref[0] = x2 * lax.rsqrt(ms + EPS) * g_ref[...]


def _outproj_final(yg, x1, w_out, mod, g):
    bsz, t, d = x1.shape
    e = w_out.shape[0]
    rb = 512
    return pl.pallas_call(
        functools.partial(_outproj_final_kernel, d=d),
        grid=(bsz, t // rb),
        in_specs=[
            pl.BlockSpec((1, rb, e), lambda b, r: (b, r, 0)),
            pl.BlockSpec((1, rb, d), lambda b, r: (b, r, 0)),
            pl.BlockSpec((e, d), lambda b, r: (0, 0)),
            pl.BlockSpec((MOD_ROWS, 3 * d), lambda b, r: (0, 0)),
            pl.BlockSpec((1, d), lambda b, r: (0, 0)),
        ],
        out_specs=pl.BlockSpec((1, rb, d), lambda b, r: (b, r, 0)),
        out_shape=jax.ShapeDtypeStruct((bsz, t, d), F32),
        compiler_params=pltpu.CompilerParams(
            dimension_semantics=("arbitrary", "arbitrary"), vmem_limit_bytes=VMEM_LIMIT),
        name="outproj1",
    )(yg, x1, w_out, mod, g.reshape(1, d))


def _shift_rows(xv, k):
    n = xv.shape[0]
    rid = lax.broadcasted_iota(jnp.int32, xv.shape, 0)
    rolled = pltpu.roll(xv, k % n, 0)
    valid = (rid >= k) if k > 0 else (rid < n + k)
    return jnp.where(valid, rolled, 0.0)


def _tile_scan(a, u, reverse):
    n = a.shape[0]
    rid = lax.broadcasted_iota(jnp.int32, a.shape, 0)
    s = 1
    while s < n:
        valid = (rid < n - s) if reverse else (rid >= s)
        sh = (n - s) if reverse else s
        a_sh = jnp.where(valid, pltpu.roll(a, sh, 0), 1.0)
        u_sh = jnp.where(valid, pltpu.roll(u, sh, 0), 0.0)
        u = u + a * u_sh
        a = a * a_sh
        s *= 2
    return a, u


def _rglru_kernel(hx_ref, hc_ref, wx_ref, wg_ref, cw_ref, cb_ref, wa_ref, wi_ref, ba_ref, bi_ref,
                  lam_ref, y_ref,
                  xp_s, xc_s, sg_s, af_s, uf_s, ab_s, ub_s,
                  *, tc, t):
    gw = RG_BLOCK
    gr = GRID_W
    n_tr = t // gr
    a_refs = (af_s, ab_s)
    u_refs = (uf_s, ub_s)
    cw = cw_ref[...]
    cb = cb_ref[...]

    def conv_taps(xm2, xm1, x0, xp1):
        return cw[0:1] * xm2 + cw[1:2] * xm1 + cw[2:3] * x0 + cw[3:4] * xp1 + cb

    xcx = _dot(hc_ref[0], wx_ref[...])
    xc_s[0:tc, :] = conv_taps(_shift_rows(xcx, 2), _shift_rows(xcx, 1), xcx, _shift_rows(xcx, -1))

    pad = 2 * gr
    nb = t // ROW_BLOCK
    hsp = (0.5 * RG_C) * jax.nn.softplus(-lam_ref[...])
    hba = 0.5 * ba_ref[...]
    hbi = 0.5 * bi_ref[...]

    def lat_r0(i):
        return i * ROW_BLOCK if isinstance(i, int) else pl.multiple_of(i * ROW_BLOCK, ROW_BLOCK)

    def proj_x(i):
        r0 = lat_r0(i)
        xp_s[pl.ds(pad + r0, ROW_BLOCK), :] = _dot(hx_ref[0, pl.ds(r0, ROW_BLOCK), :], wx_ref[...])

    def proj_g(i):
        r0 = lat_r0(i)
        sg_s[pl.ds(r0, ROW_BLOCK), :] = _silu(
            _dot(hx_ref[0, pl.ds(r0, ROW_BLOCK), :], wg_ref[...])).astype(BF16)

    def conv_block(i):
        for k in range(ROW_BLOCK // gr):
            r0 = lat_r0(i) + k * gr
            xc_s[pl.ds(tc + r0, gr), :] = conv_taps(
                xp_s[pl.ds(r0, gr), :], xp_s[pl.ds(r0 + gr, gr), :],
                xp_s[pl.ds(r0 + 2 * gr, gr), :], xp_s[pl.ds(r0 + 3 * gr, gr), :])

    tot = {"H": jnp.zeros((gr, gw), F32), "A": jnp.ones((gr, gw), F32),
           "E": jnp.zeros((gr, gw), F32), "G": jnp.ones((gr, gw), F32)}

    def gates(r0, d, latent):
        for part in range(ROW_BLOCK // GATE_ROWS):
            gates_part(r0 + part * GATE_ROWS, d, latent)

    def gates_part(r0, d, latent):
        rows = pl.ds(r0, GATE_ROWS)
        xc = xc_s[rows, :]
        xcb = xc.astype(BF16)
        hxc = 0.5 * xc
        tr = jnp.tanh(_dot(xcb, wa_ref[d, 0]) + hba[d:d + 1, :])
        ti = jnp.tanh(_dot(xcb, wi_ref[d, 0]) + hbi[d:d + 1, :])
        nla = hsp[d:d + 1, :] + hsp[d:d + 1, :] * tr
        a = jnp.exp(-nla)
        a_refs[d][rows, :] = a
        z = jnp.tanh(nla) * (a * a + 1.0)
        u = (z * lax.rsqrt(jnp.maximum(z, F32_TINY))) * (hxc + hxc * ti)
        u_refs[d][rows, :] = u
        if not latent:
            return
        for k in range(GATE_ROWS // gr):
            ak, uk = a[k * gr:(k + 1) * gr, :], u[k * gr:(k + 1) * gr, :]
            if d == 0:
                tot["H"] = ak * tot["H"] + uk
                tot["A"] = ak * tot["A"]
            else:
                tot["E"] = tot["E"] + tot["G"] * uk
                tot["G"] = tot["G"] * ak

    proj_x(nb - 1)
    for i in range(tc // ROW_BLOCK):
        gates(i * ROW_BLOCK, 0, False)
    proj_g(nb - 1)
    for i in range(tc // ROW_BLOCK):
        gates(i * ROW_BLOCK, 1, False)
    xp_s[0:gr, :] = _shift_rows(xp_s[pad + (n_tr - 2) * gr:pad + (n_tr - 1) * gr, :], 1)
    xp_s[gr:pad, :] = _shift_rows(xp_s[pad + (n_tr - 1) * gr:pad + n_tr * gr, :], 1)
    proj_x(0)
    proj_g(0)
    xp_s[pad + n_tr * gr:pad + (n_tr + 1) * gr, :] = _shift_rows(xp_s[pad:pad + gr, :], -1)

    def pipe_body(k):
        proj_x(k + 1)
        conv_block(k)
        gates(tc + lat_r0(k), 0, True)
        proj_g(k + 1)
        gates(tc + lat_r0(k), 1, True)

    for k in range(nb - 2):
        pipe_body(k)
    for i in (nb - 2, nb - 1):
        conv_block(i)
        gates(tc + lat_r0(i), 0, True)
        gates(tc + lat_r0(i), 1, True)

    n_ct = tc // 8

    def ctx_body(i, carry):
        hf, hb = carry
        rf = pl.ds(pl.multiple_of(i * 8, 8), 8)
        a, u = _tile_scan(af_s[rf, :], uf_s[rf, :], False)
        hf = a[7:8] * hf + u[7:8]
        rb = pl.ds(pl.multiple_of((n_ct - 1 - i) * 8, 8), 8)
        a, u = _tile_scan(ab_s[rb, :], ub_s[rb, :], True)
        hb = a[0:1] * hb + u[0:1]
        return hf, hb

    zero = jnp.zeros((1, gw), F32)
    hf, hb = lax.fori_loop(0, n_ct, ctx_body, (zero, zero))

    def column_carries(e_tot, p_tot, h0, reverse):
        rid = lax.broadcasted_iota(jnp.int32, (8, gw), 0)
        order = range(gr // 8 - 1, -1, -1) if reverse else range(gr // 8)
        carry, out = h0, [None] * (gr // 8)
        for i in order:
            a_t, u_t = _tile_scan(p_tot[i * 8:(i + 1) * 8, :], e_tot[i * 8:(i + 1) * 8, :], reverse)
            h = u_t + a_t * carry
            if reverse:
                out[i] = jnp.where(rid == 7, carry, pltpu.roll(h, 7, 0))
                carry = h[0:1]
            else:
                out[i] = jnp.where(rid == 0, carry, pltpu.roll(h, 1, 0))
                carry = h[7:8]
        return jnp.concatenate(out, axis=0)

    h = column_carries(tot["H"], tot["A"], hf, False)
    for r in range(n_tr):
        rows = slice(tc + r * gr, tc + (r + 1) * gr)
        h = af_s[rows, :] * h + uf_s[rows, :]
        uf_s[rows, :] = h
    h = column_carries(tot["E"], tot["G"], hb, True)
    for r in range(n_tr - 1, -1, -1):
        rows = slice(tc + r * gr, tc + (r + 1) * gr)
        h = ab_s[rows, :] * h + ub_s[rows, :]
        gate = sg_s[r * gr:(r + 1) * gr, :].astype(F32)
        y_ref[0, r * gr:(r + 1) * gr, :] = ((uf_s[rows, :] + h) * gate).astype(BF16)


def _rglru(hx, hc, w_in, conv_w, conv_b, w_a, b_a, w_x, b_x, lam):
    bsz, t, d = hx.shape
    tc = hc.shape[1]
    e = w_in.shape[1] // 2
    gw = RG_BLOCK
    nb = e // gw
    n_rows = tc + t
    seq_f32 = pltpu.VMEM((n_rows, gw), F32)
    vec = lambda rows: pl.BlockSpec((rows, gw), lambda b, j: (0, j))
    gate_w = pl.BlockSpec((2, 1, gw, gw), lambda b, j: (0, j, 0, 0))
    return pl.pallas_call(
        functools.partial(_rglru_kernel, tc=tc, t=t),
        grid=(bsz, nb),
        in_specs=[
            pl.BlockSpec((1, t, d), lambda b, j: (b, 0, 0)),
            pl.BlockSpec((1, tc, d), lambda b, j: (b, 0, 0)),
            pl.BlockSpec((d, gw), lambda b, j: (0, j)),
            pl.BlockSpec((d, gw), lambda b, j: (0, nb + j)),
            vec(conv_w.shape[0]), vec(1),
            gate_w, gate_w, vec(2), vec(2), vec(2),
        ],
        out_specs=pl.BlockSpec((1, t, gw), lambda b, j: (b, 0, j)),
        out_shape=jax.ShapeDtypeStruct((bsz, t, e), BF16),
        scratch_shapes=[
            pltpu.VMEM((t + 3 * GRID_W, gw), F32),
            seq_f32,
            pltpu.VMEM((t, gw), BF16),
            seq_f32, seq_f32, seq_f32, seq_f32,
        ],
        compiler_params=pltpu.CompilerParams(
            dimension_semantics=("arbitrary", "arbitrary"), vmem_limit_bytes=VMEM_LIMIT),
        name="rglru",
    )(hx, hc, w_in, w_in, conv_w, conv_b.reshape(1, e), w_a, w_x, b_a, b_x, lam)


def kernel(x, c, ctx, c_ctx, ada_w, ada_b, norm_g, final_norm_g, hg_w_in, hg_lower_bounds,
           hg_norm_g, hg_w_out, rg_w_in, rg_conv_w, rg_conv_b, rg_w_a, rg_b_a, rg_w_x, rg_b_x,
           rg_lambda, rg_w_out):
    bsz, t, d = x.shape
    assert ada_w.shape[0] == 2 and bsz < MOD_ROWS and t % GRID_W == 0
    cond = jnp.concatenate(
        [c, c_ctx[None, :], jnp.zeros((MOD_ROWS - bsz - 1, d), F32)], axis=0)
    mod = _ada(cond, ada_w, ada_b)

    hx, hc = _prenorm(x, ctx, mod[0], norm_g[0])
    w_hg = hg_w_in[0].astype(BF16)
    ogx, ogc, half_decay = _hgrn(hx, hc, w_hg, hg_lower_bounds, hg_norm_g[0], layer=0)
    ogx, ogc = lax.cond(
        jnp.all(half_decay < HG_MAX_HALF_DECAY),
        lambda: (ogx, ogc),
        lambda: tuple(_hgrn_exact(hx, hc, w_hg, hg_lower_bounds, hg_norm_g[0], layer=0)))
    x1, hx1, hc1 = _outproj_mid(ogx, ogc, x, ctx, hg_w_out[0].astype(BF16), mod[0], mod[1], norm_g[1])
    yg = _rglru(hx1, hc1, rg_w_in[0].astype(BF16), rg_conv_w[0], rg_conv_b[0],
                (0.5 * rg_w_a[0]).astype(BF16), rg_b_a[0], (0.5 * rg_w_x[0]).astype(BF16), rg_b_x[0],
                rg_lambda[0])
    return _outproj_final(yg, x1, rg_w_out[0].astype(BF16), mod[1], final_norm_g)
```

```python
import functools

import jax
import jax.numpy as jnp
from jax import lax
from jax.experimental import pallas as pl
from jax.experimental.pallas import tpu as pltpu

F32 = jnp.float32
BF16 = jnp.bfloat16

EPS = 1e-6
F32_TINY = 1.1754944e-38
GRID_W = 64
HG_KEY_DIM = 128
HG_CHUNK = 64
HG_MAX_HALF_DECAY = 80.0
RG_BLOCK = 256
RG_C = 8.0
MOD_ROWS = 16
ROW_BLOCK = 256
GATE_ROWS = 64
OUT_ROWS = 128
HG_BLOCK = 512
VMEM_LIMIT = 56 * 1024 * 1024

_NT = (((1,), (1,)), ((), ()))
_TN = (((0,), (0,)), ((), ()))


def _half_tanh(z):
    return jnp.tanh(0.5 * z)


def _silu(z):
    hz = 0.5 * z
    return hz + hz * jnp.tanh(hz)


def _dot(a, b):
    return jnp.dot(a, b, preferred_element_type=F32)


def _ada_kernel(c_ref, w_ref, b_ref, o_ref):
    s = _silu(c_ref[...])
    o_ref[0] = jnp.dot(s, w_ref[0], preferred_element_type=F32,
                       precision=lax.Precision.HIGHEST) + b_ref[0]


def _ada(cond, ada_w, ada_b):
    depth, d, n3 = ada_w.shape
    bn = d
    return pl.pallas_call(
        _ada_kernel,
        grid=(depth, n3 // bn),
        in_specs=[
            pl.BlockSpec((MOD_ROWS, d), lambda i, n: (0, 0)),
            pl.BlockSpec((1, d, bn), lambda i, n: (i, 0, n)),
            pl.BlockSpec((1, 1, bn), lambda i, n: (i, 0, n)),
        ],
        out_specs=pl.BlockSpec((1, MOD_ROWS, bn), lambda i, n: (i, 0, n)),
        out_shape=jax.ShapeDtypeStruct((depth, MOD_ROWS, n3), F32),
        name="ada",
    )(cond, ada_w, ada_b.reshape(depth, 1, n3))


def _norm_mod(xv, g, shift, scale):
    ms = jnp.mean(xv * xv, axis=-1, keepdims=True)
    return xv * lax.rsqrt(ms + EPS) * g * (1.0 + scale) + shift


def _prenorm_kernel(x_ref, ctx_ref, mod_ref, g_ref, hx_ref, hc_ref, *, d, ctx_row):
    b = pl.program_id(0)
    g = g_ref[...]
    row = mod_ref[pl.ds(b, 1), :]
    hx_ref[0] = _norm_mod(x_ref[0], g, row[:, :d], row[:, d:2 * d]).astype(BF16)

    @pl.when(pl.program_id(1) == 0)
    def _():
        rowc = mod_ref[ctx_row:ctx_row + 1, :]
        hc_ref[0] = _norm_mod(ctx_ref[0], g, rowc[:, :d], rowc[:, d:2 * d]).astype(BF16)


def _prenorm(x, ctx, mod, g):
    bsz, t, d = x.shape
    tc = ctx.shape[1]
    rb = 512
    return pl.pallas_call(
        functools.partial(_prenorm_kernel, d=d, ctx_row=bsz),
        grid=(bsz, t // rb),
        in_specs=[
            pl.BlockSpec((1, rb, d), lambda b, r: (b, r, 0)),
            pl.BlockSpec((1, tc, d), lambda b, r: (b, 0, 0)),
            pl.BlockSpec((MOD_ROWS, 3 * d), lambda b, r: (0, 0)),
            pl.BlockSpec((1, d), lambda b, r: (0, 0)),
        ],
        out_specs=[
            pl.BlockSpec((1, rb, d), lambda b, r: (b, r, 0)),
            pl.BlockSpec((1, tc, d), lambda b, r: (b, 0, 0)),
        ],
        out_shape=[jax.ShapeDtypeStruct((bsz, t, d), BF16),
                   jax.ShapeDtypeStruct((bsz, tc, d), BF16)],
        name="prenorm",
    )(x, ctx, mod, g.reshape(1, d))


def _hgrn_kernel(hx_ref, hc_ref, wq_ref, wv_ref, wf_ref, wb_ref, wg_ref, lbp_ref, ng_ref, tri_ref,
                 ox_ref, oc_ref, decay_ref,
                 qa_b, va_b, lffa_b, lfba_b, kfa_b, kba_b, qb_b, vb_b, lffb_b, lfbb_b, kfb_b, kbb_b,
                 qtf_s, qtb_s, sg_s, o_s, oif_s, oib_s, ut_s, em_s,
                 *, layer, tc, t, heads):
    n_rows = tc + t
    c = HG_CHUNK
    kd = HG_KEY_DIM
    qt_s = (qtf_s, qtb_s)
    oi_s = (oif_s, oib_s)

    lbp = lbp_ref[...]
    ex = jnp.exp(lbp - jnp.max(lbp, axis=0, keepdims=True))
    lb = jnp.sum(ex[:layer + 1], axis=0, keepdims=True) / jnp.sum(ex, axis=0, keepdims=True)
    f_mid = 0.5 * (1.0 + lb)
    f_half = 0.5 * (1.0 - lb)

    rid = lax.broadcasted_iota(jnp.int32, (c, c), 0)
    cid = lax.broadcasted_iota(jnp.int32, (c, c), 1)
    masks = (rid >= cid, rid <= cid)

    def split3(xv):
        hi = xv.astype(BF16)
        r1 = xv - hi.astype(F32)
        mid = r1.astype(BF16)
        lo = (r1 - mid.astype(F32)).astype(BF16)
        return jnp.concatenate([hi, mid, lo], axis=0)

    def proj_steps(hrows, buf, r0):
        nr = hrows.shape[0]
        q_b, v_b, lf_b, k_b = buf

        def step_q():
            q_b[0:nr, :] = _silu(_dot(hrows, wq_ref[...]))

        def step_v():
            v_b[0:nr, :] = _dot(hrows, wv_ref[...]).astype(BF16)

        def step_f(d):
            f = f_mid + f_half * _half_tanh(_dot(hrows, (wf_ref, wb_ref)[d][...]))
            lf_b[d][0:nr, :] = jnp.log(f)
            k_b[d][0:nr, :] = 1.0 - f

        def step_g():
            sg_s[pl.ds(r0, nr), :] = _silu(_dot(hrows, wg_ref[...])).astype(BF16)

        return [step_q, step_v, functools.partial(step_f, 0), functools.partial(step_f, 1), step_g]

    def chunk_steps(buf, nr, r0, c0):
        q_b, v_b, lf_b, k_b = buf
        crs = [slice(cc * c, (cc + 1) * c) for cc in range(nr // c)]
        vals = {}

        def step_cumsum():
            vals["b"] = [[_dot(tri_ref[d], split3(lf_b[d][cr, :])) for d in range(2)] for cr in crs]

        def step_scale():
            qts, kts = [], []
            for cc, cr in enumerate(crs):
                qv = q_b[cr, :]
                qt, kt = [], []
                for d in range(2):
                    bsum = vals["b"][cc][d]
                    m = 0.5 * (bsum[c - 1:c, :] if d == 0 else bsum[0:1, :])
                    half_decay[0] = jnp.maximum(half_decay[0], jnp.abs(m))
                    em_s[d, pl.ds(c0 + cc, 1), :] = jnp.exp(m)
                    qt.append((qv * jnp.exp(bsum - m)).astype(BF16))
                    kt.append((k_b[d][cr, :] * jnp.exp(m - bsum)).astype(BF16))
                    qt_s[d][pl.ds(r0 + cc * c, c), :] = qt[d]
                qts.append(qt)
                kts.append(kt)
            vals["qt"], vals["kt"] = qts, kts

        def step_scores():
            scs = []
            for cc in range(len(crs)):
                for h in range(heads):
                    cs = slice(h * kd, (h + 1) * kd)
                    qt, kt = vals["qt"][cc], vals["kt"][cc]
                    sf = lax.dot_general(qt[0][:, cs], kt[0][:, cs], _NT, preferred_element_type=F32)
                    sb = lax.dot_general(qt[1][:, cs], kt[1][:, cs], _NT, preferred_element_type=F32)
                    scs.append((jnp.where(masks[0], sf, 0.0) + jnp.where(masks[1], sb, 0.0)).astype(BF16))
            vals["sc"] = scs

        def step_out():
            for cc, cr in enumerate(crs):
                for h in range(heads):
                    cs = slice(h * kd, (h + 1) * kd)
                    vv = v_b[cr, cs]
                    kt = vals["kt"][cc]
                    o_s[pl.ds(r0 + cc * c, c), cs] = _dot(vals["sc"][cc * heads + h], vv)
                    kk = jnp.concatenate([kt[0][:, cs], kt[1][:, cs]], axis=1)
                    ut_s[c0 + cc, h] = lax.dot_general(vv, kk, _TN, preferred_element_type=F32)

        return [step_cumsum, step_scale, step_scores, step_out]

    half_decay = [jnp.zeros((1, heads * kd), F32)]

    bufs = ((qa_b, va_b, (lffa_b, lfba_b), (kfa_b, kba_b)),
            (qb_b, vb_b, (lffb_b, lfbb_b), (kfb_b, kbb_b)))
    blocks = [(hc_ref[0], 0, tc)]
    blocks += [(hx_ref[0, i * HG_BLOCK:(i + 1) * HG_BLOCK, :], tc + i * HG_BLOCK, HG_BLOCK)
               for i in range(t // HG_BLOCK)]
    units = [(d, h) for d in range(2) for h in range(heads)]
    st = {u: jnp.zeros((kd, kd), F32) for u in units}

    def scan_step(d, ci):
        rows = slice(ci * c, (ci + 1) * c)
        for h in range(heads):
            cs = slice(h * kd, (h + 1) * kd)
            em = em_s[d, ci:ci + 1, cs]
            sm = st[(d, h)] * em
            st[(d, h)] = em * (sm + ut_s[ci, h][:, d * kd:(d + 1) * kd])
            oi_s[d][rows, cs] = lax.dot_general(
                qt_s[d][rows, cs], sm.astype(BF16), _NT, preferred_element_type=F32)

    prev, fwd_now, fwd_next = [], [], []
    for n, (hrows, r0, nr) in enumerate(blocks):
        proj = proj_steps(hrows, bufs[n % 2], r0)
        share = -(-len(fwd_now) // len(proj))
        for k, step in enumerate(proj):
            step()
            if k < len(prev):
                prev[k]()
            for f in fwd_now[k * share:(k + 1) * share]:
                f()
        fwd_now = fwd_next
        fwd_next = [functools.partial(scan_step, 0, ci) for ci in range(r0 // c, (r0 + nr) // c)]
        prev = chunk_steps(bufs[n % 2], nr, r0, r0 // c)
    for f in fwd_now:
        f()
    for step in prev:
        step()
    for f in fwd_next:
        f()
    decay_ref[0, 0] = half_decay[0]

    ng = ng_ref[...]

    def readout(r0):
        rows = slice(r0, r0 + ROW_BLOCK)
        o = o_s[rows, :] + oif_s[rows, :] + oib_s[rows, :]
        parts = []
        for h in range(heads):
            cs = slice(h * kd, (h + 1) * kd)
            oh = o[:, cs]
            ms = jnp.mean(oh * oh, axis=-1, keepdims=True)
            parts.append(oh * lax.rsqrt(ms + EPS) * ng[:, cs])
        y = (jnp.concatenate(parts, axis=-1) * sg_s[rows, :].astype(F32)).astype(BF16)
        if r0 < tc:
            oc_ref[0, r0:r0 + ROW_BLOCK, :] = y
        else:
            ox_ref[0, r0 - tc:r0 - tc + ROW_BLOCK, :] = y

    n_ctx = tc // c
    n_all = n_rows // c
    cpb = ROW_BLOCK // c
    order = list(range(n_ctx - 1, -1, -1)) + list(range(n_all - 1, n_ctx - 1, -1))

    @pl.when(pl.program_id(0) >= 0)
    def _():
        for ci in order:
            scan_step(1, ci)
            if ci % cpb == 0:
                readout(ci * c)


def _hgrn(hx, hc, w_in, lower_bounds, norm_g, layer):
    bsz, t, d = hx.shape
    tc = hc.shape[1]
    e = w_in.shape[1] // 5
    heads = 2
    gw = heads * HG_KEY_DIM
    ng_blocks = e // gw
    n_rows = tc + t
    c = HG_CHUNK

    low = jnp.tril(jnp.ones((c, c), F32))
    tri = jnp.stack([jnp.tile(low, (1, 3)), jnp.tile(low.T, (1, 3))]).astype(BF16)

    def wspec(sec):
        return pl.BlockSpec((d, gw), lambda b, j, sec=sec: (0, sec * ng_blocks + j))

    seq_f32 = pltpu.VMEM((n_rows, gw), F32)
    seq_bf16 = pltpu.VMEM((n_rows, gw), BF16)
    blk_f32 = pltpu.VMEM((HG_BLOCK, gw), F32)
    blk_bf16 = pltpu.VMEM((HG_BLOCK, gw), BF16)
    n_chunks = n_rows // c
    return pl.pallas_call(
        functools.partial(_hgrn_kernel, layer=layer, tc=tc, t=t, heads=heads),
        grid=(bsz, ng_blocks),
        in_specs=[
            pl.BlockSpec((1, t, d), lambda b, j: (b, 0, 0)),
            pl.BlockSpec((1, tc, d), lambda b, j: (b, 0, 0)),
            wspec(0), wspec(1), wspec(2), wspec(3), wspec(4),
            pl.BlockSpec((lower_bounds.shape[0], gw), lambda b, j: (0, j)),
            pl.BlockSpec((1, gw), lambda b, j: (0, j)),
            pl.BlockSpec((2, c, 3 * c), lambda b, j: (0, 0, 0)),
        ],
        out_specs=[
            pl.BlockSpec((1, t, gw), lambda b, j: (b, 0, j)),
            pl.BlockSpec((1, tc, gw), lambda b, j: (b, 0, j)),
            pl.BlockSpec((1, 1, 1, gw), lambda b, j: (b, j, 0, 0)),
        ],
        out_shape=[jax.ShapeDtypeStruct((bsz, t, e), BF16),
                   jax.ShapeDtypeStruct((bsz, tc, e), BF16),
                   jax.ShapeDtypeStruct((bsz, ng_blocks, 1, gw), F32)],
        scratch_shapes=[
            blk_f32, blk_bf16, blk_f32, blk_f32, blk_f32, blk_f32,
            blk_f32, blk_bf16, blk_f32, blk_f32, blk_f32, blk_f32,
            seq_bf16, seq_bf16,
            seq_bf16,
            seq_f32, seq_f32, seq_f32,
            pltpu.VMEM((n_chunks, heads, HG_KEY_DIM, 2 * HG_KEY_DIM), F32),
            pltpu.VMEM((2, n_chunks, gw), F32),
        ],
        compiler_params=pltpu.CompilerParams(
            dimension_semantics=("arbitrary", "arbitrary"), vmem_limit_bytes=VMEM_LIMIT),
        name="hgrn2",
    )(hx, hc, w_in, w_in, w_in, w_in, w_in, lower_bounds, norm_g.reshape(1, e), tri)


def _hgrn_exact_kernel(hx_ref, hc_ref, wq_ref, wv_ref, wf_ref, wb_ref, wg_ref, lbp_ref, ng_ref,
                       ox_ref, oc_ref, q_s, v_s, ff_s, fb_s, sg_s, o_s, st_s,
                       *, layer, tc, t, heads):
    n_rows = tc + t
    kd = HG_KEY_DIM
    f_refs = (ff_s, fb_s)

    lbp = lbp_ref[...]
    ex = jnp.exp(lbp - jnp.max(lbp, axis=0, keepdims=True))
    lb = jnp.sum(ex[:layer + 1], axis=0, keepdims=True) / jnp.sum(ex, axis=0, keepdims=True)
    f_mid = 0.5 * (1.0 + lb)
    f_half = 0.5 * (1.0 - lb)

    def project(hrows, r0):
        rows = pl.ds(r0, ROW_BLOCK)
        q_s[rows, :] = _silu(_dot(hrows, wq_ref[...]))
        v_s[rows, :] = _dot(hrows, wv_ref[...])
        ff_s[rows, :] = f_mid + f_half * _half_tanh(_dot(hrows, wf_ref[...]))
        fb_s[rows, :] = f_mid + f_half * _half_tanh(_dot(hrows, wb_ref[...]))
        sg_s[rows, :] = _silu(_dot(hrows, wg_ref[...]))

    for i in range(tc // ROW_BLOCK):
        project(hc_ref[0, i * ROW_BLOCK:(i + 1) * ROW_BLOCK, :], i * ROW_BLOCK)

    def proj_body(i, carry):
        r0 = pl.multiple_of(i * ROW_BLOCK, ROW_BLOCK)
        project(hx_ref[0, pl.ds(r0, ROW_BLOCK), :], tc + r0)
        return carry

    lax.fori_loop(0, t // ROW_BLOCK, proj_body, 0)

    o_s[...] = jnp.zeros_like(o_s)
    st_s[...] = jnp.zeros_like(st_s)
    rid = lax.broadcasted_iota(jnp.int32, (8, kd), 0)
    n_ct = tc // 8
    n_tiles = n_rows // 8

    def tile_body(i, carry):
        tiles = (i, jnp.where(i < n_ct, n_ct - 1 - i, n_tiles + n_ct - 1 - i))
        for d in range(2):
            rows = pl.ds(pl.multiple_of(tiles[d] * 8, 8), 8)
            for h in range(heads):
                cs = slice(h * kd, (h + 1) * kd)
                f8, v8, q8 = f_refs[d][rows, cs], v_s[rows, cs], q_s[rows, cs]
                k8 = 1.0 - f8
                s = st_s[d, h]
                acc = jnp.zeros((8, kd), F32)
                for r in (range(8) if d == 0 else range(7, -1, -1)):
                    sel = rid == r
                    outer = lax.dot_general(jnp.where(sel, v8, 0.0), k8, _TN, preferred_element_type=F32)
                    s = s * f8[r:r + 1, :] + outer
                    acc = acc + lax.dot_general(jnp.where(sel, q8, 0.0), s, _NT,
                                                preferred_element_type=F32)
                st_s[d, h] = s
                o_s[rows, cs] = o_s[rows, cs] + acc
        return carry

    lax.fori_loop(0, n_tiles, tile_body, 0)

    ng = ng_ref[...]

    def readout(r0):
        rows = pl.ds(r0, ROW_BLOCK)
        o = o_s[rows, :]
        parts = []
        for h in range(heads):
            cs = slice(h * kd, (h + 1) * kd)
            oh = o[:, cs]
            ms = jnp.mean(oh * oh, axis=-1, keepdims=True)
            parts.append(oh * lax.rsqrt(ms + EPS) * ng[:, cs])
        return (jnp.concatenate(parts, axis=-1) * sg_s[rows, :]).astype(BF16)

    for i in range(tc // ROW_BLOCK):
        oc_ref[0, i * ROW_BLOCK:(i + 1) * ROW_BLOCK, :] = readout(i * ROW_BLOCK)

    def out_body(i, carry):
        r0 = pl.multiple_of(i * ROW_BLOCK, ROW_BLOCK)
        ox_ref[0, pl.ds(r0, ROW_BLOCK), :] = readout(tc + r0)
        return carry

    lax.fori_loop(0, t // ROW_BLOCK, out_body, 0)


def _hgrn_exact(hx, hc, w_in, lower_bounds, norm_g, layer):
    bsz, t, d = hx.shape
    tc = hc.shape[1]
    e = w_in.shape[1] // 5
    heads = 2
    gw = heads * HG_KEY_DIM
    ng_blocks = e // gw
    seq_f32 = pltpu.VMEM((tc + t, gw), F32)

    def wspec(sec):
        return pl.BlockSpec((d, gw), lambda b, j, sec=sec: (0, sec * ng_blocks + j))

    return pl.pallas_call(
        functools.partial(_hgrn_exact_kernel, layer=layer, tc=tc, t=t, heads=heads),
        grid=(bsz, ng_blocks),
        in_specs=[
            pl.BlockSpec((1, t, d), lambda b, j: (b, 0, 0)),
            pl.BlockSpec((1, tc, d), lambda b, j: (b, 0, 0)),
            wspec(0), wspec(1), wspec(2), wspec(3), wspec(4),
            pl.BlockSpec((lower_bounds.shape[0], gw), lambda b, j: (0, j)),
            pl.BlockSpec((1, gw), lambda b, j: (0, j)),
        ],
        out_specs=[
            pl.BlockSpec((1, t, gw), lambda b, j: (b, 0, j)),
            pl.BlockSpec((1, tc, gw), lambda b, j: (b, 0, j)),
        ],
        out_shape=[jax.ShapeDtypeStruct((bsz, t, e), BF16),
                   jax.ShapeDtypeStruct((bsz, tc, e), BF16)],
        scratch_shapes=[
            seq_f32, seq_f32, seq_f32, seq_f32, seq_f32,
            seq_f32,
            pltpu.VMEM((2, heads, HG_KEY_DIM, HG_KEY_DIM), F32),
        ],
        compiler_params=pltpu.CompilerParams(
            dimension_semantics=("arbitrary", "arbitrary"), vmem_limit_bytes=VMEM_LIMIT),
        name="hgrn2_exact",
    )(hx, hc, w_in, w_in, w_in, w_in, w_in, lower_bounds, norm_g.reshape(1, e))


def _outproj_mid_kernel(ogx_ref, ogc_ref, x_ref, ctx_ref, w_ref, mod0_ref, mod1_ref, g1_ref,
                        x1_ref, hx_ref, hc_ref, *, d, ctx_row):
    b = pl.program_id(0)
    g1 = g1_ref[...]
    row0 = mod0_ref[pl.ds(b, 1), :]
    row1 = mod1_ref[pl.ds(b, 1), :]
    for i in range(x_ref.shape[1] // OUT_ROWS):
        rows = slice(i * OUT_ROWS, (i + 1) * OUT_ROWS)
        x1 = x_ref[0, rows, :] + row0[:, 2 * d:] * _dot(ogx_ref[0, rows, :], w_ref[...])
        x1_ref[0, rows, :] = x1
        hx_ref[0, rows, :] = _norm_mod(x1, g1, row1[:, :d], row1[:, d:2 * d]).astype(BF16)

    @pl.when(pl.program_id(1) == 0)
    def _():
        c0 = mod0_ref[ctx_row:ctx_row + 1, :]
        c1 = mod1_ref[ctx_row:ctx_row + 1, :]
        for i in range(ctx_ref.shape[1] // OUT_ROWS):
            rows = slice(i * OUT_ROWS, (i + 1) * OUT_ROWS)
            ctx1 = ctx_ref[0, rows, :] + c0[:, 2 * d:] * _dot(ogc_ref[0, rows, :], w_ref[...])
            hc_ref[0, rows, :] = _norm_mod(ctx1, g1, c1[:, :d], c1[:, d:2 * d]).astype(BF16)


def _outproj_mid(ogx, ogc, x, ctx, w_out, mod0, mod1, g1):
    bsz, t, d = x.shape
    tc = ctx.shape[1]
    e = w_out.shape[0]
    rb = 512
    return pl.pallas_call(
        functools.partial(_outproj_mid_kernel, d=d, ctx_row=bsz),
        grid=(bsz, t // rb),
        in_specs=[
            pl.BlockSpec((1, rb, e), lambda b, r: (b, r, 0)),
            pl.BlockSpec((1, tc, e), lambda b, r: (b, 0, 0)),
            pl.BlockSpec((1, rb, d), lambda b, r: (b, r, 0)),
            pl.BlockSpec((1, tc, d), lambda b, r: (b, 0, 0)),
            pl.BlockSpec((e, d), lambda b, r: (0, 0)),
            pl.BlockSpec((MOD_ROWS, 3 * d), lambda b, r: (0, 0)),
            pl.BlockSpec((MOD_ROWS, 3 * d), lambda b, r: (0, 0)),
            pl.BlockSpec((1, d), lambda b, r: (0, 0)),
        ],
        out_specs=[
            pl.BlockSpec((1, rb, d), lambda b, r: (b, r, 0)),
            pl.BlockSpec((1, rb, d), lambda b, r: (b, r, 0)),
            pl.BlockSpec((1, tc, d), lambda b, r: (b, 0, 0)),
        ],
        out_shape=[jax.ShapeDtypeStruct((bsz, t, d), F32),
                   jax.ShapeDtypeStruct((bsz, t, d), BF16),
                   jax.ShapeDtypeStruct((bsz, tc, d), BF16)],
        compiler_params=pltpu.CompilerParams(
            dimension_semantics=("arbitrary", "arbitrary"), vmem_limit_bytes=VMEM_LIMIT),
        name="outproj0",
    )(ogx, ogc, x, ctx, w_out, mod0, mod1, g1.reshape(1, d))


def _outproj_final_kernel(yg_ref, x_ref, w_ref, mod_ref, g_ref, o_ref, *, d):
    b = pl.program_id(0)
    row = mod_ref[pl.ds(b, 1), :]
    for i in range(x_ref.shape[1] // OUT_ROWS):
        rows = slice(i * OUT_ROWS, (i + 1) * OUT_ROWS)
        x2 = x_ref[0, rows, :] + row[:, 2 * d:] * _dot(yg_ref[0, rows, :], w_ref[...])
        ms = jnp.mean(x2 * x2, axis=-1, keepdims=True)
        o_ref[0, rows, :] = x2 * lax.rsqrt(ms + EPS) * g_ref[...]


def _outproj_final(yg, x1, w_out, mod, g):
    bsz, t, d = x1.shape
    e = w_out.shape[0]
    rb = 512
    return pl.pallas_call(
        functools.partial(_outproj_final_kernel, d=d),
        grid=(bsz, t // rb),
        in_specs=[
            pl.BlockSpec((1, rb, e), lambda b, r: (b, r, 0)),
            pl.BlockSpec((1, rb, d), lambda b, r: (b, r, 0)),
            pl.BlockSpec((e, d), lambda b, r: (0, 0)),
            pl.BlockSpec((MOD_ROWS, 3 * d), lambda b, r: (0, 0)),
            pl.BlockSpec((1, d), lambda b, r: (0, 0)),
        ],
        out_specs=pl.BlockSpec((1, rb, d), lambda b, r: (b, r, 0)),
        out_shape=jax.ShapeDtypeStruct((bsz, t, d), F32),
        compiler_params=pltpu.CompilerParams(
            dimension_semantics=("arbitrary", "arbitrary"), vmem_limit_bytes=VMEM_LIMIT),
        name="outproj1",
    )(yg, x1, w_out, mod, g.reshape(1, d))


def _shift_rows(xv, k):
    n = xv.shape[0]
    rid = lax.broadcasted_iota(jnp.int32, xv.shape, 0)
    rolled = pltpu.roll(xv, k % n, 0)
    valid = (rid >= k) if k > 0 else (rid < n + k)
    return jnp.where(valid, rolled, 0.0)


def _tile_scan(a, u, reverse):
    n = a.shape[0]
    rid = lax.broadcasted_iota(jnp.int32, a.shape, 0)
    s = 1
    while s < n:
        valid = (rid < n - s) if reverse else (rid >= s)
        sh = (n - s) if reverse else s
        a_sh = jnp.where(valid, pltpu.roll(a, sh, 0), 1.0)
        u_sh = jnp.where(valid, pltpu.roll(u, sh, 0), 0.0)
        u = u + a * u_sh
        a = a * a_sh
        s *= 2
    return a, u


def _rglru_kernel(hx_ref, hc_ref, wx_ref, wg_ref, cw_ref, cb_ref, wa_ref, wi_ref, ba_ref, bi_ref,
                  lam_ref, y_ref,
                  xp_s, xc_s, sg_s, af_s, uf_s, ab_s, ub_s,
                  *, tc, t):
    gw = RG_BLOCK
    gr = GRID_W
    n_tr = t // gr
    a_refs = (af_s, ab_s)
    u_refs = (uf_s, ub_s)
    cw = cw_ref[...]
    cb = cb_ref[...]

    def conv_taps(xm2, xm1, x0, xp1):
        return cw[0:1] * xm2 + cw[1:2] * xm1 + cw[2:3] * x0 + cw[3:4] * xp1 + cb

    xcx = _dot(hc_ref[0], wx_ref[...])
    xc_s[0:tc, :] = conv_taps(_shift_rows(xcx, 2), _shift_rows(xcx, 1), xcx, _shift_rows(xcx, -1))

    pad = 2 * gr
    nb = t // ROW_BLOCK
    hsp = (0.5 * RG_C) * jax.nn.softplus(-lam_ref[...])
    hba = 0.5 * ba_ref[...]
    hbi = 0.5 * bi_ref[...]

    def lat_r0(i):
        return i * ROW_BLOCK if isinstance(i, int) else pl.multiple_of(i * ROW_BLOCK, ROW_BLOCK)

    def proj_x(i):
        r0 = lat_r0(i)
        xp_s[pl.ds(pad + r0, ROW_BLOCK), :] = _dot(hx_ref[0, pl.ds(r0, ROW_BLOCK), :], wx_ref[...])

    def proj_g(i):
        r0 = lat_r0(i)
        sg_s[pl.ds(r0, ROW_BLOCK), :] = _silu(
            _dot(hx_ref[0, pl.ds(r0, ROW_BLOCK), :], wg_ref[...])).astype(BF16)

    def conv_block(i):
        for k in range(ROW_BLOCK // gr):
            r0 = lat_r0(i) + k * gr
            xc_s[pl.ds(tc + r0, gr), :] = conv_taps(
                xp_s[pl.ds(r0, gr), :], xp_s[pl.ds(r0 + gr, gr), :],
                xp_s[pl.ds(r0 + 2 * gr, gr), :], xp_s[pl.ds(r0 + 3 * gr, gr), :])

    tot = {"H": jnp.zeros((gr, gw), F32), "A": jnp.ones((gr, gw), F32),
           "E": jnp.zeros((gr, gw), F32), "G": jnp.ones((gr, gw), F32)}

    def gates(r0, d, latent):
        for part in range(ROW_BLOCK // GATE_ROWS):
            gates_part(r0 + part * GATE_ROWS, d, latent)

    def gates_part(r0, d, latent):
        rows = pl.ds(r0, GATE_ROWS)
        xc = xc_s[rows, :]
        xcb = xc.astype(BF16)
        hxc = 0.5 * xc
        tr = jnp.tanh(_dot(xcb, wa_ref[d, 0]) + hba[d:d + 1, :])
        ti = jnp.tanh(_dot(xcb, wi_ref[d, 0]) + hbi[d:d + 1, :])
        nla = hsp[d:d + 1, :] + hsp[d:d + 1, :] * tr
        a = jnp.exp(-nla)
        a_refs[d][rows, :] = a
        z = jnp.tanh(nla) * (a * a + 1.0)
        u = (z * lax.rsqrt(jnp.maximum(z, F32_TINY))) * (hxc + hxc * ti)
        u_refs[d][rows, :] = u
        if not latent:
            return
        for k in range(GATE_ROWS // gr):
            ak, uk = a[k * gr:(k + 1) * gr, :], u[k * gr:(k + 1) * gr, :]
            if d == 0:
                tot["H"] = ak * tot["H"] + uk
                tot["A"] = ak * tot["A"]
            else:
                tot["E"] = tot["E"] + tot["G"] * uk
                tot["G"] = tot["G"] * ak

    proj_x(nb - 1)
    for i in range(tc // ROW_BLOCK):
        gates(i * ROW_BLOCK, 0, False)
    proj_g(nb - 1)
    for i in range(tc // ROW_BLOCK):
        gates(i * ROW_BLOCK, 1, False)
    def ctx_final(d):
        n_ct = tc // 8
        tiles = [_tile_scan(a_refs[d][i * 8:(i + 1) * 8, :], u_refs[d][i * 8:(i + 1) * 8, :], d == 1)
                 for i in range(n_ct)]
        h = jnp.zeros((1, gw), F32)
        for i in (range(n_ct) if d == 0 else range(n_ct - 1, -1, -1)):
            a, u = tiles[i]
            h = a[7:8] * h + u[7:8] if d == 0 else a[0:1] * h + u[0:1]
        return h

    hf = ctx_final(0)
    hb = ctx_final(1)
    xp_s[0:gr, :] = _shift_rows(xp_s[pad + (n_tr - 2) * gr:pad + (n_tr - 1) * gr, :], 1)
    xp_s[gr:pad, :] = _shift_rows(xp_s[pad + (n_tr - 1) * gr:pad + n_tr * gr, :], 1)
    proj_x(0)
    proj_g(0)
    xp_s[pad + n_tr * gr:pad + (n_tr + 1) * gr, :] = _shift_rows(xp_s[pad:pad + gr, :], -1)

    def pipe_body(k):
        proj_x(k + 1)
        conv_block(k)
        gates(tc + lat_r0(k), 0, True)
        proj_g(k + 1)
        gates(tc + lat_r0(k), 1, True)

    for k in range(nb - 2):
        pipe_body(k)
    for i in (nb - 2, nb - 1):
        conv_block(i)
        gates(tc + lat_r0(i), 0, True)
        gates(tc + lat_r0(i), 1, True)

    def column_carries(e_tot, p_tot, h0, reverse):
        rid = lax.broadcasted_iota(jnp.int32, (8, gw), 0)
        order = range(gr // 8 - 1, -1, -1) if reverse else range(gr // 8)
        carry, out = h0, [None] * (gr // 8)
        for i in order:
            a_t, u_t = _tile_scan(p_tot[i * 8:(i + 1) * 8, :], e_tot[i * 8:(i + 1) * 8, :], reverse)
            h = u_t + a_t * carry
            if reverse:
                out[i] = jnp.where(rid == 7, carry, pltpu.roll(h, 7, 0))
                carry = h[0:1]
            else:
                out[i] = jnp.where(rid == 0, carry, pltpu.roll(h, 1, 0))
                carry = h[7:8]
        return jnp.concatenate(out, axis=0)

    h = column_carries(tot["H"], tot["A"], hf, False)
    for r in range(n_tr):
        rows = slice(tc + r * gr, tc + (r + 1) * gr)
        h = af_s[rows, :] * h + uf_s[rows, :]
        uf_s[rows, :] = h
    h = column_carries(tot["E"], tot["G"], hb, True)
    for r in range(n_tr - 1, -1, -1):
        rows = slice(tc + r * gr, tc + (r + 1) * gr)
        h = ab_s[rows, :] * h + ub_s[rows, :]
        gate = sg_s[r * gr:(r + 1) * gr, :].astype(F32)
        y_ref[0, r * gr:(r + 1) * gr, :] = ((uf_s[rows, :] + h) * gate).astype(BF16)


def _rglru(hx, hc, w_in, conv_w, conv_b, w_a, b_a, w_x, b_x, lam):
    bsz, t, d = hx.shape
    tc = hc.shape[1]
    e = w_in.shape[1] // 2
    gw = RG_BLOCK
    nb = e // gw
    n_rows = tc + t
    seq_f32 = pltpu.VMEM((n_rows, gw), F32)
    vec = lambda rows: pl.BlockSpec((rows, gw), lambda b, j: (0, j))
    gate_w = pl.BlockSpec((2, 1, gw, gw), lambda b, j: (0, j, 0, 0))
    return pl.pallas_call(
        functools.partial(_rglru_kernel, tc=tc, t=t),
        grid=(bsz, nb),
        in_specs=[
            pl.BlockSpec((1, t, d), lambda b, j: (b, 0, 0)),
            pl.BlockSpec((1, tc, d), lambda b, j: (b, 0, 0)),
            pl.BlockSpec((d, gw), lambda b, j: (0, j)),
            pl.BlockSpec((d, gw), lambda b, j: (0, nb + j)),
            vec(conv_w.shape[0]), vec(1),
            gate_w, gate_w, vec(2), vec(2), vec(2),
        ],
        out_specs=pl.BlockSpec((1, t, gw), lambda b, j: (b, 0, j)),
        out_shape=jax.ShapeDtypeStruct((bsz, t, e), BF16),
        scratch_shapes=[
            pltpu.VMEM((t + 3 * GRID_W, gw), F32),
            seq_f32,
            pltpu.VMEM((t, gw), BF16),
            seq_f32, seq_f32, seq_f32, seq_f32,
        ],
        compiler_params=pltpu.CompilerParams(
            dimension_semantics=("arbitrary", "arbitrary"), vmem_limit_bytes=VMEM_LIMIT),
        name="rglru",
    )(hx, hc, w_in, w_in, conv_w, conv_b.reshape(1, e), w_a, w_x, b_a, b_x, lam)


def kernel(x, c, ctx, c_ctx, ada_w, ada_b, norm_g, final_norm_g, hg_w_in, hg_lower_bounds,
           hg_norm_g, hg_w_out, rg_w_in, rg_conv_w, rg_conv_b, rg_w_a, rg_b_a, rg_w_x, rg_b_x,
           rg_lambda, rg_w_out):
    bsz, t, d = x.shape
    assert ada_w.shape[0] == 2 and bsz < MOD_ROWS and t % GRID_W == 0
    cond = jnp.concatenate(
        [c, c_ctx[None, :], jnp.zeros((MOD_ROWS - bsz - 1, d), F32)], axis=0)
    mod = _ada(cond, ada_w, ada_b)

    hx, hc = _prenorm(x, ctx, mod[0], norm_g[0])
    w_hg = hg_w_in[0].astype(BF16)
    ogx, ogc, half_decay = _hgrn(hx, hc, w_hg, hg_lower_bounds, hg_norm_g[0], layer=0)
    ogx, ogc = lax.cond(
        jnp.all(half_decay < HG_MAX_HALF_DECAY),
        lambda: (ogx, ogc),
        lambda: tuple(_hgrn_exact(hx, hc, w_hg, hg_lower_bounds, hg_norm_g[0], layer=0)))
    x1, hx1, hc1 = _outproj_mid(ogx, ogc, x, ctx, hg_w_out[0].astype(BF16), mod[0], mod[1], norm_g[1])
    yg = _rglru(hx1, hc1, rg_w_in[0].astype(BF16), rg_conv_w[0], rg_conv_b[0],
                (0.5 * rg_w_a[0]).astype(BF16), rg_b_a[0], (0.5 * rg_w_x[0]).astype(BF16), rg_b_x[0],
                rg_lambda[0])
    return _outproj_final(yg, x1, rg_w_out[0].astype(BF16), mod[1], final_norm_g)
```

```python
import functools

import jax
import jax.numpy as jnp
from jax import lax
from jax.experimental import pallas as pl
from jax.experimental.pallas import tpu as pltpu

F32 = jnp.float32
BF16 = jnp.bfloat16

EPS = 1e-6
F32_TINY = 1.1754944e-38
GRID_W = 64
HG_KEY_DIM = 128
HG_CHUNK = 64
HG_MAX_HALF_DECAY = 80.0
RG_BLOCK = 256
RG_C = 8.0
MOD_ROWS = 16
ROW_BLOCK = 256
GATE_ROWS = 64
HG_BLOCK = 512
VMEM_LIMIT = 56 * 1024 * 1024

_NT = (((1,), (1,)), ((), ()))
_TN = (((0,), (0,)), ((), ()))


def _half_tanh(z):
    return jnp.tanh(0.5 * z)


def _silu(z):
    hz = 0.5 * z
    return hz + hz * jnp.tanh(hz)


def _dot(a, b):
    return jnp.dot(a, b, preferred_element_type=F32)


def _ada_kernel(c_ref, w_ref, b_ref, o_ref):
    s = _silu(c_ref[...])
    o_ref[0] = jnp.dot(s, w_ref[0], preferred_element_type=F32,
                       precision=lax.Precision.HIGHEST) + b_ref[0]


def _ada(cond, ada_w, ada_b):
    depth, d, n3 = ada_w.shape
    bn = d
    return pl.pallas_call(
        _ada_kernel,
        grid=(depth, n3 // bn),
        in_specs=[
            pl.BlockSpec((MOD_ROWS, d), lambda i, n: (0, 0)),
            pl.BlockSpec((1, d, bn), lambda i, n: (i, 0, n)),
            pl.BlockSpec((1, 1, bn), lambda i, n: (i, 0, n)),
        ],
        out_specs=pl.BlockSpec((1, MOD_ROWS, bn), lambda i, n: (i, 0, n)),
        out_shape=jax.ShapeDtypeStruct((depth, MOD_ROWS, n3), F32),
        name="ada",
    )(cond, ada_w, ada_b.reshape(depth, 1, n3))


def _norm_mod(xv, g, shift, scale):
    ms = jnp.mean(xv * xv, axis=-1, keepdims=True)
    return xv * lax.rsqrt(ms + EPS) * g * (1.0 + scale) + shift


def _prenorm_kernel(x_ref, ctx_ref, mod_ref, g_ref, hx_ref, hc_ref, *, d, ctx_row):
    b = pl.program_id(0)
    g = g_ref[...]
    row = mod_ref[pl.ds(b, 1), :]
    hx_ref[0] = _norm_mod(x_ref[0], g, row[:, :d], row[:, d:2 * d]).astype(BF16)

    @pl.when(pl.program_id(1) == 0)
    def _():
        rowc = mod_ref[ctx_row:ctx_row + 1, :]
        hc_ref[0] = _norm_mod(ctx_ref[0], g, rowc[:, :d], rowc[:, d:2 * d]).astype(BF16)


def _prenorm(x, ctx, mod, g):
    bsz, t, d = x.shape
    tc = ctx.shape[1]
    rb = 512
    return pl.pallas_call(
        functools.partial(_prenorm_kernel, d=d, ctx_row=bsz),
        grid=(bsz, t // rb),
        in_specs=[
            pl.BlockSpec((1, rb, d), lambda b, r: (b, r, 0)),
            pl.BlockSpec((1, tc, d), lambda b, r: (b, 0, 0)),
            pl.BlockSpec((MOD_ROWS, 3 * d), lambda b, r: (0, 0)),
            pl.BlockSpec((1, d), lambda b, r: (0, 0)),
        ],
        out_specs=[
            pl.BlockSpec((1, rb, d), lambda b, r: (b, r, 0)),
            pl.BlockSpec((1, tc, d), lambda b, r: (b, 0, 0)),
        ],
        out_shape=[jax.ShapeDtypeStruct((bsz, t, d), BF16),
                   jax.ShapeDtypeStruct((bsz, tc, d), BF16)],
        name="prenorm",
    )(x, ctx, mod, g.reshape(1, d))


def _hgrn_kernel(hx_ref, hc_ref, wq_ref, wv_ref, wf_ref, wb_ref, wg_ref, lbp_ref, ng_ref, tri_ref,
                 ox_ref, oc_ref, decay_ref,
                 qa_b, va_b, lffa_b, lfba_b, kfa_b, kba_b, qb_b, vb_b, lffb_b, lfbb_b, kfb_b, kbb_b,
                 qtf_s, qtb_s, sg_s, o_s, oif_s, oib_s, ut_s, em_s,
                 *, layer, tc, t, heads):
    n_rows = tc + t
    c = HG_CHUNK
    kd = HG_KEY_DIM
    qt_s = (qtf_s, qtb_s)
    oi_s = (oif_s, oib_s)

    lbp = lbp_ref[...]
    ex = jnp.exp(lbp - jnp.max(lbp, axis=0, keepdims=True))
    lb = jnp.sum(ex[:layer + 1], axis=0, keepdims=True) / jnp.sum(ex, axis=0, keepdims=True)
    f_mid = 0.5 * (1.0 + lb)
    f_half = 0.5 * (1.0 - lb)

    rid = lax.broadcasted_iota(jnp.int32, (c, c), 0)
    cid = lax.broadcasted_iota(jnp.int32, (c, c), 1)
    masks = (rid >= cid, rid <= cid)

    def split3(xv):
        hi = xv.astype(BF16)
        r1 = xv - hi.astype(F32)
        mid = r1.astype(BF16)
        lo = (r1 - mid.astype(F32)).astype(BF16)
        return jnp.concatenate([hi, mid, lo], axis=0)

    def proj_steps(hrows, buf, r0):
        nr = hrows.shape[0]
        q_b, v_b, lf_b, k_b = buf

        def step_q():
            q_b[0:nr, :] = _silu(_dot(hrows, wq_ref[...]))

        def step_v():
            v_b[0:nr, :] = _dot(hrows, wv_ref[...]).astype(BF16)

        def step_f(d):
            f = f_mid + f_half * _half_tanh(_dot(hrows, (wf_ref, wb_ref)[d][...]))
            lf_b[d][0:nr, :] = jnp.log(f)
            k_b[d][0:nr, :] = 1.0 - f

        def step_g():
            sg_s[pl.ds(r0, nr), :] = _silu(_dot(hrows, wg_ref[...])).astype(BF16)

        return [step_q, step_v, functools.partial(step_f, 0), functools.partial(step_f, 1), step_g]

    def chunk_steps(buf, nr, r0, c0):
        q_b, v_b, lf_b, k_b = buf
        crs = [slice(cc * c, (cc + 1) * c) for cc in range(nr // c)]
        vals = {}

        def step_cumsum():
            vals["b"] = [[_dot(tri_ref[d], split3(lf_b[d][cr, :])) for d in range(2)] for cr in crs]

        def step_scale():
            qts, kts = [], []
            for cc, cr in enumerate(crs):
                qv = q_b[cr, :]
                qt, kt = [], []
                for d in range(2):
                    bsum = vals["b"][cc][d]
                    m = 0.5 * (bsum[c - 1:c, :] if d == 0 else bsum[0:1, :])
                    half_decay[0] = jnp.maximum(half_decay[0], jnp.abs(m))
                    em_s[d, pl.ds(c0 + cc, 1), :] = jnp.exp(m)
                    qt.append((qv * jnp.exp(bsum - m)).astype(BF16))
                    kt.append((k_b[d][cr, :] * jnp.exp(m - bsum)).astype(BF16))
                    qt_s[d][pl.ds(r0 + cc * c, c), :] = qt[d]
                qts.append(qt)
                kts.append(kt)
            vals["qt"], vals["kt"] = qts, kts

        def step_scores():
            scs = []
            for cc in range(len(crs)):
                for h in range(heads):
                    cs = slice(h * kd, (h + 1) * kd)
                    qt, kt = vals["qt"][cc], vals["kt"][cc]
                    sf = lax.dot_general(qt[0][:, cs], kt[0][:, cs], _NT, preferred_element_type=F32)
                    sb = lax.dot_general(qt[1][:, cs], kt[1][:, cs], _NT, preferred_element_type=F32)
                    scs.append((jnp.where(masks[0], sf, 0.0) + jnp.where(masks[1], sb, 0.0)).astype(BF16))
            vals["sc"] = scs

        def step_out():
            for cc, cr in enumerate(crs):
                for h in range(heads):
                    cs = slice(h * kd, (h + 1) * kd)
                    vv = v_b[cr, cs]
                    kt = vals["kt"][cc]
                    o_s[pl.ds(r0 + cc * c, c), cs] = _dot(vals["sc"][cc * heads + h], vv)
                    kk = jnp.concatenate([kt[0][:, cs], kt[1][:, cs]], axis=1)
                    ut_s[c0 + cc, h] = lax.dot_general(vv, kk, _TN, preferred_element_type=F32)

        return [step_cumsum, step_scale, step_scores, step_out]

    half_decay = [jnp.zeros((1, heads * kd), F32)]

    bufs = ((qa_b, va_b, (lffa_b, lfba_b), (kfa_b, kba_b)),
            (qb_b, vb_b, (lffb_b, lfbb_b), (kfb_b, kbb_b)))
    blocks = [(hc_ref[0], 0, tc)]
    blocks += [(hx_ref[0, i * HG_BLOCK:(i + 1) * HG_BLOCK, :], tc + i * HG_BLOCK, HG_BLOCK)
               for i in range(t // HG_BLOCK)]
    units = [(d, h) for d in range(2) for h in range(heads)]
    st = {u: jnp.zeros((kd, kd), F32) for u in units}

    def scan_step(d, ci):
        rows = slice(ci * c, (ci + 1) * c)
        for h in range(heads):
            cs = slice(h * kd, (h + 1) * kd)
            em = em_s[d, ci:ci + 1, cs]
            sm = st[(d, h)] * em
            st[(d, h)] = em * (sm + ut_s[ci, h][:, d * kd:(d + 1) * kd])
            oi_s[d][rows, cs] = lax.dot_general(
                qt_s[d][rows, cs], sm.astype(BF16), _NT, preferred_element_type=F32)

    prev, fwd_now, fwd_next = [], [], []
    for n, (hrows, r0, nr) in enumerate(blocks):
        proj = proj_steps(hrows, bufs[n % 2], r0)
        share = -(-len(fwd_now) // len(proj))
        for k, step in enumerate(proj):
            step()
            if k < len(prev):
                prev[k]()
            for f in fwd_now[k * share:(k + 1) * share]:
                f()
        fwd_now = fwd_next
        fwd_next = [functools.partial(scan_step, 0, ci) for ci in range(r0 // c, (r0 + nr) // c)]
        prev = chunk_steps(bufs[n % 2], nr, r0, r0 // c)
    for f in fwd_now:
        f()
    for step in prev:
        step()
    for f in fwd_next:
        f()
    decay_ref[0, 0] = half_decay[0]

    ng = ng_ref[...]

    def readout(r0):
        rows = slice(r0, r0 + ROW_BLOCK)
        o = o_s[rows, :] + oif_s[rows, :] + oib_s[rows, :]
        parts = []
        for h in range(heads):
            cs = slice(h * kd, (h + 1) * kd)
            oh = o[:, cs]
            ms = jnp.mean(oh * oh, axis=-1, keepdims=True)
            parts.append(oh * lax.rsqrt(ms + EPS) * ng[:, cs])
        y = (jnp.concatenate(parts, axis=-1) * sg_s[rows, :].astype(F32)).astype(BF16)
        if r0 < tc:
            oc_ref[0, r0:r0 + ROW_BLOCK, :] = y
        else:
            ox_ref[0, r0 - tc:r0 - tc + ROW_BLOCK, :] = y

    n_ctx = tc // c
    n_all = n_rows // c
    cpb = ROW_BLOCK // c
    order = list(range(n_ctx - 1, -1, -1)) + list(range(n_all - 1, n_ctx - 1, -1))

    @pl.when(pl.program_id(0) >= 0)
    def _():
        for ci in order:
            scan_step(1, ci)
            if ci % cpb == 0:
                readout(ci * c)


def _hgrn(hx, hc, w_in, lower_bounds, norm_g, layer):
    bsz, t, d = hx.shape
    tc = hc.shape[1]
    e = w_in.shape[1] // 5
    heads = 2
    gw = heads * HG_KEY_DIM
    ng_blocks = e // gw
    n_rows = tc + t
    c = HG_CHUNK

    low = jnp.tril(jnp.ones((c, c), F32))
    tri = jnp.stack([jnp.tile(low, (1, 3)), jnp.tile(low.T, (1, 3))]).astype(BF16)

    def wspec(sec):
        return pl.BlockSpec((d, gw), lambda b, j, sec=sec: (0, sec * ng_blocks + j))

    seq_f32 = pltpu.VMEM((n_rows, gw), F32)
    seq_bf16 = pltpu.VMEM((n_rows, gw), BF16)
    blk_f32 = pltpu.VMEM((HG_BLOCK, gw), F32)
    blk_bf16 = pltpu.VMEM((HG_BLOCK, gw), BF16)
    n_chunks = n_rows // c
    return pl.pallas_call(
        functools.partial(_hgrn_kernel, layer=layer, tc=tc, t=t, heads=heads),
        grid=(bsz, ng_blocks),
        in_specs=[
            pl.BlockSpec((1, t, d), lambda b, j: (b, 0, 0)),
            pl.BlockSpec((1, tc, d), lambda b, j: (b, 0, 0)),
            wspec(0), wspec(1), wspec(2), wspec(3), wspec(4),
            pl.BlockSpec((lower_bounds.shape[0], gw), lambda b, j: (0, j)),
            pl.BlockSpec((1, gw), lambda b, j: (0, j)),
            pl.BlockSpec((2, c, 3 * c), lambda b, j: (0, 0, 0)),
        ],
        out_specs=[
            pl.BlockSpec((1, t, gw), lambda b, j: (b, 0, j)),
            pl.BlockSpec((1, tc, gw), lambda b, j: (b, 0, j)),
            pl.BlockSpec((1, 1, 1, gw), lambda b, j: (b, j, 0, 0)),
        ],
        out_shape=[jax.ShapeDtypeStruct((bsz, t, e), BF16),
                   jax.ShapeDtypeStruct((bsz, tc, e), BF16),
                   jax.ShapeDtypeStruct((bsz, ng_blocks, 1, gw), F32)],
        scratch_shapes=[
            blk_f32, blk_bf16, blk_f32, blk_f32, blk_f32, blk_f32,
            blk_f32, blk_bf16, blk_f32, blk_f32, blk_f32, blk_f32,
            seq_bf16, seq_bf16,
            seq_bf16,
            seq_f32, seq_f32, seq_f32,
            pltpu.VMEM((n_chunks, heads, HG_KEY_DIM, 2 * HG_KEY_DIM), F32),
            pltpu.VMEM((2, n_chunks, gw), F32),
        ],
        compiler_params=pltpu.CompilerParams(
            dimension_semantics=("arbitrary", "arbitrary"), vmem_limit_bytes=VMEM_LIMIT),
        name="hgrn2",
    )(hx, hc, w_in, w_in, w_in, w_in, w_in, lower_bounds, norm_g.reshape(1, e), tri)


def _hgrn_exact_kernel(hx_ref, hc_ref, wq_ref, wv_ref, wf_ref, wb_ref, wg_ref, lbp_ref, ng_ref,
                       ox_ref, oc_ref, q_s, v_s, ff_s, fb_s, sg_s, o_s, st_s,
                       *, layer, tc, t, heads):
    n_rows = tc + t
    kd = HG_KEY_DIM
    f_refs = (ff_s, fb_s)

    lbp = lbp_ref[...]
    ex = jnp.exp(lbp - jnp.max(lbp, axis=0, keepdims=True))
    lb = jnp.sum(ex[:layer + 1], axis=0, keepdims=True) / jnp.sum(ex, axis=0, keepdims=True)
    f_mid = 0.5 * (1.0 + lb)
    f_half = 0.5 * (1.0 - lb)

    def project(hrows, r0):
        rows = pl.ds(r0, ROW_BLOCK)
        q_s[rows, :] = _silu(_dot(hrows, wq_ref[...]))
        v_s[rows, :] = _dot(hrows, wv_ref[...])
        ff_s[rows, :] = f_mid + f_half * _half_tanh(_dot(hrows, wf_ref[...]))
        fb_s[rows, :] = f_mid + f_half * _half_tanh(_dot(hrows, wb_ref[...]))
        sg_s[rows, :] = _silu(_dot(hrows, wg_ref[...]))

    for i in range(tc // ROW_BLOCK):
        project(hc_ref[0, i * ROW_BLOCK:(i + 1) * ROW_BLOCK, :], i * ROW_BLOCK)

    def proj_body(i, carry):
        r0 = pl.multiple_of(i * ROW_BLOCK, ROW_BLOCK)
        project(hx_ref[0, pl.ds(r0, ROW_BLOCK), :], tc + r0)
        return carry

    lax.fori_loop(0, t // ROW_BLOCK, proj_body, 0)

    o_s[...] = jnp.zeros_like(o_s)
    st_s[...] = jnp.zeros_like(st_s)
    rid = lax.broadcasted_iota(jnp.int32, (8, kd), 0)
    n_ct = tc // 8
    n_tiles = n_rows // 8

    def tile_body(i, carry):
        tiles = (i, jnp.where(i < n_ct, n_ct - 1 - i, n_tiles + n_ct - 1 - i))
        for d in range(2):
            rows = pl.ds(pl.multiple_of(tiles[d] * 8, 8), 8)
            for h in range(heads):
                cs = slice(h * kd, (h + 1) * kd)
                f8, v8, q8 = f_refs[d][rows, cs], v_s[rows, cs], q_s[rows, cs]
                k8 = 1.0 - f8
                s = st_s[d, h]
                acc = jnp.zeros((8, kd), F32)
                for r in (range(8) if d == 0 else range(7, -1, -1)):
                    sel = rid == r
                    outer = lax.dot_general(jnp.where(sel, v8, 0.0), k8, _TN, preferred_element_type=F32)
                    s = s * f8[r:r + 1, :] + outer
                    acc = acc + lax.dot_general(jnp.where(sel, q8, 0.0), s, _NT,
                                                preferred_element_type=F32)
                st_s[d, h] = s
                o_s[rows, cs] = o_s[rows, cs] + acc
        return carry

    lax.fori_loop(0, n_tiles, tile_body, 0)

    ng = ng_ref[...]

    def readout(r0):
        rows = pl.ds(r0, ROW_BLOCK)
        o = o_s[rows, :]
        parts = []
        for h in range(heads):
            cs = slice(h * kd, (h + 1) * kd)
            oh = o[:, cs]
            ms = jnp.mean(oh * oh, axis=-1, keepdims=True)
            parts.append(oh * lax.rsqrt(ms + EPS) * ng[:, cs])
        return (jnp.concatenate(parts, axis=-1) * sg_s[rows, :]).astype(BF16)

    for i in range(tc // ROW_BLOCK):
        oc_ref[0, i * ROW_BLOCK:(i + 1) * ROW_BLOCK, :] = readout(i * ROW_BLOCK)

    def out_body(i, carry):
        r0 = pl.multiple_of(i * ROW_BLOCK, ROW_BLOCK)
        ox_ref[0, pl.ds(r0, ROW_BLOCK), :] = readout(tc + r0)
        return carry

    lax.fori_loop(0, t // ROW_BLOCK, out_body, 0)


def _hgrn_exact(hx, hc, w_in, lower_bounds, norm_g, layer):
    bsz, t, d = hx.shape
    tc = hc.shape[1]
    e = w_in.shape[1] // 5
    heads = 2
    gw = heads * HG_KEY_DIM
    ng_blocks = e // gw
    seq_f32 = pltpu.VMEM((tc + t, gw), F32)

    def wspec(sec):
        return pl.BlockSpec((d, gw), lambda b, j, sec=sec: (0, sec * ng_blocks + j))

    return pl.pallas_call(
        functools.partial(_hgrn_exact_kernel, layer=layer, tc=tc, t=t, heads=heads),
        grid=(bsz, ng_blocks),
        in_specs=[
            pl.BlockSpec((1, t, d), lambda b, j: (b, 0, 0)),
            pl.BlockSpec((1, tc, d), lambda b, j: (b, 0, 0)),
            wspec(0), wspec(1), wspec(2), wspec(3), wspec(4),
            pl.BlockSpec((lower_bounds.shape[0], gw), lambda b, j: (0, j)),
            pl.BlockSpec((1, gw), lambda b, j: (0, j)),
        ],
        out_specs=[
            pl.BlockSpec((1, t, gw), lambda b, j: (b, 0, j)),
            pl.BlockSpec((1, tc, gw), lambda b, j: (b, 0, j)),
        ],
        out_shape=[jax.ShapeDtypeStruct((bsz, t, e), BF16),
                   jax.ShapeDtypeStruct((bsz, tc, e), BF16)],
        scratch_shapes=[
            seq_f32, seq_f32, seq_f32, seq_f32, seq_f32,
            seq_f32,
            pltpu.VMEM((2, heads, HG_KEY_DIM, HG_KEY_DIM), F32),
        ],
        compiler_params=pltpu.CompilerParams(
            dimension_semantics=("arbitrary", "arbitrary"), vmem_limit_bytes=VMEM_LIMIT),
        name="hgrn2_exact",
    )(hx, hc, w_in, w_in, w_in, w_in, w_in, lower_bounds, norm_g.reshape(1, e))


def _outproj_mid_kernel(ogx_ref, ogc_ref, x_ref, ctx_ref, w_ref, mod0_ref, mod1_ref, g1_ref,
                        x1_ref, hx_ref, hc_ref, *, d, ctx_row):
    b = pl.program_id(0)
    g1 = g1_ref[...]
    row0 = mod0_ref[pl.ds(b, 1), :]
    row1 = mod1_ref[pl.ds(b, 1), :]
    x1 = x_ref[0] + row0[:, 2 * d:] * _dot(ogx_ref[0], w_ref[...])
    x1_ref[0] = x1
    hx_ref[0] = _norm_mod(x1, g1, row1[:, :d], row1[:, d:2 * d]).astype(BF16)

    @pl.when(pl.program_id(1) == 0)
    def _():
        c0 = mod0_ref[ctx_row:ctx_row + 1, :]
        c1 = mod1_ref[ctx_row:ctx_row + 1, :]
        ctx1 = ctx_ref[0] + c0[:, 2 * d:] * _dot(ogc_ref[0], w_ref[...])
        hc_ref[0] = _norm_mod(ctx1, g1, c1[:, :d], c1[:, d:2 * d]).astype(BF16)


def _outproj_mid(ogx, ogc, x, ctx, w_out, mod0, mod1, g1):
    bsz, t, d = x.shape
    tc = ctx.shape[1]
    e = w_out.shape[0]
    rb = 512
    return pl.pallas_call(
        functools.partial(_outproj_mid_kernel, d=d, ctx_row=bsz),
        grid=(bsz, t // rb),
        in_specs=[
            pl.BlockSpec((1, rb, e), lambda b, r: (b, r, 0)),
            pl.BlockSpec((1, tc, e), lambda b, r: (b, 0, 0)),
            pl.BlockSpec((1, rb, d), lambda b, r: (b, r, 0)),
            pl.BlockSpec((1, tc, d), lambda b, r: (b, 0, 0)),
            pl.BlockSpec((e, d), lambda b, r: (0, 0)),
            pl.BlockSpec((MOD_ROWS, 3 * d), lambda b, r: (0, 0)),
            pl.BlockSpec((MOD_ROWS, 3 * d), lambda b, r: (0, 0)),
            pl.BlockSpec((1, d), lambda b, r: (0, 0)),
        ],
        out_specs=[
            pl.BlockSpec((1, rb, d), lambda b, r: (b, r, 0)),
            pl.BlockSpec((1, rb, d), lambda b, r: (b, r, 0)),
            pl.BlockSpec((1, tc, d), lambda b, r: (b, 0, 0)),
        ],
        out_shape=[jax.ShapeDtypeStruct((bsz, t, d), F32),
                   jax.ShapeDtypeStruct((bsz, t, d), BF16),
                   jax.ShapeDtypeStruct((bsz, tc, d), BF16)],
        compiler_params=pltpu.CompilerParams(
            dimension_semantics=("arbitrary", "arbitrary"), vmem_limit_bytes=VMEM_LIMIT),
        name="outproj0",
    )(ogx, ogc, x, ctx, w_out, mod0, mod1, g1.reshape(1, d))


def _outproj_final_kernel(yg_ref, x_ref, w_ref, mod_ref, g_ref, o_ref, *, d):
    b = pl.program_id(0)
    row = mod_ref[pl.ds(b, 1), :]
    x2 = x_ref[0] + row[:, 2 * d:] * _dot(yg_ref[0], w_ref[...])
    ms = jnp.mean(x2 * x2, axis=-1, keepdims=True)
    o_ref[0] = x2 * lax.rsqrt(ms + EPS) * g_ref[...]


def _outproj_final(yg, x1, w_out, mod, g):
    bsz, t, d = x1.shape
    e = w_out.shape[0]
    rb = 512
    return pl.pallas_call(
        functools.partial(_outproj_final_kernel, d=d),
        grid=(bsz, t // rb),
        in_specs=[
            pl.BlockSpec((1, rb, e), lambda b, r: (b, r, 0)),
            pl.BlockSpec((1, rb, d), lambda b, r: (b, r, 0)),
            pl.BlockSpec((e, d), lambda b, r: (0, 0)),
            pl.BlockSpec((MOD_ROWS, 3 * d), lambda b, r: (0, 0)),
            pl.BlockSpec((1, d), lambda b, r: (0, 0)),
        ],
        out_specs=pl.BlockSpec((1, rb, d), lambda b, r: (b, r, 0)),
        out_shape=jax.ShapeDtypeStruct((bsz, t, d), F32),
        compiler_params=pltpu.CompilerParams(
            dimension_semantics=("arbitrary", "arbitrary"), vmem_limit_bytes=VMEM_LIMIT),
        name="outproj1",
    )(yg, x1, w_out, mod, g.reshape(1, d))


def _shift_rows(xv, k):
    n = xv.shape[0]
    rid = lax.broadcasted_iota(jnp.int32, xv.shape, 0)
    rolled = pltpu.roll(xv, k % n, 0)
    valid = (rid >= k) if k > 0 else (rid < n + k)
    return jnp.where(valid, rolled, 0.0)


def _tile_scan(a, u, reverse):
    n = a.shape[0]
    rid = lax.broadcasted_iota(jnp.int32, a.shape, 0)
    s = 1
    while s < n:
        valid = (rid < n - s) if reverse else (rid >= s)
        sh = (n - s) if reverse else s
        a_sh = jnp.where(valid, pltpu.roll(a, sh, 0), 1.0)
        u_sh = jnp.where(valid, pltpu.roll(u, sh, 0), 0.0)
        u = u + a * u_sh
        a = a * a_sh
        s *= 2
    return a, u


def _rglru_kernel(hx_ref, hc_ref, wx_ref, wg_ref, cw_ref, cb_ref, wa_ref, wi_ref, ba_ref, bi_ref,
                  lam_ref, y_ref,
                  xp_s, xc_s, sg_s, af_s, uf_s, ab_s, ub_s,
                  *, tc, t):
    gw = RG_BLOCK
    gr = GRID_W
    n_tr = t // gr
    a_refs = (af_s, ab_s)
    u_refs = (uf_s, ub_s)
    cw = cw_ref[...]
    cb = cb_ref[...]

    def conv_taps(xm2, xm1, x0, xp1):
        return cw[0:1] * xm2 + cw[1:2] * xm1 + cw[2:3] * x0 + cw[3:4] * xp1 + cb

    xcx = _dot(hc_ref[0], wx_ref[...])
    xc_s[0:tc, :] = conv_taps(_shift_rows(xcx, 2), _shift_rows(xcx, 1), xcx, _shift_rows(xcx, -1))

    pad = 2 * gr
    nb = t // ROW_BLOCK
    hsp = (0.5 * RG_C) * jax.nn.softplus(-lam_ref[...])
    hba = 0.5 * ba_ref[...]
    hbi = 0.5 * bi_ref[...]

    def lat_r0(i):
        return i * ROW_BLOCK if isinstance(i, int) else pl.multiple_of(i * ROW_BLOCK, ROW_BLOCK)

    def proj_x(i):
        r0 = lat_r0(i)
        xp_s[pl.ds(pad + r0, ROW_BLOCK), :] = _dot(hx_ref[0, pl.ds(r0, ROW_BLOCK), :], wx_ref[...])

    def proj_g(i):
        r0 = lat_r0(i)
        sg_s[pl.ds(r0, ROW_BLOCK), :] = _silu(
            _dot(hx_ref[0, pl.ds(r0, ROW_BLOCK), :], wg_ref[...])).astype(BF16)

    def conv_block(i):
        for k in range(ROW_BLOCK // gr):
            r0 = lat_r0(i) + k * gr
            xc_s[pl.ds(tc + r0, gr), :] = conv_taps(
                xp_s[pl.ds(r0, gr), :], xp_s[pl.ds(r0 + gr, gr), :],
                xp_s[pl.ds(r0 + 2 * gr, gr), :], xp_s[pl.ds(r0 + 3 * gr, gr), :])

    tot = {"H": jnp.zeros((gr, gw), F32), "A": jnp.ones((gr, gw), F32),
           "E": jnp.zeros((gr, gw), F32), "G": jnp.ones((gr, gw), F32)}

    def gates(r0, d, latent):
        for part in range(ROW_BLOCK // GATE_ROWS):
            gates_part(r0 + part * GATE_ROWS, d, latent)

    def gates_part(r0, d, latent):
        rows = pl.ds(r0, GATE_ROWS)
        xc = xc_s[rows, :]
        xcb = xc.astype(BF16)
        hxc = 0.5 * xc
        tr = jnp.tanh(_dot(xcb, wa_ref[d, 0]) + hba[d:d + 1, :])
        ti = jnp.tanh(_dot(xcb, wi_ref[d, 0]) + hbi[d:d + 1, :])
        nla = hsp[d:d + 1, :] + hsp[d:d + 1, :] * tr
        a = jnp.exp(-nla)
        a_refs[d][rows, :] = a
        z = jnp.tanh(nla) * (a * a + 1.0)
        u = (z * lax.rsqrt(jnp.maximum(z, F32_TINY))) * (hxc + hxc * ti)
        u_refs[d][rows, :] = u
        if not latent:
            return
        for k in range(GATE_ROWS // gr):
            ak, uk = a[k * gr:(k + 1) * gr, :], u[k * gr:(k + 1) * gr, :]
            if d == 0:
                tot["H"] = ak * tot["H"] + uk
                tot["A"] = ak * tot["A"]
            else:
                tot["E"] = tot["E"] + tot["G"] * uk
                tot["G"] = tot["G"] * ak

    proj_x(nb - 1)
    for i in range(tc // ROW_BLOCK):
        gates(i * ROW_BLOCK, 0, False)
    proj_g(nb - 1)
    for i in range(tc // ROW_BLOCK):
        gates(i * ROW_BLOCK, 1, False)
    def ctx_final(d):
        n_ct = tc // 8
        tiles = [_tile_scan(a_refs[d][i * 8:(i + 1) * 8, :], u_refs[d][i * 8:(i + 1) * 8, :], d == 1)
                 for i in range(n_ct)]
        h = jnp.zeros((1, gw), F32)
        for i in (range(n_ct) if d == 0 else range(n_ct - 1, -1, -1)):
            a, u = tiles[i]
            h = a[7:8] * h + u[7:8] if d == 0 else a[0:1] * h + u[0:1]
        return h

    hf = ctx_final(0)
    hb = ctx_final(1)
    xp_s[0:gr, :] = _shift_rows(xp_s[pad + (n_tr - 2) * gr:pad + (n_tr - 1) * gr, :], 1)
    xp_s[gr:pad, :] = _shift_rows(xp_s[pad + (n_tr - 1) * gr:pad + n_tr * gr, :], 1)
    proj_x(0)
    proj_g(0)
    xp_s[pad + n_tr * gr:pad + (n_tr + 1) * gr, :] = _shift_rows(xp_s[pad:pad + gr, :], -1)

    def pipe_body(k):
        proj_x(k + 1)
        conv_block(k)
        gates(tc + lat_r0(k), 0, True)
        proj_g(k + 1)
        gates(tc + lat_r0(k), 1, True)

    for k in range(nb - 2):
        pipe_body(k)
    for i in (nb - 2, nb - 1):
        conv_block(i)
        gates(tc + lat_r0(i), 0, True)
        gates(tc + lat_r0(i), 1, True)

    def column_carries(e_tot, p_tot, h0, reverse):
        rid = lax.broadcasted_iota(jnp.int32, (8, gw), 0)
        order = range(gr // 8 - 1, -1, -1) if reverse else range(gr // 8)
        carry, out = h0, [None] * (gr // 8)
        for i in order:
            a_t, u_t = _tile_scan(p_tot[i * 8:(i + 1) * 8, :], e_tot[i * 8:(i + 1) * 8, :], reverse)
            h = u_t + a_t * carry
            if reverse:
                out[i] = jnp.where(rid == 7, carry, pltpu.roll(h, 7, 0))
                carry = h[0:1]
            else:
                out[i] = jnp.where(rid == 0, carry, pltpu.roll(h, 1, 0))
                carry = h[7:8]
        return jnp.concatenate(out, axis=0)

    h = column_carries(tot["H"], tot["A"], hf, False)
    for r in range(n_tr):
        rows = slice(tc + r * gr, tc + (r + 1) * gr)
        h = af_s[rows, :] * h + uf_s[rows, :]
        uf_s[rows, :] = h
    h = column_carries(tot["E"], tot["G"], hb, True)
    for r in range(n_tr - 1, -1, -1):
        rows = slice(tc + r * gr, tc + (r + 1) * gr)
        h = ab_s[rows, :] * h + ub_s[rows, :]
        gate = sg_s[r * gr:(r + 1) * gr, :].astype(F32)
        y_ref[0, r * gr:(r + 1) * gr, :] = ((uf_s[rows, :] + h) * gate).astype(BF16)


def _rglru(hx, hc, w_in, conv_w, conv_b, w_a, b_a, w_x, b_x, lam):
    bsz, t, d = hx.shape
    tc = hc.shape[1]
    e = w_in.shape[1] // 2
    gw = RG_BLOCK
    nb = e // gw
    n_rows = tc + t
    seq_f32 = pltpu.VMEM((n_rows, gw), F32)
    vec = lambda rows: pl.BlockSpec((rows, gw), lambda b, j: (0, j))
    gate_w = pl.BlockSpec((2, 1, gw, gw), lambda b, j: (0, j, 0, 0))
    return pl.pallas_call(
        functools.partial(_rglru_kernel, tc=tc, t=t),
        grid=(bsz, nb),
        in_specs=[
            pl.BlockSpec((1, t, d), lambda b, j: (b, 0, 0)),
            pl.BlockSpec((1, tc, d), lambda b, j: (b, 0, 0)),
            pl.BlockSpec((d, gw), lambda b, j: (0, j)),
            pl.BlockSpec((d, gw), lambda b, j: (0, nb + j)),
            vec(conv_w.shape[0]), vec(1),
            gate_w, gate_w, vec(2), vec(2), vec(2),
        ],
        out_specs=pl.BlockSpec((1, t, gw), lambda b, j: (b, 0, j)),
        out_shape=jax.ShapeDtypeStruct((bsz, t, e), BF16),
        scratch_shapes=[
            pltpu.VMEM((t + 3 * GRID_W, gw), F32),
            seq_f32,
            pltpu.VMEM((t, gw), BF16),
            seq_f32, seq_f32, seq_f32, seq_f32,
        ],
        compiler_params=pltpu.CompilerParams(
            dimension_semantics=("arbitrary", "arbitrary"), vmem_limit_bytes=VMEM_LIMIT),
        name="rglru",
    )(hx, hc, w_in, w_in, conv_w, conv_b.reshape(1, e), w_a, w_x, b_a, b_x, lam)


def kernel(x, c, ctx, c_ctx, ada_w, ada_b, norm_g, final_norm_g, hg_w_in, hg_lower_bounds,
           hg_norm_g, hg_w_out, rg_w_in, rg_conv_w, rg_conv_b, rg_w_a, rg_b_a, rg_w_x, rg_b_x,
           rg_lambda, rg_w_out):
    bsz, t, d = x.shape
    assert ada_w.shape[0] == 2 and bsz < MOD_ROWS and t % GRID_W == 0
    cond = jnp.concatenate(
        [c, c_ctx[None, :], jnp.zeros((MOD_ROWS - bsz - 1, d), F32)], axis=0)
    mod = _ada(cond, ada_w, ada_b)

    hx, hc = _prenorm(x, ctx, mod[0], norm_g[0])
    w_hg = hg_w_in[0].astype(BF16)
    ogx, ogc, half_decay = _hgrn(hx, hc, w_hg, hg_lower_bounds, hg_norm_g[0], layer=0)
    ogx, ogc = lax.cond(
        jnp.all(half_decay < HG_MAX_HALF_DECAY),
        lambda: (ogx, ogc),
        lambda: tuple(_hgrn_exact(hx, hc, w_hg, hg_lower_bounds, hg_norm_g[0], layer=0)))
    x1, hx1, hc1 = _outproj_mid(ogx, ogc, x, ctx, hg_w_out[0].astype(BF16), mod[0], mod[1], norm_g[1])
    yg = _rglru(hx1, hc1, rg_w_in[0].astype(BF16), rg_conv_w[0], rg_conv_b[0],
                (0.5 * rg_w_a[0]).astype(BF16), rg_b_a[0], (0.5 * rg_w_x[0]).astype(BF16), rg_b_x[0],
                rg_lambda[0])
    return _outproj_final(yg, x1, rg_w_out[0].astype(BF16), mod[1], final_norm_g)
```

```python
import functools

import jax
import jax.numpy as jnp
from jax import lax
from jax.experimental import pallas as pl
from jax.experimental.pallas import tpu as pltpu

F32 = jnp.float32
BF16 = jnp.bfloat16

EPS = 1e-6
F32_TINY = 1.1754944e-38
GRID_W = 64
HG_KEY_DIM = 128
HG_CHUNK = 64
HG_MAX_HALF_DECAY = 80.0
RG_BLOCK = 256
RG_C = 8.0
MOD_ROWS = 16
ROW_BLOCK = 256
GATE_ROWS = 64
HG_BLOCK = 512
VMEM_LIMIT = 56 * 1024 * 1024

_NT = (((1,), (1,)), ((), ()))
_TN = (((0,), (0,)), ((), ()))


def _silu(z):
    return _silu_half(0.5 * z)


def _silu_half(hz):
    return hz + hz * jnp.tanh(hz)


def _dot(a, b):
    return jnp.dot(a, b, preferred_element_type=F32)


def _ada_kernel(c_ref, w_ref, b_ref, o_ref):
    s = _silu(c_ref[...])
    o_ref[0] = jnp.dot(s, w_ref[0], preferred_element_type=F32,
                       precision=lax.Precision.HIGHEST) + b_ref[0]


def _ada(cond, ada_w, ada_b):
    depth, d, n3 = ada_w.shape
    bn = d
    return pl.pallas_call(
        _ada_kernel,
        grid=(depth, n3 // bn),
        in_specs=[
            pl.BlockSpec((MOD_ROWS, d), lambda i, n: (0, 0)),
            pl.BlockSpec((1, d, bn), lambda i, n: (i, 0, n)),
            pl.BlockSpec((1, 1, bn), lambda i, n: (i, 0, n)),
        ],
        out_specs=pl.BlockSpec((1, MOD_ROWS, bn), lambda i, n: (i, 0, n)),
        out_shape=jax.ShapeDtypeStruct((depth, MOD_ROWS, n3), F32),
        name="ada",
    )(cond, ada_w, ada_b.reshape(depth, 1, n3))


def _norm_mod(xv, g, shift, scale):
    ms = jnp.mean(xv * xv, axis=-1, keepdims=True)
    return xv * lax.rsqrt(ms + EPS) * g * (1.0 + scale) + shift


def _prenorm_kernel(x_ref, ctx_ref, mod_ref, g_ref, hx_ref, hc_ref, *, d, ctx_row):
    b = pl.program_id(0)
    g = g_ref[...]
    row = mod_ref[pl.ds(b, 1), :]
    hx_ref[0] = _norm_mod(x_ref[0], g, row[:, :d], row[:, d:2 * d]).astype(BF16)

    @pl.when(pl.program_id(1) == 0)
    def _():
        rowc = mod_ref[ctx_row:ctx_row + 1, :]
        hc_ref[0] = _norm_mod(ctx_ref[0], g, rowc[:, :d], rowc[:, d:2 * d]).astype(BF16)


def _prenorm(x, ctx, mod, g):
    bsz, t, d = x.shape
    tc = ctx.shape[1]
    rb = 1024
    return pl.pallas_call(
        functools.partial(_prenorm_kernel, d=d, ctx_row=bsz),
        grid=(bsz, t // rb),
        in_specs=[
            pl.BlockSpec((1, rb, d), lambda b, r: (b, r, 0)),
            pl.BlockSpec((1, tc, d), lambda b, r: (b, 0, 0)),
            pl.BlockSpec((MOD_ROWS, 3 * d), lambda b, r: (0, 0)),
            pl.BlockSpec((1, d), lambda b, r: (0, 0)),
        ],
        out_specs=[
            pl.BlockSpec((1, rb, d), lambda b, r: (b, r, 0)),
            pl.BlockSpec((1, tc, d), lambda b, r: (b, 0, 0)),
        ],
        out_shape=[jax.ShapeDtypeStruct((bsz, t, d), BF16),
                   jax.ShapeDtypeStruct((bsz, tc, d), BF16)],
        name="prenorm",
    )(x, ctx, mod, g.reshape(1, d))


def _hgrn_kernel(hx_ref, hc_ref, wq_ref, wv_ref, wf_ref, wb_ref, wg_ref, lbp_ref, ng_ref, tri_ref,
                 ox_ref, oc_ref, decay_ref,
                 qa_b, va_b, lffa_b, lfba_b, kfa_b, kba_b, qb_b, vb_b, lffb_b, lfbb_b, kfb_b, kbb_b,
                 qtf_s, qtb_s, sg_s, o_s, oif_s, oib_s, ut_s, em_s,
                 *, layer, tc, t, heads):
    n_rows = tc + t
    c = HG_CHUNK
    kd = HG_KEY_DIM
    qt_s = (qtf_s, qtb_s)
    oi_s = (oif_s, oib_s)

    lbp = lbp_ref[...]
    ex = jnp.exp(lbp - jnp.max(lbp, axis=0, keepdims=True))
    lb = jnp.sum(ex[:layer + 1], axis=0, keepdims=True) / jnp.sum(ex, axis=0, keepdims=True)
    f_mid = 0.5 * (1.0 + lb)
    f_half = 0.5 * (1.0 - lb)

    rid = lax.broadcasted_iota(jnp.int32, (c, c), 0)
    cid = lax.broadcasted_iota(jnp.int32, (c, c), 1)
    masks = (rid >= cid, rid <= cid)

    def split3(xv):
        hi = xv.astype(BF16)
        r1 = xv - hi.astype(F32)
        mid = r1.astype(BF16)
        lo = (r1 - mid.astype(F32)).astype(BF16)
        return jnp.concatenate([hi, mid, lo], axis=0)

    def proj_steps(hrows, buf, r0):
        nr = hrows.shape[0]
        q_b, v_b, lf_b, k_b = buf

        def step_q():
            q_b[0:nr, :] = _silu_half(_dot(hrows, wq_ref[...]))

        def step_v():
            v_b[0:nr, :] = _dot(hrows, wv_ref[...]).astype(BF16)

        def step_f(d):
            f = f_mid + f_half * jnp.tanh(_dot(hrows, (wf_ref, wb_ref)[d][...]))
            lf_b[d][0:nr, :] = jnp.log(f)
            k_b[d][0:nr, :] = 1.0 - f

        def step_g():
            sg_s[pl.ds(r0, nr), :] = _silu_half(_dot(hrows, wg_ref[...])).astype(BF16)

        return [step_q, step_v, functools.partial(step_f, 0), functools.partial(step_f, 1), step_g]

    def chunk_steps(buf, nr, r0, c0):
        q_b, v_b, lf_b, k_b = buf
        crs = [slice(cc * c, (cc + 1) * c) for cc in range(nr // c)]
        vals = {}

        def step_cumsum():
            vals["b"] = [[_dot(tri_ref[d], split3(lf_b[d][cr, :])) for d in range(2)] for cr in crs]

        def step_scale():
            qts, kts = [], []
            for cc, cr in enumerate(crs):
                qv = q_b[cr, :]
                qt, kt = [], []
                for d in range(2):
                    bsum = vals["b"][cc][d]
                    m = 0.5 * (bsum[c - 1:c, :] if d == 0 else bsum[0:1, :])
                    half_decay[0] = jnp.maximum(half_decay[0], jnp.abs(m))
                    em_s[d, pl.ds(c0 + cc, 1), :] = jnp.exp(m)
                    qt.append((qv * jnp.exp(bsum - m)).astype(BF16))
                    kt.append((k_b[d][cr, :] * jnp.exp(m - bsum)).astype(BF16))
                    qt_s[d][pl.ds(r0 + cc * c, c), :] = qt[d]
                qts.append(qt)
                kts.append(kt)
            vals["qt"], vals["kt"] = qts, kts

        def step_scores():
            scs = []
            for cc in range(len(crs)):
                for h in range(heads):
                    cs = slice(h * kd, (h + 1) * kd)
                    qt, kt = vals["qt"][cc], vals["kt"][cc]
                    sf = lax.dot_general(qt[0][:, cs], kt[0][:, cs], _NT, preferred_element_type=F32)
                    sb = lax.dot_general(qt[1][:, cs], kt[1][:, cs], _NT, preferred_element_type=F32)
                    scs.append((jnp.where(masks[0], sf, 0.0) + jnp.where(masks[1], sb, 0.0)).astype(BF16))
            vals["sc"] = scs

        def step_out():
            for cc, cr in enumerate(crs):
                for h in range(heads):
                    cs = slice(h * kd, (h + 1) * kd)
                    vv = v_b[cr, cs]
                    kt = vals["kt"][cc]
                    o_s[pl.ds(r0 + cc * c, c), cs] = _dot(vals["sc"][cc * heads + h], vv)
                    kk = jnp.concatenate([kt[0][:, cs], kt[1][:, cs]], axis=1)
                    ut_s[c0 + cc, h] = lax.dot_general(vv, kk, _TN, preferred_element_type=F32)

        return [step_cumsum, step_scale, step_scores, step_out]

    half_decay = [jnp.zeros((1, heads * kd), F32)]

    bufs = ((qa_b, va_b, (lffa_b, lfba_b), (kfa_b, kba_b)),
            (qb_b, vb_b, (lffb_b, lfbb_b), (kfb_b, kbb_b)))
    blocks = [(hc_ref[0], 0, tc)]
    blocks += [(hx_ref[0, i * HG_BLOCK:(i + 1) * HG_BLOCK, :], tc + i * HG_BLOCK, HG_BLOCK)
               for i in range(t // HG_BLOCK)]
    units = [(d, h) for d in range(2) for h in range(heads)]
    st = {u: jnp.zeros((kd, kd), F32) for u in units}

    def scan_step(d, ci):
        rows = slice(ci * c, (ci + 1) * c)
        for h in range(heads):
            cs = slice(h * kd, (h + 1) * kd)
            em = em_s[d, ci:ci + 1, cs]
            sm = st[(d, h)] * em
            st[(d, h)] = em * (sm + ut_s[ci, h][:, d * kd:(d + 1) * kd])
            oi_s[d][rows, cs] = lax.dot_general(
                qt_s[d][rows, cs], sm.astype(BF16), _NT, preferred_element_type=F32)

    prev, fwd_now, fwd_next = [], [], []
    for n, (hrows, r0, nr) in enumerate(blocks):
        proj = proj_steps(hrows, bufs[n % 2], r0)
        share = -(-len(fwd_now) // len(proj))
        for k, step in enumerate(proj):
            step()
            if k < len(prev):
                prev[k]()
            for f in fwd_now[k * share:(k + 1) * share]:
                f()
        fwd_now = fwd_next
        fwd_next = [functools.partial(scan_step, 0, ci) for ci in range(r0 // c, (r0 + nr) // c)]
        prev = chunk_steps(bufs[n % 2], nr, r0, r0 // c)
    for f in fwd_now:
        f()
    for step in prev:
        step()
    for f in fwd_next:
        f()
    decay_ref[0, 0] = half_decay[0]

    ng = ng_ref[...]

    def readout(r0):
        rows = slice(r0, r0 + ROW_BLOCK)
        o = o_s[rows, :] + oif_s[rows, :] + oib_s[rows, :]
        parts = []
        for h in range(heads):
            cs = slice(h * kd, (h + 1) * kd)
            oh = o[:, cs]
            ms = jnp.mean(oh * oh, axis=-1, keepdims=True)
            parts.append(oh * lax.rsqrt(ms + EPS) * ng[:, cs])
        y = (jnp.concatenate(parts, axis=-1) * sg_s[rows, :].astype(F32)).astype(BF16)
        if r0 < tc:
            oc_ref[0, r0:r0 + ROW_BLOCK, :] = y
        else:
            ox_ref[0, r0 - tc:r0 - tc + ROW_BLOCK, :] = y

    n_ctx = tc // c
    n_all = n_rows // c
    cpb = ROW_BLOCK // c
    order = list(range(n_ctx - 1, -1, -1)) + list(range(n_all - 1, n_ctx - 1, -1))

    @pl.when(pl.program_id(0) >= 0)
    def _():
        for ci in order:
            scan_step(1, ci)
            if ci % cpb == 0:
                readout(ci * c)


def _hgrn(hx, hc, w_in, lower_bounds, norm_g, layer):
    bsz, t, d = hx.shape
    tc = hc.shape[1]
    e = w_in.shape[1] // 5
    heads = 2
    gw = heads * HG_KEY_DIM
    ng_blocks = e // gw
    n_rows = tc + t
    c = HG_CHUNK

    low = jnp.tril(jnp.ones((c, c), F32))
    tri = jnp.stack([jnp.tile(low, (1, 3)), jnp.tile(low.T, (1, 3))]).astype(BF16)

    def wspec(sec):
        return pl.BlockSpec((d, gw), lambda b, j, sec=sec: (0, sec * ng_blocks + j))

    seq_f32 = pltpu.VMEM((n_rows, gw), F32)
    seq_bf16 = pltpu.VMEM((n_rows, gw), BF16)
    blk_f32 = pltpu.VMEM((HG_BLOCK, gw), F32)
    blk_bf16 = pltpu.VMEM((HG_BLOCK, gw), BF16)
    n_chunks = n_rows // c
    return pl.pallas_call(
        functools.partial(_hgrn_kernel, layer=layer, tc=tc, t=t, heads=heads),
        grid=(bsz, ng_blocks),
        in_specs=[
            pl.BlockSpec((1, t, d), lambda b, j: (b, 0, 0)),
            pl.BlockSpec((1, tc, d), lambda b, j: (b, 0, 0)),
            wspec(0), wspec(1), wspec(2), wspec(3), wspec(4),
            pl.BlockSpec((lower_bounds.shape[0], gw), lambda b, j: (0, j)),
            pl.BlockSpec((1, gw), lambda b, j: (0, j)),
            pl.BlockSpec((2, c, 3 * c), lambda b, j: (0, 0, 0)),
        ],
        out_specs=[
            pl.BlockSpec((1, t, gw), lambda b, j: (b, 0, j)),
            pl.BlockSpec((1, tc, gw), lambda b, j: (b, 0, j)),
            pl.BlockSpec((1, 1, 1, gw), lambda b, j: (b, j, 0, 0)),
        ],
        out_shape=[jax.ShapeDtypeStruct((bsz, t, e), BF16),
                   jax.ShapeDtypeStruct((bsz, tc, e), BF16),
                   jax.ShapeDtypeStruct((bsz, ng_blocks, 1, gw), F32)],
        scratch_shapes=[
            blk_f32, blk_bf16, blk_f32, blk_f32, blk_f32, blk_f32,
            blk_f32, blk_bf16, blk_f32, blk_f32, blk_f32, blk_f32,
            seq_bf16, seq_bf16,
            seq_bf16,
            seq_f32, seq_f32, seq_f32,
            pltpu.VMEM((n_chunks, heads, HG_KEY_DIM, 2 * HG_KEY_DIM), F32),
            pltpu.VMEM((2, n_chunks, gw), F32),
        ],
        compiler_params=pltpu.CompilerParams(
            dimension_semantics=("arbitrary", "arbitrary"), vmem_limit_bytes=VMEM_LIMIT),
        name="hgrn2",
    )(hx, hc, w_in, w_in, w_in, w_in, w_in, lower_bounds, norm_g.reshape(1, e), tri)


def _hgrn_exact_kernel(hx_ref, hc_ref, wq_ref, wv_ref, wf_ref, wb_ref, wg_ref, lbp_ref, ng_ref,
                       ox_ref, oc_ref, q_s, v_s, ff_s, fb_s, sg_s, o_s, st_s,
                       *, layer, tc, t, heads):
    n_rows = tc + t
    kd = HG_KEY_DIM
    f_refs = (ff_s, fb_s)

    lbp = lbp_ref[...]
    ex = jnp.exp(lbp - jnp.max(lbp, axis=0, keepdims=True))
    lb = jnp.sum(ex[:layer + 1], axis=0, keepdims=True) / jnp.sum(ex, axis=0, keepdims=True)
    f_mid = 0.5 * (1.0 + lb)
    f_half = 0.5 * (1.0 - lb)

    def project(hrows, r0):
        rows = pl.ds(r0, ROW_BLOCK)
        q_s[rows, :] = _silu_half(_dot(hrows, wq_ref[...]))
        v_s[rows, :] = _dot(hrows, wv_ref[...])
        ff_s[rows, :] = f_mid + f_half * jnp.tanh(_dot(hrows, wf_ref[...]))
        fb_s[rows, :] = f_mid + f_half * jnp.tanh(_dot(hrows, wb_ref[...]))
        sg_s[rows, :] = _silu_half(_dot(hrows, wg_ref[...]))

    for i in range(tc // ROW_BLOCK):
        project(hc_ref[0, i * ROW_BLOCK:(i + 1) * ROW_BLOCK, :], i * ROW_BLOCK)

    def proj_body(i, carry):
        r0 = pl.multiple_of(i * ROW_BLOCK, ROW_BLOCK)
        project(hx_ref[0, pl.ds(r0, ROW_BLOCK), :], tc + r0)
        return carry

    lax.fori_loop(0, t // ROW_BLOCK, proj_body, 0)

    o_s[...] = jnp.zeros_like(o_s)
    st_s[...] = jnp.zeros_like(st_s)
    rid = lax.broadcasted_iota(jnp.int32, (8, kd), 0)
    n_ct = tc // 8
    n_tiles = n_rows // 8

    def tile_body(i, carry):
        tiles = (i, jnp.where(i < n_ct, n_ct - 1 - i, n_tiles + n_ct - 1 - i))
        for d in range(2):
            rows = pl.ds(pl.multiple_of(tiles[d] * 8, 8), 8)
            for h in range(heads):
                cs = slice(h * kd, (h + 1) * kd)
                f8, v8, q8 = f_refs[d][rows, cs], v_s[rows, cs], q_s[rows, cs]
                k8 = 1.0 - f8
                s = st_s[d, h]
                acc = jnp.zeros((8, kd), F32)
                for r in (range(8) if d == 0 else range(7, -1, -1)):
                    sel = rid == r
                    outer = lax.dot_general(jnp.where(sel, v8, 0.0), k8, _TN, preferred_element_type=F32)
                    s = s * f8[r:r + 1, :] + outer
                    acc = acc + lax.dot_general(jnp.where(sel, q8, 0.0), s, _NT,
                                                preferred_element_type=F32)
                st_s[d, h] = s
                o_s[rows, cs] = o_s[rows, cs] + acc
        return carry

    lax.fori_loop(0, n_tiles, tile_body, 0)

    ng = ng_ref[...]

    def readout(r0):
        rows = pl.ds(r0, ROW_BLOCK)
        o = o_s[rows, :]
        parts = []
        for h in range(heads):
            cs = slice(h * kd, (h + 1) * kd)
            oh = o[:, cs]
            ms = jnp.mean(oh * oh, axis=-1, keepdims=True)
            parts.append(oh * lax.rsqrt(ms + EPS) * ng[:, cs])
        return (jnp.concatenate(parts, axis=-1) * sg_s[rows, :]).astype(BF16)

    for i in range(tc // ROW_BLOCK):
        oc_ref[0, i * ROW_BLOCK:(i + 1) * ROW_BLOCK, :] = readout(i * ROW_BLOCK)

    def out_body(i, carry):
        r0 = pl.multiple_of(i * ROW_BLOCK, ROW_BLOCK)
        ox_ref[0, pl.ds(r0, ROW_BLOCK), :] = readout(tc + r0)
        return carry

    lax.fori_loop(0, t // ROW_BLOCK, out_body, 0)


def _hgrn_exact(hx, hc, w_in, lower_bounds, norm_g, layer):
    bsz, t, d = hx.shape
    tc = hc.shape[1]
    e = w_in.shape[1] // 5
    heads = 2
    gw = heads * HG_KEY_DIM
    ng_blocks = e // gw
    seq_f32 = pltpu.VMEM((tc + t, gw), F32)

    def wspec(sec):
        return pl.BlockSpec((d, gw), lambda b, j, sec=sec: (0, sec * ng_blocks + j))

    return pl.pallas_call(
        functools.partial(_hgrn_exact_kernel, layer=layer, tc=tc, t=t, heads=heads),
        grid=(bsz, ng_blocks),
        in_specs=[
            pl.BlockSpec((1, t, d), lambda b, j: (b, 0, 0)),
            pl.BlockSpec((1, tc, d), lambda b, j: (b, 0, 0)),
            wspec(0), wspec(1), wspec(2), wspec(3), wspec(4),
            pl.BlockSpec((lower_bounds.shape[0], gw), lambda b, j: (0, j)),
            pl.BlockSpec((1, gw), lambda b, j: (0, j)),
        ],
        out_specs=[
            pl.BlockSpec((1, t, gw), lambda b, j: (b, 0, j)),
            pl.BlockSpec((1, tc, gw), lambda b, j: (b, 0, j)),
        ],
        out_shape=[jax.ShapeDtypeStruct((bsz, t, e), BF16),
                   jax.ShapeDtypeStruct((bsz, tc, e), BF16)],
        scratch_shapes=[
            seq_f32, seq_f32, seq_f32, seq_f32, seq_f32,
            seq_f32,
            pltpu.VMEM((2, heads, HG_KEY_DIM, HG_KEY_DIM), F32),
        ],
        compiler_params=pltpu.CompilerParams(
            dimension_semantics=("arbitrary", "arbitrary"), vmem_limit_bytes=VMEM_LIMIT),
        name="hgrn2_exact",
    )(hx, hc, w_in, w_in, w_in, w_in, w_in, lower_bounds, norm_g.reshape(1, e))


def _outproj_mid_kernel(ogx_ref, ogc_ref, x_ref, ctx_ref, w_ref, mod0_ref, mod1_ref, g1_ref,
                        x1_ref, hx_ref, hc_ref, *, d, ctx_row):
    b = pl.program_id(0)
    g1 = g1_ref[...]
    row0 = mod0_ref[pl.ds(b, 1), :]
    row1 = mod1_ref[pl.ds(b, 1), :]
    x1 = x_ref[0] + row0[:, 2 * d:] * _dot(ogx_ref[0], w_ref[...])
    x1_ref[0] = x1
    hx_ref[0] = _norm_mod(x1, g1, row1[:, :d], row1[:, d:2 * d]).astype(BF16)

    @pl.when(pl.program_id(1) == 0)
    def _():
        c0 = mod0_ref[ctx_row:ctx_row + 1, :]
        c1 = mod1_ref[ctx_row:ctx_row + 1, :]
        ctx1 = ctx_ref[0] + c0[:, 2 * d:] * _dot(ogc_ref[0], w_ref[...])
        hc_ref[0] = _norm_mod(ctx1, g1, c1[:, :d], c1[:, d:2 * d]).astype(BF16)


def _outproj_mid(ogx, ogc, x, ctx, w_out, mod0, mod1, g1):
    bsz, t, d = x.shape
    tc = ctx.shape[1]
    e = w_out.shape[0]
    rb = 1024
    return pl.pallas_call(
        functools.partial(_outproj_mid_kernel, d=d, ctx_row=bsz),
        grid=(bsz, t // rb),
        in_specs=[
            pl.BlockSpec((1, rb, e), lambda b, r: (b, r, 0)),
            pl.BlockSpec((1, tc, e), lambda b, r: (b, 0, 0)),
            pl.BlockSpec((1, rb, d), lambda b, r: (b, r, 0)),
            pl.BlockSpec((1, tc, d), lambda b, r: (b, 0, 0)),
            pl.BlockSpec((e, d), lambda b, r: (0, 0)),
            pl.BlockSpec((MOD_ROWS, 3 * d), lambda b, r: (0, 0)),
            pl.BlockSpec((MOD_ROWS, 3 * d), lambda b, r: (0, 0)),
            pl.BlockSpec((1, d), lambda b, r: (0, 0)),
        ],
        out_specs=[
            pl.BlockSpec((1, rb, d), lambda b, r: (b, r, 0)),
            pl.BlockSpec((1, rb, d), lambda b, r: (b, r, 0)),
            pl.BlockSpec((1, tc, d), lambda b, r: (b, 0, 0)),
        ],
        out_shape=[jax.ShapeDtypeStruct((bsz, t, d), F32),
                   jax.ShapeDtypeStruct((bsz, t, d), BF16),
                   jax.ShapeDtypeStruct((bsz, tc, d), BF16)],
        compiler_params=pltpu.CompilerParams(
            dimension_semantics=("arbitrary", "arbitrary"), vmem_limit_bytes=VMEM_LIMIT),
        name="outproj0",
    )(ogx, ogc, x, ctx, w_out, mod0, mod1, g1.reshape(1, d))


def _outproj_final_kernel(yg_ref, x_ref, w_ref, mod_ref, g_ref, o_ref, *, d):
    b = pl.program_id(0)
    row = mod_ref[pl.ds(b, 1), :]
    x2 = x_ref[0] + row[:, 2 * d:] * _dot(yg_ref[0], w_ref[...])
    ms = jnp.mean(x2 * x2, axis=-1, keepdims=True)
    o_ref[0] = x2 * lax.rsqrt(ms + EPS) * g_ref[...]


def _outproj_final(yg, x1, w_out, mod, g):
    bsz, t, d = x1.shape
    e = w_out.shape[0]
    rb = 1024
    return pl.pallas_call(
        functools.partial(_outproj_final_kernel, d=d),
        grid=(bsz, t // rb),
        in_specs=[
            pl.BlockSpec((1, rb, e), lambda b, r: (b, r, 0)),
            pl.BlockSpec((1, rb, d), lambda b, r: (b, r, 0)),
            pl.BlockSpec((e, d), lambda b, r: (0, 0)),
            pl.BlockSpec((MOD_ROWS, 3 * d), lambda b, r: (0, 0)),
            pl.BlockSpec((1, d), lambda b, r: (0, 0)),
        ],
        out_specs=pl.BlockSpec((1, rb, d), lambda b, r: (b, r, 0)),
        out_shape=jax.ShapeDtypeStruct((bsz, t, d), F32),
        compiler_params=pltpu.CompilerParams(
            dimension_semantics=("arbitrary", "arbitrary"), vmem_limit_bytes=VMEM_LIMIT),
        name="outproj1",
    )(yg, x1, w_out, mod, g.reshape(1, d))


def _shift_rows(xv, k):
    n = xv.shape[0]
    rid = lax.broadcasted_iota(jnp.int32, xv.shape, 0)
    rolled = pltpu.roll(xv, k % n, 0)
    valid = (rid >= k) if k > 0 else (rid < n + k)
    return jnp.where(valid, rolled, 0.0)


def _tile_scan(a, u, reverse):
    n = a.shape[0]
    rid = lax.broadcasted_iota(jnp.int32, a.shape, 0)
    s = 1
    while s < n:
        valid = (rid < n - s) if reverse else (rid >= s)
        sh = (n - s) if reverse else s
        a_sh = jnp.where(valid, pltpu.roll(a, sh, 0), 1.0)
        u_sh = jnp.where(valid, pltpu.roll(u, sh, 0), 0.0)
        u = u + a * u_sh
        a = a * a_sh
        s *= 2
    return a, u


def _rglru_kernel(hx_ref, hc_ref, wx_ref, wg_ref, cw_ref, cb_ref, wa_ref, wi_ref, ba_ref, bi_ref,
                  lam_ref, y_ref,
                  xp_s, xc_s, sg_s, af_s, uf_s, ab_s, ub_s,
                  *, tc, t):
    gw = RG_BLOCK
    gr = GRID_W
    n_tr = t // gr
    a_refs = (af_s, ab_s)
    u_refs = (uf_s, ub_s)
    cw = 0.5 * cw_ref[...]
    cb = 0.5 * cb_ref[...]

    def conv_taps(xm2, xm1, x0, xp1):
        return cw[0:1] * xm2 + cw[1:2] * xm1 + cw[2:3] * x0 + cw[3:4] * xp1 + cb

    xcx = _dot(hc_ref[0], wx_ref[...])
    xc_s[0:tc, :] = conv_taps(_shift_rows(xcx, 2), _shift_rows(xcx, 1), xcx, _shift_rows(xcx, -1))

    pad = 2 * gr
    nb = t // ROW_BLOCK
    hsp = (0.5 * RG_C) * jax.nn.softplus(-lam_ref[...])
    hba = 0.5 * ba_ref[...]
    hbi = 0.5 * bi_ref[...]

    def lat_r0(i):
        return i * ROW_BLOCK if isinstance(i, int) else pl.multiple_of(i * ROW_BLOCK, ROW_BLOCK)

    def proj_x(i):
        r0 = lat_r0(i)
        xp_s[pl.ds(pad + r0, ROW_BLOCK), :] = _dot(hx_ref[0, pl.ds(r0, ROW_BLOCK), :], wx_ref[...])

    def proj_g(i):
        r0 = lat_r0(i)
        sg_s[pl.ds(r0, ROW_BLOCK), :] = _silu_half(
            _dot(hx_ref[0, pl.ds(r0, ROW_BLOCK), :], wg_ref[...])).astype(BF16)

    def conv_block(i):
        for k in range(ROW_BLOCK // gr):
            r0 = lat_r0(i) + k * gr
            xc_s[pl.ds(tc + r0, gr), :] = conv_taps(
                xp_s[pl.ds(r0, gr), :], xp_s[pl.ds(r0 + gr, gr), :],
                xp_s[pl.ds(r0 + 2 * gr, gr), :], xp_s[pl.ds(r0 + 3 * gr, gr), :])

    tot = {"H": jnp.zeros((gr, gw), F32), "A": jnp.ones((gr, gw), F32),
           "E": jnp.zeros((gr, gw), F32), "G": jnp.ones((gr, gw), F32)}

    def gates(r0, d, latent):
        for part in range(ROW_BLOCK // GATE_ROWS):
            gates_part(r0 + part * GATE_ROWS, d, latent)

    def gates_part(r0, d, latent):
        rows = pl.ds(r0, GATE_ROWS)
        hxc = xc_s[rows, :]
        xcb = hxc.astype(BF16)
        tr = jnp.tanh(_dot(xcb, wa_ref[d, 0]) + hba[d:d + 1, :])
        ti = jnp.tanh(_dot(xcb, wi_ref[d, 0]) + hbi[d:d + 1, :])
        nla = hsp[d:d + 1, :] + hsp[d:d + 1, :] * tr
        a = jnp.exp(-nla)
        a_refs[d][rows, :] = a
        z = jnp.tanh(nla) * (a * a + 1.0)
        u = (z * lax.rsqrt(jnp.maximum(z, F32_TINY))) * (hxc + hxc * ti)
        u_refs[d][rows, :] = u
        if not latent:
            return
        for k in range(GATE_ROWS // gr):
            ak, uk = a[k * gr:(k + 1) * gr, :], u[k * gr:(k + 1) * gr, :]
            if d == 0:
                tot["H"] = ak * tot["H"] + uk
                tot["A"] = ak * tot["A"]
            else:
                tot["E"] = tot["E"] + tot["G"] * uk
                tot["G"] = tot["G"] * ak

    proj_x(nb - 1)
    for i in range(tc // ROW_BLOCK):
        gates(i * ROW_BLOCK, 0, False)
    proj_g(nb - 1)
    for i in range(tc // ROW_BLOCK):
        gates(i * ROW_BLOCK, 1, False)
    def ctx_final(d):
        n_ct = tc // 8
        tiles = [_tile_scan(a_refs[d][i * 8:(i + 1) * 8, :], u_refs[d][i * 8:(i + 1) * 8, :], d == 1)
                 for i in range(n_ct)]
        h = jnp.zeros((1, gw), F32)
        for i in (range(n_ct) if d == 0 else range(n_ct - 1, -1, -1)):
            a, u = tiles[i]
            h = a[7:8] * h + u[7:8] if d == 0 else a[0:1] * h + u[0:1]
        return h

    hf = ctx_final(0)
    hb = ctx_final(1)
    xp_s[0:gr, :] = _shift_rows(xp_s[pad + (n_tr - 2) * gr:pad + (n_tr - 1) * gr, :], 1)
    xp_s[gr:pad, :] = _shift_rows(xp_s[pad + (n_tr - 1) * gr:pad + n_tr * gr, :], 1)
    proj_x(0)
    proj_g(0)
    xp_s[pad + n_tr * gr:pad + (n_tr + 1) * gr, :] = _shift_rows(xp_s[pad:pad + gr, :], -1)

    def pipe_body(k):
        proj_x(k + 1)
        conv_block(k)
        gates(tc + lat_r0(k), 0, True)
        proj_g(k + 1)
        gates(tc + lat_r0(k), 1, True)

    for k in range(nb - 2):
        pipe_body(k)
    for i in (nb - 2, nb - 1):
        conv_block(i)
        gates(tc + lat_r0(i), 0, True)
        gates(tc + lat_r0(i), 1, True)

    def column_carries(e_tot, p_tot, h0, reverse):
        rid = lax.broadcasted_iota(jnp.int32, (8, gw), 0)
        order = range(gr // 8 - 1, -1, -1) if reverse else range(gr // 8)
        carry, out = h0, [None] * (gr // 8)
        for i in order:
            a_t, u_t = _tile_scan(p_tot[i * 8:(i + 1) * 8, :], e_tot[i * 8:(i + 1) * 8, :], reverse)
            h = u_t + a_t * carry
            if reverse:
                out[i] = jnp.where(rid == 7, carry, pltpu.roll(h, 7, 0))
                carry = h[0:1]
            else:
                out[i] = jnp.where(rid == 0, carry, pltpu.roll(h, 1, 0))
                carry = h[7:8]
        return jnp.concatenate(out, axis=0)

    h = column_carries(tot["H"], tot["A"], hf, False)
    for r in range(n_tr):
        rows = slice(tc + r * gr, tc + (r + 1) * gr)
        h = af_s[rows, :] * h + uf_s[rows, :]
        uf_s[rows, :] = h
    h = column_carries(tot["E"], tot["G"], hb, True)
    for r in range(n_tr - 1, -1, -1):
        rows = slice(tc + r * gr, tc + (r + 1) * gr)
        h = ab_s[rows, :] * h + ub_s[rows, :]
        gate = sg_s[r * gr:(r + 1) * gr, :].astype(F32)
        y_ref[0, r * gr:(r + 1) * gr, :] = ((uf_s[rows, :] + h) * gate).astype(BF16)


def _rglru(hx, hc, w_in, conv_w, conv_b, w_a, b_a, w_x, b_x, lam):
    bsz, t, d = hx.shape
    tc = hc.shape[1]
    e = w_in.shape[1] // 2
    gw = RG_BLOCK
    nb = e // gw
    n_rows = tc + t
    seq_f32 = pltpu.VMEM((n_rows, gw), F32)
    vec = lambda rows: pl.BlockSpec((rows, gw), lambda b, j: (0, j))
    gate_w = pl.BlockSpec((2, 1, gw, gw), lambda b, j: (0, j, 0, 0))
    return pl.pallas_call(
        functools.partial(_rglru_kernel, tc=tc, t=t),
        grid=(bsz, nb),
        in_specs=[
            pl.BlockSpec((1, t, d), lambda b, j: (b, 0, 0)),
            pl.BlockSpec((1, tc, d), lambda b, j: (b, 0, 0)),
            pl.BlockSpec((d, gw), lambda b, j: (0, j)),
            pl.BlockSpec((d, gw), lambda b, j: (0, nb + j)),
            vec(conv_w.shape[0]), vec(1),
            gate_w, gate_w, vec(2), vec(2), vec(2),
        ],
        out_specs=pl.BlockSpec((1, t, gw), lambda b, j: (b, 0, j)),
        out_shape=jax.ShapeDtypeStruct((bsz, t, e), BF16),
        scratch_shapes=[
            pltpu.VMEM((t + 3 * GRID_W, gw), F32),
            seq_f32,
            pltpu.VMEM((t, gw), BF16),
            seq_f32, seq_f32, seq_f32, seq_f32,
        ],
        compiler_params=pltpu.CompilerParams(
            dimension_semantics=("arbitrary", "arbitrary"), vmem_limit_bytes=VMEM_LIMIT),
        name="rglru",
    )(hx, hc, w_in, w_in, conv_w, conv_b.reshape(1, e), w_a, w_x, b_a, b_x, lam)


def kernel(x, c, ctx, c_ctx, ada_w, ada_b, norm_g, final_norm_g, hg_w_in, hg_lower_bounds,
           hg_norm_g, hg_w_out, rg_w_in, rg_conv_w, rg_conv_b, rg_w_a, rg_b_a, rg_w_x, rg_b_x,
           rg_lambda, rg_w_out):
    bsz, t, d = x.shape
    assert ada_w.shape[0] == 2 and bsz < MOD_ROWS and t % GRID_W == 0
    cond = jnp.concatenate(
        [c, c_ctx[None, :], jnp.zeros((MOD_ROWS - bsz - 1, d), F32)], axis=0)
    mod = _ada(cond, ada_w, ada_b)

    hx, hc = _prenorm(x, ctx, mod[0], norm_g[0])
    e = hg_w_in.shape[-1] // 5
    halves = jnp.concatenate([jnp.full((e,), 0.5, F32), jnp.ones((e,), F32), jnp.full((3 * e,), 0.5, F32)])
    w_hg = (hg_w_in[0] * halves).astype(BF16)
    ogx, ogc, half_decay = _hgrn(hx, hc, w_hg, hg_lower_bounds, hg_norm_g[0], layer=0)
    ogx, ogc = lax.cond(
        jnp.all(half_decay < HG_MAX_HALF_DECAY),
        lambda: (ogx, ogc),
        lambda: tuple(_hgrn_exact(hx, hc, w_hg, hg_lower_bounds, hg_norm_g[0], layer=0)))
    x1, hx1, hc1 = _outproj_mid(ogx, ogc, x, ctx, hg_w_out[0].astype(BF16), mod[0], mod[1], norm_g[1])
    er = rg_w_in.shape[-1] // 2
    rg_halves = jnp.concatenate([jnp.ones((er,), F32), jnp.full((er,), 0.5, F32)])
    yg = _rglru(hx1, hc1, (rg_w_in[0] * rg_halves).astype(BF16), rg_conv_w[0], rg_conv_b[0],
                rg_w_a[0].astype(BF16), rg_b_a[0], rg_w_x[0].astype(BF16), rg_b_x[0], rg_lambda[0])
    return _outproj_final(yg, x1, rg_w_out[0].astype(BF16), mod[1], final_norm_g)
```

```python
import functools

import jax
import jax.numpy as jnp
from jax import lax
from jax.experimental import pallas as pl
from jax.experimental.pallas import tpu as pltpu

F32 = jnp.float32
BF16 = jnp.bfloat16

EPS = 1e-6
F32_TINY = 1.1754944e-38
GRID_W = 64
HG_KEY_DIM = 128
HG_CHUNK = 64
HG_MAX_HALF_DECAY = 80.0
RG_BLOCK = 256
RG_C = 8.0
MOD_ROWS = 16
ROW_BLOCK = 256
GATE_ROWS = 64
HG_BLOCK = 512
VMEM_LIMIT = 56 * 1024 * 1024

_NT = (((1,), (1,)), ((), ()))
_TN = (((0,), (0,)), ((), ()))


def _silu(z):
    return _silu_half(0.5 * z)


def _silu_half(hz):
    return hz + hz * jnp.tanh(hz)


def _dot(a, b):
    return jnp.dot(a, b, preferred_element_type=F32)


def _ada_kernel(c_ref, w_ref, b_ref, o_ref):
    s = _silu(c_ref[...])
    w = w_ref[0]
    s_hi = s.astype(BF16)
    s_lo = (s - s_hi.astype(F32)).astype(BF16)
    w_hi = w.astype(BF16)
    w_lo = (w - w_hi.astype(F32)).astype(BF16)
    both = _dot(jnp.concatenate([s_hi, s_lo], axis=0), w_hi)
    o_ref[0] = both[:MOD_ROWS] + both[MOD_ROWS:] + _dot(s_hi, w_lo) + b_ref[0]


def _ada(cond, ada_w, ada_b):
    depth, d, n3 = ada_w.shape
    bn = d
    return pl.pallas_call(
        _ada_kernel,
        grid=(depth, n3 // bn),
        in_specs=[
            pl.BlockSpec((MOD_ROWS, d), lambda i, n: (0, 0)),
            pl.BlockSpec((1, d, bn), lambda i, n: (i, 0, n)),
            pl.BlockSpec((1, 1, bn), lambda i, n: (i, 0, n)),
        ],
        out_specs=pl.BlockSpec((1, MOD_ROWS, bn), lambda i, n: (i, 0, n)),
        out_shape=jax.ShapeDtypeStruct((depth, MOD_ROWS, n3), F32),
        name="ada",
    )(cond, ada_w, ada_b.reshape(depth, 1, n3))


def _norm_mod(xv, g, shift, scale):
    ms = jnp.mean(xv * xv, axis=-1, keepdims=True)
    return xv * lax.rsqrt(ms + EPS) * g * (1.0 + scale) + shift


def _prenorm_kernel(x_ref, ctx_ref, mod_ref, g_ref, hx_ref, hc_ref, *, d, ctx_row):
    b = pl.program_id(0)
    g = g_ref[...]
    row = mod_ref[pl.ds(b, 1), :]
    hx_ref[0] = _norm_mod(x_ref[0], g, row[:, :d], row[:, d:2 * d]).astype(BF16)

    @pl.when(pl.program_id(1) == 0)
    def _():
        rowc = mod_ref[ctx_row:ctx_row + 1, :]
        hc_ref[0] = _norm_mod(ctx_ref[0], g, rowc[:, :d], rowc[:, d:2 * d]).astype(BF16)


def _prenorm(x, ctx, mod, g):
    bsz, t, d = x.shape
    tc = ctx.shape[1]
    rb = 1024
    return pl.pallas_call(
        functools.partial(_prenorm_kernel, d=d, ctx_row=bsz),
        grid=(bsz, t // rb),
        in_specs=[
            pl.BlockSpec((1, rb, d), lambda b, r: (b, r, 0)),
            pl.BlockSpec((1, tc, d), lambda b, r: (b, 0, 0)),
            pl.BlockSpec((MOD_ROWS, 3 * d), lambda b, r: (0, 0)),
            pl.BlockSpec((1, d), lambda b, r: (0, 0)),
        ],
        out_specs=[
            pl.BlockSpec((1, rb, d), lambda b, r: (b, r, 0)),
            pl.BlockSpec((1, tc, d), lambda b, r: (b, 0, 0)),
        ],
        out_shape=[jax.ShapeDtypeStruct((bsz, t, d), BF16),
                   jax.ShapeDtypeStruct((bsz, tc, d), BF16)],
        name="prenorm",
    )(x, ctx, mod, g.reshape(1, d))


def _hgrn_kernel(hx_ref, hc_ref, wq_ref, wv_ref, wf_ref, wb_ref, wg_ref, lbp_ref, ng_ref, tri_ref,
                 ox_ref, oc_ref, decay_ref,
                 qa_b, va_b, lffa_b, lfba_b, kfa_b, kba_b, qb_b, vb_b, lffb_b, lfbb_b, kfb_b, kbb_b,
                 qtf_s, qtb_s, sg_s, o_s, oif_s, oib_s, ut_s, em_s,
                 *, layer, tc, t, heads):
    n_rows = tc + t
    c = HG_CHUNK
    kd = HG_KEY_DIM
    qt_s = (qtf_s, qtb_s)
    oi_s = (oif_s, oib_s)

    lbp = lbp_ref[...]
    ex = jnp.exp(lbp - jnp.max(lbp, axis=0, keepdims=True))
    lb = jnp.sum(ex[:layer + 1], axis=0, keepdims=True) / jnp.sum(ex, axis=0, keepdims=True)
    f_mid = 0.5 * (1.0 + lb)
    f_half = 0.5 * (1.0 - lb)

    rid = lax.broadcasted_iota(jnp.int32, (c, c), 0)
    cid = lax.broadcasted_iota(jnp.int32, (c, c), 1)
    masks = (rid >= cid, rid <= cid)

    def split3(xv):
        hi = xv.astype(BF16)
        r1 = xv - hi.astype(F32)
        mid = r1.astype(BF16)
        lo = (r1 - mid.astype(F32)).astype(BF16)
        return jnp.concatenate([hi, mid, lo], axis=0)

    def proj_steps(hrows, buf, r0):
        nr = hrows.shape[0]
        q_b, v_b, lf_b, k_b = buf

        def step_q():
            q_b[0:nr, :] = _silu_half(_dot(hrows, wq_ref[...]))

        def step_v():
            v_b[0:nr, :] = _dot(hrows, wv_ref[...]).astype(BF16)

        def step_f(d):
            f = f_mid + f_half * jnp.tanh(_dot(hrows, (wf_ref, wb_ref)[d][...]))
            lf_b[d][0:nr, :] = jnp.log(f)
            k_b[d][0:nr, :] = 1.0 - f

        def step_g():
            sg_s[pl.ds(r0, nr), :] = _silu_half(_dot(hrows, wg_ref[...])).astype(BF16)

        return [step_q, step_v, functools.partial(step_f, 0), functools.partial(step_f, 1), step_g]

    def chunk_steps(buf, nr, r0, c0):
        q_b, v_b, lf_b, k_b = buf
        crs = [slice(cc * c, (cc + 1) * c) for cc in range(nr // c)]
        vals = {}

        def step_cumsum():
            vals["b"] = [[_dot(tri_ref[d], split3(lf_b[d][cr, :])) for d in range(2)] for cr in crs]

        def step_scale():
            qts, kts = [], []
            for cc, cr in enumerate(crs):
                qv = q_b[cr, :]
                qt, kt = [], []
                for d in range(2):
                    bsum = vals["b"][cc][d]
                    m = 0.5 * (bsum[c - 1:c, :] if d == 0 else bsum[0:1, :])
                    half_decay[0] = jnp.maximum(half_decay[0], jnp.abs(m))
                    em_s[d, pl.ds(c0 + cc, 1), :] = jnp.exp(m)
                    qt.append((qv * jnp.exp(bsum - m)).astype(BF16))
                    kt.append((k_b[d][cr, :] * jnp.exp(m - bsum)).astype(BF16))
                    qt_s[d][pl.ds(r0 + cc * c, c), :] = qt[d]
                qts.append(qt)
                kts.append(kt)
            vals["qt"], vals["kt"] = qts, kts

        def step_scores():
            scs = []
            for cc in range(len(crs)):
                for h in range(heads):
                    cs = slice(h * kd, (h + 1) * kd)
                    qt, kt = vals["qt"][cc], vals["kt"][cc]
                    sf = lax.dot_general(qt[0][:, cs], kt[0][:, cs], _NT, preferred_element_type=F32)
                    sb = lax.dot_general(qt[1][:, cs], kt[1][:, cs], _NT, preferred_element_type=F32)
                    scs.append((jnp.where(masks[0], sf, 0.0) + jnp.where(masks[1], sb, 0.0)).astype(BF16))
            vals["sc"] = scs

        def step_out():
            for cc, cr in enumerate(crs):
                for h in range(heads):
                    cs = slice(h * kd, (h + 1) * kd)
                    vv = v_b[cr, cs]
                    kt = vals["kt"][cc]
                    o_s[pl.ds(r0 + cc * c, c), cs] = _dot(vals["sc"][cc * heads + h], vv)
                    kk = jnp.concatenate([kt[0][:, cs], kt[1][:, cs]], axis=1)
                    ut_s[c0 + cc, h] = lax.dot_general(vv, kk, _TN, preferred_element_type=F32)

        return [step_cumsum, step_scale, step_scores, step_out]

    half_decay = [jnp.zeros((1, heads * kd), F32)]

    bufs = ((qa_b, va_b, (lffa_b, lfba_b), (kfa_b, kba_b)),
            (qb_b, vb_b, (lffb_b, lfbb_b), (kfb_b, kbb_b)))
    blocks = [(hc_ref[0], 0, tc)]
    blocks += [(hx_ref[0, i * HG_BLOCK:(i + 1) * HG_BLOCK, :], tc + i * HG_BLOCK, HG_BLOCK)
               for i in range(t // HG_BLOCK)]
    units = [(d, h) for d in range(2) for h in range(heads)]
    st = {u: jnp.zeros((kd, kd), F32) for u in units}

    def scan_step(d, ci):
        rows = slice(ci * c, (ci + 1) * c)
        for h in range(heads):
            cs = slice(h * kd, (h + 1) * kd)
            em = em_s[d, ci:ci + 1, cs]
            sm = st[(d, h)] * em
            st[(d, h)] = em * (sm + ut_s[ci, h][:, d * kd:(d + 1) * kd])
            oi_s[d][rows, cs] = lax.dot_general(
                qt_s[d][rows, cs], sm.astype(BF16), _NT, preferred_element_type=F32)

    prev, fwd_now, fwd_next = [], [], []
    for n, (hrows, r0, nr) in enumerate(blocks):
        proj = proj_steps(hrows, bufs[n % 2], r0)
        share = -(-len(fwd_now) // len(proj))
        for k, step in enumerate(proj):
            step()
            if k < len(prev):
                prev[k]()
            for f in fwd_now[k * share:(k + 1) * share]:
                f()
        fwd_now = fwd_next
        fwd_next = [functools.partial(scan_step, 0, ci) for ci in range(r0 // c, (r0 + nr) // c)]
        prev = chunk_steps(bufs[n % 2], nr, r0, r0 // c)
    for f in fwd_now:
        f()
    for step in prev:
        step()
    for f in fwd_next:
        f()
    decay_ref[0, 0] = half_decay[0]

    ng = ng_ref[...]

    def readout(r0):
        rows = slice(r0, r0 + ROW_BLOCK)
        o = o_s[rows, :] + oif_s[rows, :] + oib_s[rows, :]
        parts = []
        for h in range(heads):
            cs = slice(h * kd, (h + 1) * kd)
            oh = o[:, cs]
            ms = jnp.mean(oh * oh, axis=-1, keepdims=True)
            parts.append(oh * lax.rsqrt(ms + EPS) * ng[:, cs])
        y = (jnp.concatenate(parts, axis=-1) * sg_s[rows, :].astype(F32)).astype(BF16)
        if r0 < tc:
            oc_ref[0, r0:r0 + ROW_BLOCK, :] = y
        else:
            ox_ref[0, r0 - tc:r0 - tc + ROW_BLOCK, :] = y

    n_ctx = tc // c
    n_all = n_rows // c
    cpb = ROW_BLOCK // c
    order = list(range(n_ctx - 1, -1, -1)) + list(range(n_all - 1, n_ctx - 1, -1))

    @pl.when(pl.program_id(0) >= 0)
    def _():
        for ci in order:
            scan_step(1, ci)
            if ci % cpb == 0:
                readout(ci * c)


def _hgrn(hx, hc, w_in, lower_bounds, norm_g, layer):
    bsz, t, d = hx.shape
    tc = hc.shape[1]
    e = w_in.shape[1] // 5
    heads = 2
    gw = heads * HG_KEY_DIM
    ng_blocks = e // gw
    n_rows = tc + t
    c = HG_CHUNK

    low = jnp.tril(jnp.ones((c, c), F32))
    tri = jnp.stack([jnp.tile(low, (1, 3)), jnp.tile(low.T, (1, 3))]).astype(BF16)

    def wspec(sec):
        return pl.BlockSpec((d, gw), lambda b, j, sec=sec: (0, sec * ng_blocks + j))

    seq_f32 = pltpu.VMEM((n_rows, gw), F32)
    seq_bf16 = pltpu.VMEM((n_rows, gw), BF16)
    blk_f32 = pltpu.VMEM((HG_BLOCK, gw), F32)
    blk_bf16 = pltpu.VMEM((HG_BLOCK, gw), BF16)
    n_chunks = n_rows // c
    return pl.pallas_call(
        functools.partial(_hgrn_kernel, layer=layer, tc=tc, t=t, heads=heads),
        grid=(bsz, ng_blocks),
        in_specs=[
            pl.BlockSpec((1, t, d), lambda b, j: (b, 0, 0)),
            pl.BlockSpec((1, tc, d), lambda b, j: (b, 0, 0)),
            wspec(0), wspec(1), wspec(2), wspec(3), wspec(4),
            pl.BlockSpec((lower_bounds.shape[0], gw), lambda b, j: (0, j)),
            pl.BlockSpec((1, gw), lambda b, j: (0, j)),
            pl.BlockSpec((2, c, 3 * c), lambda b, j: (0, 0, 0)),
        ],
        out_specs=[
            pl.BlockSpec((1, t, gw), lambda b, j: (b, 0, j)),
            pl.BlockSpec((1, tc, gw), lambda b, j: (b, 0, j)),
            pl.BlockSpec((1, 1, 1, gw), lambda b, j: (b, j, 0, 0)),
        ],
        out_shape=[jax.ShapeDtypeStruct((bsz, t, e), BF16),
                   jax.ShapeDtypeStruct((bsz, tc, e), BF16),
                   jax.ShapeDtypeStruct((bsz, ng_blocks, 1, gw), F32)],
        scratch_shapes=[
            blk_f32, blk_bf16, blk_f32, blk_f32, blk_f32, blk_f32,
            blk_f32, blk_bf16, blk_f32, blk_f32, blk_f32, blk_f32,
            seq_bf16, seq_bf16,
            seq_bf16,
            seq_f32, seq_f32, seq_f32,
            pltpu.VMEM((n_chunks, heads, HG_KEY_DIM, 2 * HG_KEY_DIM), F32),
            pltpu.VMEM((2, n_chunks, gw), F32),
        ],
        compiler_params=pltpu.CompilerParams(
            dimension_semantics=("arbitrary", "arbitrary"), vmem_limit_bytes=VMEM_LIMIT),
        name="hgrn2",
    )(hx, hc, w_in, w_in, w_in, w_in, w_in, lower_bounds, norm_g.reshape(1, e), tri)


def _hgrn_exact_kernel(hx_ref, hc_ref, wq_ref, wv_ref, wf_ref, wb_ref, wg_ref, lbp_ref, ng_ref,
                       ox_ref, oc_ref, q_s, v_s, ff_s, fb_s, sg_s, o_s, st_s,
                       *, layer, tc, t, heads):
    n_rows = tc + t
    kd = HG_KEY_DIM
    f_refs = (ff_s, fb_s)

    lbp = lbp_ref[...]
    ex = jnp.exp(lbp - jnp.max(lbp, axis=0, keepdims=True))
    lb = jnp.sum(ex[:layer + 1], axis=0, keepdims=True) / jnp.sum(ex, axis=0, keepdims=True)
    f_mid = 0.5 * (1.0 + lb)
    f_half = 0.5 * (1.0 - lb)

    def project(hrows, r0):
        rows = pl.ds(r0, ROW_BLOCK)
        q_s[rows, :] = _silu_half(_dot(hrows, wq_ref[...]))
        v_s[rows, :] = _dot(hrows, wv_ref[...])
        ff_s[rows, :] = f_mid + f_half * jnp.tanh(_dot(hrows, wf_ref[...]))
        fb_s[rows, :] = f_mid + f_half * jnp.tanh(_dot(hrows, wb_ref[...]))
        sg_s[rows, :] = _silu_half(_dot(hrows, wg_ref[...]))

    for i in range(tc // ROW_BLOCK):
        project(hc_ref[0, i * ROW_BLOCK:(i + 1) * ROW_BLOCK, :], i * ROW_BLOCK)

    def proj_body(i, carry):
        r0 = pl.multiple_of(i * ROW_BLOCK, ROW_BLOCK)
        project(hx_ref[0, pl.ds(r0, ROW_BLOCK), :], tc + r0)
        return carry

    lax.fori_loop(0, t // ROW_BLOCK, proj_body, 0)

    o_s[...] = jnp.zeros_like(o_s)
    st_s[...] = jnp.zeros_like(st_s)
    rid = lax.broadcasted_iota(jnp.int32, (8, kd), 0)
    n_ct = tc // 8
    n_tiles = n_rows // 8

    def tile_body(i, carry):
        tiles = (i, jnp.where(i < n_ct, n_ct - 1 - i, n_tiles + n_ct - 1 - i))
        for d in range(2):
            rows = pl.ds(pl.multiple_of(tiles[d] * 8, 8), 8)
            for h in range(heads):
                cs = slice(h * kd, (h + 1) * kd)
                f8, v8, q8 = f_refs[d][rows, cs], v_s[rows, cs], q_s[rows, cs]
                k8 = 1.0 - f8
                s = st_s[d, h]
                acc = jnp.zeros((8, kd), F32)
                for r in (range(8) if d == 0 else range(7, -1, -1)):
                    sel = rid == r
                    outer = lax.dot_general(jnp.where(sel, v8, 0.0), k8, _TN, preferred_element_type=F32)
                    s = s * f8[r:r + 1, :] + outer
                    acc = acc + lax.dot_general(jnp.where(sel, q8, 0.0), s, _NT,
                                                preferred_element_type=F32)
                st_s[d, h] = s
                o_s[rows, cs] = o_s[rows, cs] + acc
        return carry

    lax.fori_loop(0, n_tiles, tile_body, 0)

    ng = ng_ref[...]

    def readout(r0):
        rows = pl.ds(r0, ROW_BLOCK)
        o = o_s[rows, :]
        parts = []
        for h in range(heads):
            cs = slice(h * kd, (h + 1) * kd)
            oh = o[:, cs]
            ms = jnp.mean(oh * oh, axis=-1, keepdims=True)
            parts.append(oh * lax.rsqrt(ms + EPS) * ng[:, cs])
        return (jnp.concatenate(parts, axis=-1) * sg_s[rows, :]).astype(BF16)

    for i in range(tc // ROW_BLOCK):
        oc_ref[0, i * ROW_BLOCK:(i + 1) * ROW_BLOCK, :] = readout(i * ROW_BLOCK)

    def out_body(i, carry):
        r0 = pl.multiple_of(i * ROW_BLOCK, ROW_BLOCK)
        ox_ref[0, pl.ds(r0, ROW_BLOCK), :] = readout(tc + r0)
        return carry

    lax.fori_loop(0, t // ROW_BLOCK, out_body, 0)


def _hgrn_exact(hx, hc, w_in, lower_bounds, norm_g, layer):
    bsz, t, d = hx.shape
    tc = hc.shape[1]
    e = w_in.shape[1] // 5
    heads = 2
    gw = heads * HG_KEY_DIM
    ng_blocks = e // gw
    seq_f32 = pltpu.VMEM((tc + t, gw), F32)

    def wspec(sec):
        return pl.BlockSpec((d, gw), lambda b, j, sec=sec: (0, sec * ng_blocks + j))

    return pl.pallas_call(
        functools.partial(_hgrn_exact_kernel, layer=layer, tc=tc, t=t, heads=heads),
        grid=(bsz, ng_blocks),
        in_specs=[
            pl.BlockSpec((1, t, d), lambda b, j: (b, 0, 0)),
            pl.BlockSpec((1, tc, d), lambda b, j: (b, 0, 0)),
            wspec(0), wspec(1), wspec(2), wspec(3), wspec(4),
            pl.BlockSpec((lower_bounds.shape[0], gw), lambda b, j: (0, j)),
            pl.BlockSpec((1, gw), lambda b, j: (0, j)),
        ],
        out_specs=[
            pl.BlockSpec((1, t, gw), lambda b, j: (b, 0, j)),
            pl.BlockSpec((1, tc, gw), lambda b, j: (b, 0, j)),
        ],
        out_shape=[jax.ShapeDtypeStruct((bsz, t, e), BF16),
                   jax.ShapeDtypeStruct((bsz, tc, e), BF16)],
        scratch_shapes=[
            seq_f32, seq_f32, seq_f32, seq_f32, seq_f32,
            seq_f32,
            pltpu.VMEM((2, heads, HG_KEY_DIM, HG_KEY_DIM), F32),
        ],
        compiler_params=pltpu.CompilerParams(
            dimension_semantics=("arbitrary", "arbitrary"), vmem_limit_bytes=VMEM_LIMIT),
        name="hgrn2_exact",
    )(hx, hc, w_in, w_in, w_in, w_in, w_in, lower_bounds, norm_g.reshape(1, e))


def _outproj_mid_kernel(ogx_ref, ogc_ref, x_ref, ctx_ref, w_ref, mod0_ref, mod1_ref, g1_ref,
                        x1_ref, hx_ref, hc_ref, *, d, ctx_row):
    b = pl.program_id(0)
    g1 = g1_ref[...]
    row0 = mod0_ref[pl.ds(b, 1), :]
    row1 = mod1_ref[pl.ds(b, 1), :]
    x1 = x_ref[0] + row0[:, 2 * d:] * _dot(ogx_ref[0], w_ref[...])
    x1_ref[0] = x1
    hx_ref[0] = _norm_mod(x1, g1, row1[:, :d], row1[:, d:2 * d]).astype(BF16)

    @pl.when(pl.program_id(1) == 0)
    def _():
        c0 = mod0_ref[ctx_row:ctx_row + 1, :]
        c1 = mod1_ref[ctx_row:ctx_row + 1, :]
        ctx1 = ctx_ref[0] + c0[:, 2 * d:] * _dot(ogc_ref[0], w_ref[...])
        hc_ref[0] = _norm_mod(ctx1, g1, c1[:, :d], c1[:, d:2 * d]).astype(BF16)


def _outproj_mid(ogx, ogc, x, ctx, w_out, mod0, mod1, g1):
    bsz, t, d = x.shape
    tc = ctx.shape[1]
    e = w_out.shape[0]
    rb = 1024
    return pl.pallas_call(
        functools.partial(_outproj_mid_kernel, d=d, ctx_row=bsz),
        grid=(bsz, t // rb),
        in_specs=[
            pl.BlockSpec((1, rb, e), lambda b, r: (b, r, 0)),
            pl.BlockSpec((1, tc, e), lambda b, r: (b, 0, 0)),
            pl.BlockSpec((1, rb, d), lambda b, r: (b, r, 0)),
            pl.BlockSpec((1, tc, d), lambda b, r: (b, 0, 0)),
            pl.BlockSpec((e, d), lambda b, r: (0, 0)),
            pl.BlockSpec((MOD_ROWS, 3 * d), lambda b, r: (0, 0)),
            pl.BlockSpec((MOD_ROWS, 3 * d), lambda b, r: (0, 0)),
            pl.BlockSpec((1, d), lambda b, r: (0, 0)),
        ],
        out_specs=[
            pl.BlockSpec((1, rb, d), lambda b, r: (b, r, 0)),
            pl.BlockSpec((1, rb, d), lambda b, r: (b, r, 0)),
            pl.BlockSpec((1, tc, d), lambda b, r: (b, 0, 0)),
        ],
        out_shape=[jax.ShapeDtypeStruct((bsz, t, d), F32),
                   jax.ShapeDtypeStruct((bsz, t, d), BF16),
                   jax.ShapeDtypeStruct((bsz, tc, d), BF16)],
        compiler_params=pltpu.CompilerParams(
            dimension_semantics=("arbitrary", "arbitrary"), vmem_limit_bytes=VMEM_LIMIT),
        name="outproj0",
    )(ogx, ogc, x, ctx, w_out, mod0, mod1, g1.reshape(1, d))


def _outproj_final_kernel(yg_ref, x_ref, w_ref, mod_ref, g_ref, o_ref, *, d):
    b = pl.program_id(0)
    row = mod_ref[pl.ds(b, 1), :]
    x2 = x_ref[0] + row[:, 2 * d:] * _dot(yg_ref[0], w_ref[...])
    ms = jnp.mean(x2 * x2, axis=-1, keepdims=True)
    o_ref[0] = x2 * lax.rsqrt(ms + EPS) * g_ref[...]


def _outproj_final(yg, x1, w_out, mod, g):
    bsz, t, d = x1.shape
    e = w_out.shape[0]
    rb = 1024
    return pl.pallas_call(
        functools.partial(_outproj_final_kernel, d=d),
        grid=(bsz, t // rb),
        in_specs=[
            pl.BlockSpec((1, rb, e), lambda b, r: (b, r, 0)),
            pl.BlockSpec((1, rb, d), lambda b, r: (b, r, 0)),
            pl.BlockSpec((e, d), lambda b, r: (0, 0)),
            pl.BlockSpec((MOD_ROWS, 3 * d), lambda b, r: (0, 0)),
            pl.BlockSpec((1, d), lambda b, r: (0, 0)),
        ],
        out_specs=pl.BlockSpec((1, rb, d), lambda b, r: (b, r, 0)),
        out_shape=jax.ShapeDtypeStruct((bsz, t, d), F32),
        compiler_params=pltpu.CompilerParams(
            dimension_semantics=("arbitrary", "arbitrary"), vmem_limit_bytes=VMEM_LIMIT),
        name="outproj1",
    )(yg, x1, w_out, mod, g.reshape(1, d))


def _shift_rows(xv, k):
    n = xv.shape[0]
    rid = lax.broadcasted_iota(jnp.int32, xv.shape, 0)
    rolled = pltpu.roll(xv, k % n, 0)
    valid = (rid >= k) if k > 0 else (rid < n + k)
    return jnp.where(valid, rolled, 0.0)


def _tile_scan(a, u, reverse):
    n = a.shape[0]
    rid = lax.broadcasted_iota(jnp.int32, a.shape, 0)
    s = 1
    while s < n:
        valid = (rid < n - s) if reverse else (rid >= s)
        sh = (n - s) if reverse else s
        a_sh = jnp.where(valid, pltpu.roll(a, sh, 0), 1.0)
        u_sh = jnp.where(valid, pltpu.roll(u, sh, 0), 0.0)
        u = u + a * u_sh
        a = a * a_sh
        s *= 2
    return a, u


def _rglru_kernel(hx_ref, hc_ref, wx_ref, wg_ref, cw_ref, cb_ref, wa_ref, wi_ref, ba_ref, bi_ref,
                  lam_ref, y_ref,
                  xp_s, xc_s, sg_s, af_s, uf_s, ab_s, ub_s,
                  *, tc, t):
    gw = RG_BLOCK
    gr = GRID_W
    n_tr = t // gr
    a_refs = (af_s, ab_s)
    u_refs = (uf_s, ub_s)
    cw = 0.5 * cw_ref[...]
    cb = 0.5 * cb_ref[...]

    def conv_taps(xm2, xm1, x0, xp1):
        return cw[0:1] * xm2 + cw[1:2] * xm1 + cw[2:3] * x0 + cw[3:4] * xp1 + cb

    xcx = _dot(hc_ref[0], wx_ref[...])
    xc_s[0:tc, :] = conv_taps(_shift_rows(xcx, 2), _shift_rows(xcx, 1), xcx, _shift_rows(xcx, -1))

    pad = 2 * gr
    nb = t // ROW_BLOCK
    hsp = (0.5 * RG_C) * jax.nn.softplus(-lam_ref[...])
    hba = 0.5 * ba_ref[...]
    hbi = 0.5 * bi_ref[...]

    def lat_r0(i):
        return i * ROW_BLOCK if isinstance(i, int) else pl.multiple_of(i * ROW_BLOCK, ROW_BLOCK)

    def proj_x(i):
        r0 = lat_r0(i)
        xp_s[pl.ds(pad + r0, ROW_BLOCK), :] = _dot(hx_ref[0, pl.ds(r0, ROW_BLOCK), :], wx_ref[...])

    def proj_g(i):
        r0 = lat_r0(i)
        sg_s[pl.ds(r0, ROW_BLOCK), :] = _silu_half(
            _dot(hx_ref[0, pl.ds(r0, ROW_BLOCK), :], wg_ref[...])).astype(BF16)

    def conv_block(i):
        for k in range(ROW_BLOCK // gr):
            r0 = lat_r0(i) + k * gr
            xc_s[pl.ds(tc + r0, gr), :] = conv_taps(
                xp_s[pl.ds(r0, gr), :], xp_s[pl.ds(r0 + gr, gr), :],
                xp_s[pl.ds(r0 + 2 * gr, gr), :], xp_s[pl.ds(r0 + 3 * gr, gr), :])

    tot = {"H": jnp.zeros((gr, gw), F32), "A": jnp.ones((gr, gw), F32),
           "E": jnp.zeros((gr, gw), F32), "G": jnp.ones((gr, gw), F32)}

    def gates(r0, d, latent):
        for part in range(ROW_BLOCK // GATE_ROWS):
            gates_part(r0 + part * GATE_ROWS, d, latent)

    def gates_part(r0, d, latent):
        rows = pl.ds(r0, GATE_ROWS)
        hxc = xc_s[rows, :]
        xcb = hxc.astype(BF16)
        tr = jnp.tanh(_dot(xcb, wa_ref[d, 0]) + hba[d:d + 1, :])
        ti = jnp.tanh(_dot(xcb, wi_ref[d, 0]) + hbi[d:d + 1, :])
        nla = hsp[d:d + 1, :] + hsp[d:d + 1, :] * tr
        a = jnp.exp(-nla)
        a_refs[d][rows, :] = a
        z = jnp.tanh(nla) * (a * a + 1.0)
        u = (z * lax.rsqrt(jnp.maximum(z, F32_TINY))) * (hxc + hxc * ti)
        u_refs[d][rows, :] = u
        if not latent:
            return
        for k in range(GATE_ROWS // gr):
            ak, uk = a[k * gr:(k + 1) * gr, :], u[k * gr:(k + 1) * gr, :]
            if d == 0:
                tot["H"] = ak * tot["H"] + uk
                tot["A"] = ak * tot["A"]
            else:
                tot["E"] = tot["E"] + tot["G"] * uk
                tot["G"] = tot["G"] * ak

    proj_x(nb - 1)
    for i in range(tc // ROW_BLOCK):
        gates(i * ROW_BLOCK, 0, False)
    proj_g(nb - 1)
    for i in range(tc // ROW_BLOCK):
        gates(i * ROW_BLOCK, 1, False)
    def ctx_final(d):
        n_ct = tc // 8
        tiles = [_tile_scan(a_refs[d][i * 8:(i + 1) * 8, :], u_refs[d][i * 8:(i + 1) * 8, :], d == 1)
                 for i in range(n_ct)]
        h = jnp.zeros((1, gw), F32)
        for i in (range(n_ct) if d == 0 else range(n_ct - 1, -1, -1)):
            a, u = tiles[i]
            h = a[7:8] * h + u[7:8] if d == 0 else a[0:1] * h + u[0:1]
        return h

    hf = ctx_final(0)
    hb = ctx_final(1)
    xp_s[0:gr, :] = _shift_rows(xp_s[pad + (n_tr - 2) * gr:pad + (n_tr - 1) * gr, :], 1)
    xp_s[gr:pad, :] = _shift_rows(xp_s[pad + (n_tr - 1) * gr:pad + n_tr * gr, :], 1)
    proj_x(0)
    proj_g(0)
    xp_s[pad + n_tr * gr:pad + (n_tr + 1) * gr, :] = _shift_rows(xp_s[pad:pad + gr, :], -1)

    def pipe_body(k):
        proj_x(k + 1)
        conv_block(k)
        gates(tc + lat_r0(k), 0, True)
        proj_g(k + 1)
        gates(tc + lat_r0(k), 1, True)

    for k in range(nb - 2):
        pipe_body(k)
    for i in (nb - 2, nb - 1):
        conv_block(i)
        gates(tc + lat_r0(i), 0, True)
        gates(tc + lat_r0(i), 1, True)

    def column_carries(e_tot, p_tot, h0, reverse):
        rid = lax.broadcasted_iota(jnp.int32, (8, gw), 0)
        order = range(gr // 8 - 1, -1, -1) if reverse else range(gr // 8)
        carry, out = h0, [None] * (gr // 8)
        for i in order:
            a_t, u_t = _tile_scan(p_tot[i * 8:(i + 1) * 8, :], e_tot[i * 8:(i + 1) * 8, :], reverse)
            h = u_t + a_t * carry
            if reverse:
                out[i] = jnp.where(rid == 7, carry, pltpu.roll(h, 7, 0))
                carry = h[0:1]
            else:
                out[i] = jnp.where(rid == 0, carry, pltpu.roll(h, 1, 0))
                carry = h[7:8]
        return jnp.concatenate(out, axis=0)

    h = column_carries(tot["H"], tot["A"], hf, False)
    for r in range(n_tr):
        rows = slice(tc + r * gr, tc + (r + 1) * gr)
        h = af_s[rows, :] * h + uf_s[rows, :]
        uf_s[rows, :] = h
    h = column_carries(tot["E"], tot["G"], hb, True)
    for r in range(n_tr - 1, -1, -1):
        rows = slice(tc + r * gr, tc + (r + 1) * gr)
        h = ab_s[rows, :] * h + ub_s[rows, :]
        gate = sg_s[r * gr:(r + 1) * gr, :].astype(F32)
        y_ref[0, r * gr:(r + 1) * gr, :] = ((uf_s[rows, :] + h) * gate).astype(BF16)


def _rglru(hx, hc, w_in, conv_w, conv_b, w_a, b_a, w_x, b_x, lam):
    bsz, t, d = hx.shape
    tc = hc.shape[1]
    e = w_in.shape[1] // 2
    gw = RG_BLOCK
    nb = e // gw
    n_rows = tc + t
    seq_f32 = pltpu.VMEM((n_rows, gw), F32)
    vec = lambda rows: pl.BlockSpec((rows, gw), lambda b, j: (0, j))
    gate_w = pl.BlockSpec((2, 1, gw, gw), lambda b, j: (0, j, 0, 0))
    return pl.pallas_call(
        functools.partial(_rglru_kernel, tc=tc, t=t),
        grid=(bsz, nb),
        in_specs=[
            pl.BlockSpec((1, t, d), lambda b, j: (b, 0, 0)),
            pl.BlockSpec((1, tc, d), lambda b, j: (b, 0, 0)),
            pl.BlockSpec((d, gw), lambda b, j: (0, j)),
            pl.BlockSpec((d, gw), lambda b, j: (0, nb + j)),
            vec(conv_w.shape[0]), vec(1),
            gate_w, gate_w, vec(2), vec(2), vec(2),
        ],
        out_specs=pl.BlockSpec((1, t, gw), lambda b, j: (b, 0, j)),
        out_shape=jax.ShapeDtypeStruct((bsz, t, e), BF16),
        scratch_shapes=[
            pltpu.VMEM((t + 3 * GRID_W, gw), F32),
            seq_f32,
            pltpu.VMEM((t, gw), BF16),
            seq_f32, seq_f32, seq_f32, seq_f32,
        ],
        compiler_params=pltpu.CompilerParams(
            dimension_semantics=("arbitrary", "arbitrary"), vmem_limit_bytes=VMEM_LIMIT),
        name="rglru",
    )(hx, hc, w_in, w_in, conv_w, conv_b.reshape(1, e), w_a, w_x, b_a, b_x, lam)


def kernel(x, c, ctx, c_ctx, ada_w, ada_b, norm_g, final_norm_g, hg_w_in, hg_lower_bounds,
           hg_norm_g, hg_w_out, rg_w_in, rg_conv_w, rg_conv_b, rg_w_a, rg_b_a, rg_w_x, rg_b_x,
           rg_lambda, rg_w_out):
    bsz, t, d = x.shape
    assert ada_w.shape[0] == 2 and bsz < MOD_ROWS and t % GRID_W == 0
    cond = jnp.concatenate(
        [c, c_ctx[None, :], jnp.zeros((MOD_ROWS - bsz - 1, d), F32)], axis=0)
    mod = _ada(cond, ada_w, ada_b)

    hx, hc = _prenorm(x, ctx, mod[0], norm_g[0])
    e = hg_w_in.shape[-1] // 5
    halves = jnp.concatenate([jnp.full((e,), 0.5, F32), jnp.ones((e,), F32), jnp.full((3 * e,), 0.5, F32)])
    w_hg = (hg_w_in[0] * halves).astype(BF16)
    ogx, ogc, half_decay = _hgrn(hx, hc, w_hg, hg_lower_bounds, hg_norm_g[0], layer=0)
    ogx, ogc = lax.cond(
        jnp.all(half_decay < HG_MAX_HALF_DECAY),
        lambda: (ogx, ogc),
        lambda: tuple(_hgrn_exact(hx, hc, w_hg, hg_lower_bounds, hg_norm_g[0], layer=0)))
    x1, hx1, hc1 = _outproj_mid(ogx, ogc, x, ctx, hg_w_out[0].astype(BF16), mod[0], mod[1], norm_g[1])
    er = rg_w_in.shape[-1] // 2
    rg_halves = jnp.concatenate([jnp.ones((er,), F32), jnp.full((er,), 0.5, F32)])
    yg = _rglru(hx1, hc1, (rg_w_in[0] * rg_halves).astype(BF16), rg_conv_w[0], rg_conv_b[0],
                rg_w_a[0].astype(BF16), rg_b_a[0], rg_w_x[0].astype(BF16), rg_b_x[0], rg_lambda[0])
    return _outproj_final(yg, x1, rg_w_out[0].astype(BF16), mod[1], final_norm_g)
```

```python
import functools

import jax
import jax.numpy as jnp
from jax import lax
from jax.experimental import pallas as pl
from jax.experimental.pallas import tpu as pltpu

F32 = jnp.float32
BF16 = jnp.bfloat16

EPS = 1e-6
F32_TINY = 1.1754944e-38
GRID_W = 64
HG_KEY_DIM = 128
HG_CHUNK = 64
HG_MAX_HALF_DECAY = 80.0
RG_BLOCK = 256
RG_C = 8.0
MOD_ROWS = 16
ROW_BLOCK = 256
GATE_ROWS = 64
HG_BLOCK = 512
VMEM_LIMIT = 56 * 1024 * 1024

_NT = (((1,), (1,)), ((), ()))
_TN = (((0,), (0,)), ((), ()))


def _silu(z):
    return _silu_half(0.5 * z)


def _silu_half(hz):
    return hz + hz * jnp.tanh(hz)


def _dot(a, b):
    return jnp.dot(a, b, preferred_element_type=F32)


def _ada_kernel(c_ref, w_ref, b_ref, o_ref):
    s = _silu(c_ref[...])
    w = w_ref[0]
    s_hi = s.astype(BF16)
    s_lo = (s - s_hi.astype(F32)).astype(BF16)
    w_hi = w.astype(BF16)
    w_lo = (w - w_hi.astype(F32)).astype(BF16)
    both = _dot(jnp.concatenate([s_hi, s_lo], axis=0), w_hi)
    o_ref[0] = both[:MOD_ROWS] + both[MOD_ROWS:] + _dot(s_hi, w_lo) + b_ref[0]


def _ada(cond, ada_w, ada_b):
    depth, d, n3 = ada_w.shape
    bn = d
    return pl.pallas_call(
        _ada_kernel,
        grid=(depth, n3 // bn),
        in_specs=[
            pl.BlockSpec((MOD_ROWS, d), lambda i, n: (0, 0)),
            pl.BlockSpec((1, d, bn), lambda i, n: (i, 0, n)),
            pl.BlockSpec((1, 1, bn), lambda i, n: (i, 0, n)),
        ],
        out_specs=pl.BlockSpec((1, MOD_ROWS, bn), lambda i, n: (i, 0, n)),
        out_shape=jax.ShapeDtypeStruct((depth, MOD_ROWS, n3), F32),
        name="ada",
    )(cond, ada_w, ada_b.reshape(depth, 1, n3))


def _norm_mod(xv, g, shift, scale):
    ms = jnp.mean(xv * xv, axis=-1, keepdims=True)
    return xv * lax.rsqrt(ms + EPS) * g * (1.0 + scale) + shift


def _prenorm_kernel(x_ref, ctx_ref, mod_ref, g_ref, hx_ref, hc_ref, *, d, ctx_row):
    b = pl.program_id(0)
    g = g_ref[...]
    row = mod_ref[pl.ds(b, 1), :]
    hx_ref[0] = _norm_mod(x_ref[0], g, row[:, :d], row[:, d:2 * d]).astype(BF16)

    @pl.when(pl.program_id(1) == 0)
    def _():
        rowc = mod_ref[ctx_row:ctx_row + 1, :]
        hc_ref[0] = _norm_mod(ctx_ref[0], g, rowc[:, :d], rowc[:, d:2 * d]).astype(BF16)


def _prenorm(x, ctx, mod, g):
    bsz, t, d = x.shape
    tc = ctx.shape[1]
    rb = t
    return pl.pallas_call(
        functools.partial(_prenorm_kernel, d=d, ctx_row=bsz),
        grid=(bsz, t // rb),
        in_specs=[
            pl.BlockSpec((1, rb, d), lambda b, r: (b, r, 0)),
            pl.BlockSpec((1, tc, d), lambda b, r: (b, 0, 0)),
            pl.BlockSpec((MOD_ROWS, 3 * d), lambda b, r: (0, 0)),
            pl.BlockSpec((1, d), lambda b, r: (0, 0)),
        ],
        out_specs=[
            pl.BlockSpec((1, rb, d), lambda b, r: (b, r, 0)),
            pl.BlockSpec((1, tc, d), lambda b, r: (b, 0, 0)),
        ],
        out_shape=[jax.ShapeDtypeStruct((bsz, t, d), BF16),
                   jax.ShapeDtypeStruct((bsz, tc, d), BF16)],
        compiler_params=pltpu.CompilerParams(
            dimension_semantics=("arbitrary", "arbitrary"), vmem_limit_bytes=VMEM_LIMIT),
        name="prenorm",
    )(x, ctx, mod, g.reshape(1, d))


def _hgrn_kernel(hx_ref, hc_ref, wq_ref, wv_ref, wf_ref, wb_ref, wg_ref, lbp_ref, ng_ref, tri_ref,
                 ox_ref, oc_ref, decay_ref,
                 qa_b, va_b, lffa_b, lfba_b, kfa_b, kba_b, qb_b, vb_b, lffb_b, lfbb_b, kfb_b, kbb_b,
                 qtf_s, qtb_s, sg_s, o_s, oif_s, oib_s, ut_s, em_s,
                 *, layer, tc, t, heads):
    n_rows = tc + t
    c = HG_CHUNK
    kd = HG_KEY_DIM
    qt_s = (qtf_s, qtb_s)
    oi_s = (oif_s, oib_s)

    lbp = lbp_ref[...]
    ex = jnp.exp(lbp - jnp.max(lbp, axis=0, keepdims=True))
    lb = jnp.sum(ex[:layer + 1], axis=0, keepdims=True) / jnp.sum(ex, axis=0, keepdims=True)
    f_mid = 0.5 * (1.0 + lb)
    f_half = 0.5 * (1.0 - lb)

    rid = lax.broadcasted_iota(jnp.int32, (c, c), 0)
    cid = lax.broadcasted_iota(jnp.int32, (c, c), 1)
    masks = (rid >= cid, rid <= cid)

    def split3(xv):
        hi = xv.astype(BF16)
        r1 = xv - hi.astype(F32)
        mid = r1.astype(BF16)
        lo = (r1 - mid.astype(F32)).astype(BF16)
        return jnp.concatenate([hi, mid, lo], axis=0)

    def proj_steps(hrows, buf, r0):
        nr = hrows.shape[0]
        q_b, v_b, lf_b, k_b = buf

        def step_q():
            q_b[0:nr, :] = _silu_half(_dot(hrows, wq_ref[...]))

        def step_v():
            v_b[0:nr, :] = _dot(hrows, wv_ref[...]).astype(BF16)

        def step_f(d):
            f = f_mid + f_half * jnp.tanh(_dot(hrows, (wf_ref, wb_ref)[d][...]))
            lf_b[d][0:nr, :] = jnp.log(f)
            k_b[d][0:nr, :] = 1.0 - f

        def step_g():
            sg_s[pl.ds(r0, nr), :] = _silu_half(_dot(hrows, wg_ref[...])).astype(BF16)

        return [step_q, step_v, functools.partial(step_f, 0), functools.partial(step_f, 1), step_g]

    def chunk_steps(buf, nr, r0, c0):
        q_b, v_b, lf_b, k_b = buf
        crs = [slice(cc * c, (cc + 1) * c) for cc in range(nr // c)]
        vals = {}

        def step_cumsum():
            vals["b"] = [[_dot(tri_ref[d], split3(lf_b[d][cr, :])) for d in range(2)] for cr in crs]

        def step_scale():
            qts, kts = [], []
            for cc, cr in enumerate(crs):
                qv = q_b[cr, :]
                qt, kt = [], []
                for d in range(2):
                    bsum = vals["b"][cc][d]
                    m = 0.5 * (bsum[c - 1:c, :] if d == 0 else bsum[0:1, :])
                    half_decay[0] = jnp.maximum(half_decay[0], jnp.abs(m))
                    em_s[d, pl.ds(c0 + cc, 1), :] = jnp.exp(m)
                    qt.append((qv * jnp.exp(bsum - m)).astype(BF16))
                    kt.append((k_b[d][cr, :] * jnp.exp(m - bsum)).astype(BF16))
                    qt_s[d][pl.ds(r0 + cc * c, c), :] = qt[d]
                qts.append(qt)
                kts.append(kt)
            vals["qt"], vals["kt"] = qts, kts

        def step_scores():
            scs = []
            for cc in range(len(crs)):
                for h in range(heads):
                    cs = slice(h * kd, (h + 1) * kd)
                    qt, kt = vals["qt"][cc], vals["kt"][cc]
                    sf = lax.dot_general(qt[0][:, cs], kt[0][:, cs], _NT, preferred_element_type=F32)
                    sb = lax.dot_general(qt[1][:, cs], kt[1][:, cs], _NT, preferred_element_type=F32)
                    scs.append((jnp.where(masks[0], sf, 0.0) + jnp.where(masks[1], sb, 0.0)).astype(BF16))
            vals["sc"] = scs

        def step_out():
            for cc, cr in enumerate(crs):
                for h in range(heads):
                    cs = slice(h * kd, (h + 1) * kd)
                    vv = v_b[cr, cs]
                    kt = vals["kt"][cc]
                    o_s[pl.ds(r0 + cc * c, c), cs] = _dot(vals["sc"][cc * heads + h], vv)
                    kk = jnp.concatenate([kt[0][:, cs], kt[1][:, cs]], axis=1)
                    ut_s[c0 + cc, h] = lax.dot_general(vv, kk, _TN, preferred_element_type=F32)

        return [step_cumsum, step_scale, step_scores, step_out]

    half_decay = [jnp.zeros((1, heads * kd), F32)]

    bufs = ((qa_b, va_b, (lffa_b, lfba_b), (kfa_b, kba_b)),
            (qb_b, vb_b, (lffb_b, lfbb_b), (kfb_b, kbb_b)))
    blocks = [(hc_ref[0], 0, tc)]
    blocks += [(hx_ref[0, i * HG_BLOCK:(i + 1) * HG_BLOCK, :], tc + i * HG_BLOCK, HG_BLOCK)
               for i in range(t // HG_BLOCK)]
    units = [(d, h) for d in range(2) for h in range(heads)]
    st = {u: jnp.zeros((kd, kd), F32) for u in units}

    def scan_step(d, ci):
        rows = slice(ci * c, (ci + 1) * c)
        for h in range(heads):
            cs = slice(h * kd, (h + 1) * kd)
            em = em_s[d, ci:ci + 1, cs]
            sm = st[(d, h)] * em
            st[(d, h)] = em * (sm + ut_s[ci, h][:, d * kd:(d + 1) * kd])
            oi_s[d][rows, cs] = lax.dot_general(
                qt_s[d][rows, cs], sm.astype(BF16), _NT, preferred_element_type=F32)

    prev, fwd_now, fwd_next = [], [], []
    for n, (hrows, r0, nr) in enumerate(blocks):
        proj = proj_steps(hrows, bufs[n % 2], r0)
        share = -(-len(fwd_now) // len(proj))
        for k, step in enumerate(proj):
            step()
            if k < len(prev):
                prev[k]()
            for f in fwd_now[k * share:(k + 1) * share]:
                f()
        fwd_now = fwd_next
        fwd_next = [functools.partial(scan_step, 0, ci) for ci in range(r0 // c, (r0 + nr) // c)]
        prev = chunk_steps(bufs[n % 2], nr, r0, r0 // c)
    for f in fwd_now:
        f()
    for step in prev:
        step()
    for f in fwd_next:
        f()
    decay_ref[0, 0] = half_decay[0]

    ng = ng_ref[...]

    def readout(r0):
        rows = slice(r0, r0 + ROW_BLOCK)
        o = o_s[rows, :] + oif_s[rows, :] + oib_s[rows, :]
        parts = []
        for h in range(heads):
            cs = slice(h * kd, (h + 1) * kd)
            oh = o[:, cs]
            ms = jnp.mean(oh * oh, axis=-1, keepdims=True)
            parts.append(oh * lax.rsqrt(ms + EPS) * ng[:, cs])
        y = (jnp.concatenate(parts, axis=-1) * sg_s[rows, :].astype(F32)).astype(BF16)
        if r0 < tc:
            oc_ref[0, r0:r0 + ROW_BLOCK, :] = y
        else:
            ox_ref[0, r0 - tc:r0 - tc + ROW_BLOCK, :] = y

    n_ctx = tc // c
    n_all = n_rows // c
    cpb = ROW_BLOCK // c
    order = list(range(n_ctx - 1, -1, -1)) + list(range(n_all - 1, n_ctx - 1, -1))

    @pl.when(pl.program_id(0) >= 0)
    def _():
        for ci in order:
            scan_step(1, ci)
            if ci % cpb == 0:
                readout(ci * c)


def _hgrn(hx, hc, w_in, lower_bounds, norm_g, layer):
    bsz, t, d = hx.shape
    tc = hc.shape[1]
    e = w_in.shape[1] // 5
    heads = 2
    gw = heads * HG_KEY_DIM
    ng_blocks = e // gw
    n_rows = tc + t
    c = HG_CHUNK

    low = jnp.tril(jnp.ones((c, c), F32))
    tri = jnp.stack([jnp.tile(low, (1, 3)), jnp.tile(low.T, (1, 3))]).astype(BF16)

    def wspec(sec):
        return pl.BlockSpec((d, gw), lambda b, j, sec=sec: (0, sec * ng_blocks + j))

    seq_f32 = pltpu.VMEM((n_rows, gw), F32)
    seq_bf16 = pltpu.VMEM((n_rows, gw), BF16)
    blk_f32 = pltpu.VMEM((HG_BLOCK, gw), F32)
    blk_bf16 = pltpu.VMEM((HG_BLOCK, gw), BF16)
    n_chunks = n_rows // c
    return pl.pallas_call(
        functools.partial(_hgrn_kernel, layer=layer, tc=tc, t=t, heads=heads),
        grid=(bsz, ng_blocks),
        in_specs=[
            pl.BlockSpec((1, t, d), lambda b, j: (b, 0, 0)),
            pl.BlockSpec((1, tc, d), lambda b, j: (b, 0, 0)),
            wspec(0), wspec(1), wspec(2), wspec(3), wspec(4),
            pl.BlockSpec((lower_bounds.shape[0], gw), lambda b, j: (0, j)),
            pl.BlockSpec((1, gw), lambda b, j: (0, j)),
            pl.BlockSpec((2, c, 3 * c), lambda b, j: (0, 0, 0)),
        ],
        out_specs=[
            pl.BlockSpec((1, t, gw), lambda b, j: (b, 0, j)),
            pl.BlockSpec((1, tc, gw), lambda b, j: (b, 0, j)),
            pl.BlockSpec((1, 1, 1, gw), lambda b, j: (b, j, 0, 0)),
        ],
        out_shape=[jax.ShapeDtypeStruct((bsz, t, e), BF16),
                   jax.ShapeDtypeStruct((bsz, tc, e), BF16),
                   jax.ShapeDtypeStruct((bsz, ng_blocks, 1, gw), F32)],
        scratch_shapes=[
            blk_f32, blk_bf16, blk_f32, blk_f32, blk_f32, blk_f32,
            blk_f32, blk_bf16, blk_f32, blk_f32, blk_f32, blk_f32,
            seq_bf16, seq_bf16,
            seq_bf16,
            seq_f32, seq_f32, seq_f32,
            pltpu.VMEM((n_chunks, heads, HG_KEY_DIM, 2 * HG_KEY_DIM), F32),
            pltpu.VMEM((2, n_chunks, gw), F32),
        ],
        compiler_params=pltpu.CompilerParams(
            dimension_semantics=("arbitrary", "arbitrary"), vmem_limit_bytes=VMEM_LIMIT),
        name="hgrn2",
    )(hx, hc, w_in, w_in, w_in, w_in, w_in, lower_bounds, norm_g.reshape(1, e), tri)


def _hgrn_exact_kernel(hx_ref, hc_ref, wq_ref, wv_ref, wf_ref, wb_ref, wg_ref, lbp_ref, ng_ref,
                       ox_ref, oc_ref, q_s, v_s, ff_s, fb_s, sg_s, o_s, st_s,
                       *, layer, tc, t, heads):
    n_rows = tc + t
    kd = HG_KEY_DIM
    f_refs = (ff_s, fb_s)

    lbp = lbp_ref[...]
    ex = jnp.exp(lbp - jnp.max(lbp, axis=0, keepdims=True))
    lb = jnp.sum(ex[:layer + 1], axis=0, keepdims=True) / jnp.sum(ex, axis=0, keepdims=True)
    f_mid = 0.5 * (1.0 + lb)
    f_half = 0.5 * (1.0 - lb)

    def project(hrows, r0):
        rows = pl.ds(r0, ROW_BLOCK)
        q_s[rows, :] = _silu_half(_dot(hrows, wq_ref[...]))
        v_s[rows, :] = _dot(hrows, wv_ref[...])
        ff_s[rows, :] = f_mid + f_half * jnp.tanh(_dot(hrows, wf_ref[...]))
        fb_s[rows, :] = f_mid + f_half * jnp.tanh(_dot(hrows, wb_ref[...]))
        sg_s[rows, :] = _silu_half(_dot(hrows, wg_ref[...]))

    for i in range(tc // ROW_BLOCK):
        project(hc_ref[0, i * ROW_BLOCK:(i + 1) * ROW_BLOCK, :], i * ROW_BLOCK)

    def proj_body(i, carry):
        r0 = pl.multiple_of(i * ROW_BLOCK, ROW_BLOCK)
        project(hx_ref[0, pl.ds(r0, ROW_BLOCK), :], tc + r0)
        return carry

    lax.fori_loop(0, t // ROW_BLOCK, proj_body, 0)

    o_s[...] = jnp.zeros_like(o_s)
    st_s[...] = jnp.zeros_like(st_s)
    rid = lax.broadcasted_iota(jnp.int32, (8, kd), 0)
    n_ct = tc // 8
    n_tiles = n_rows // 8

    def tile_body(i, carry):
        tiles = (i, jnp.where(i < n_ct, n_ct - 1 - i, n_tiles + n_ct - 1 - i))
        for d in range(2):
            rows = pl.ds(pl.multiple_of(tiles[d] * 8, 8), 8)
            for h in range(heads):
                cs = slice(h * kd, (h + 1) * kd)
                f8, v8, q8 = f_refs[d][rows, cs], v_s[rows, cs], q_s[rows, cs]
                k8 = 1.0 - f8
                s = st_s[d, h]
                acc = jnp.zeros((8, kd), F32)
                for r in (range(8) if d == 0 else range(7, -1, -1)):
                    sel = rid == r
                    outer = lax.dot_general(jnp.where(sel, v8, 0.0), k8, _TN, preferred_element_type=F32)
                    s = s * f8[r:r + 1, :] + outer
                    acc = acc + lax.dot_general(jnp.where(sel, q8, 0.0), s, _NT,
                                                preferred_element_type=F32)
                st_s[d, h] = s
                o_s[rows, cs] = o_s[rows, cs] + acc
        return carry

    lax.fori_loop(0, n_tiles, tile_body, 0)

    ng = ng_ref[...]

    def readout(r0):
        rows = pl.ds(r0, ROW_BLOCK)
        o = o_s[rows, :]
        parts = []
        for h in range(heads):
            cs = slice(h * kd, (h + 1) * kd)
            oh = o[:, cs]
            ms = jnp.mean(oh * oh, axis=-1, keepdims=True)
            parts.append(oh * lax.rsqrt(ms + EPS) * ng[:, cs])
        return (jnp.concatenate(parts, axis=-1) * sg_s[rows, :]).astype(BF16)

    for i in range(tc // ROW_BLOCK):
        oc_ref[0, i * ROW_BLOCK:(i + 1) * ROW_BLOCK, :] = readout(i * ROW_BLOCK)

    def out_body(i, carry):
        r0 = pl.multiple_of(i * ROW_BLOCK, ROW_BLOCK)
        ox_ref[0, pl.ds(r0, ROW_BLOCK), :] = readout(tc + r0)
        return carry

    lax.fori_loop(0, t // ROW_BLOCK, out_body, 0)


def _hgrn_exact(hx, hc, w_in, lower_bounds, norm_g, layer):
    bsz, t, d = hx.shape
    tc = hc.shape[1]
    e = w_in.shape[1] // 5
    heads = 2
    gw = heads * HG_KEY_DIM
    ng_blocks = e // gw
    seq_f32 = pltpu.VMEM((tc + t, gw), F32)

    def wspec(sec):
        return pl.BlockSpec((d, gw), lambda b, j, sec=sec: (0, sec * ng_blocks + j))

    return pl.pallas_call(
        functools.partial(_hgrn_exact_kernel, layer=layer, tc=tc, t=t, heads=heads),
        grid=(bsz, ng_blocks),
        in_specs=[
            pl.BlockSpec((1, t, d), lambda b, j: (b, 0, 0)),
            pl.BlockSpec((1, tc, d), lambda b, j: (b, 0, 0)),
            wspec(0), wspec(1), wspec(2), wspec(3), wspec(4),
            pl.BlockSpec((lower_bounds.shape[0], gw), lambda b, j: (0, j)),
            pl.BlockSpec((1, gw), lambda b, j: (0, j)),
        ],
        out_specs=[
            pl.BlockSpec((1, t, gw), lambda b, j: (b, 0, j)),
            pl.BlockSpec((1, tc, gw), lambda b, j: (b, 0, j)),
        ],
        out_shape=[jax.ShapeDtypeStruct((bsz, t, e), BF16),
                   jax.ShapeDtypeStruct((bsz, tc, e), BF16)],
        scratch_shapes=[
            seq_f32, seq_f32, seq_f32, seq_f32, seq_f32,
            seq_f32,
            pltpu.VMEM((2, heads, HG_KEY_DIM, HG_KEY_DIM), F32),
        ],
        compiler_params=pltpu.CompilerParams(
            dimension_semantics=("arbitrary", "arbitrary"), vmem_limit_bytes=VMEM_LIMIT),
        name="hgrn2_exact",
    )(hx, hc, w_in, w_in, w_in, w_in, w_in, lower_bounds, norm_g.reshape(1, e))


def _outproj_mid_kernel(ogx_ref, ogc_ref, x_ref, ctx_ref, w_ref, mod0_ref, mod1_ref, g1_ref,
                        x1_ref, hx_ref, hc_ref, *, d, ctx_row):
    b = pl.program_id(0)
    g1 = g1_ref[...]
    row0 = mod0_ref[pl.ds(b, 1), :]
    row1 = mod1_ref[pl.ds(b, 1), :]
    x1 = x_ref[0] + row0[:, 2 * d:] * _dot(ogx_ref[0], w_ref[...])
    x1_ref[0] = x1
    hx_ref[0] = _norm_mod(x1, g1, row1[:, :d], row1[:, d:2 * d]).astype(BF16)

    @pl.when(pl.program_id(1) == 0)
    def _():
        c0 = mod0_ref[ctx_row:ctx_row + 1, :]
        c1 = mod1_ref[ctx_row:ctx_row + 1, :]
        ctx1 = ctx_ref[0] + c0[:, 2 * d:] * _dot(ogc_ref[0], w_ref[...])
        hc_ref[0] = _norm_mod(ctx1, g1, c1[:, :d], c1[:, d:2 * d]).astype(BF16)


def _outproj_mid(ogx, ogc, x, ctx, w_out, mod0, mod1, g1):
    bsz, t, d = x.shape
    tc = ctx.shape[1]
    e = w_out.shape[0]
    rb = 1024
    return pl.pallas_call(
        functools.partial(_outproj_mid_kernel, d=d, ctx_row=bsz),
        grid=(bsz, t // rb),
        in_specs=[
            pl.BlockSpec((1, rb, e), lambda b, r: (b, r, 0)),
            pl.BlockSpec((1, tc, e), lambda b, r: (b, 0, 0)),
            pl.BlockSpec((1, rb, d), lambda b, r: (b, r, 0)),
            pl.BlockSpec((1, tc, d), lambda b, r: (b, 0, 0)),
            pl.BlockSpec((e, d), lambda b, r: (0, 0)),
            pl.BlockSpec((MOD_ROWS, 3 * d), lambda b, r: (0, 0)),
            pl.BlockSpec((MOD_ROWS, 3 * d), lambda b, r: (0, 0)),
            pl.BlockSpec((1, d), lambda b, r: (0, 0)),
        ],
        out_specs=[
            pl.BlockSpec((1, rb, d), lambda b, r: (b, r, 0)),
            pl.BlockSpec((1, rb, d), lambda b, r: (b, r, 0)),
            pl.BlockSpec((1, tc, d), lambda b, r: (b, 0, 0)),
        ],
        out_shape=[jax.ShapeDtypeStruct((bsz, t, d), F32),
                   jax.ShapeDtypeStruct((bsz, t, d), BF16),
                   jax.ShapeDtypeStruct((bsz, tc, d), BF16)],
        compiler_params=pltpu.CompilerParams(
            dimension_semantics=("arbitrary", "arbitrary"), vmem_limit_bytes=VMEM_LIMIT),
        name="outproj0",
    )(ogx, ogc, x, ctx, w_out, mod0, mod1, g1.reshape(1, d))


def _outproj_final_kernel(yg_ref, x_ref, w_ref, mod_ref, g_ref, o_ref, *, d):
    b = pl.program_id(0)
    row = mod_ref[pl.ds(b, 1), :]
    x2 = x_ref[0] + row[:, 2 * d:] * _dot(yg_ref[0], w_ref[...])
    ms = jnp.mean(x2 * x2, axis=-1, keepdims=True)
    o_ref[0] = x2 * lax.rsqrt(ms + EPS) * g_ref[...]


def _outproj_final(yg, x1, w_out, mod, g):
    bsz, t, d = x1.shape
    e = w_out.shape[0]
    rb = 1024
    return pl.pallas_call(
        functools.partial(_outproj_final_kernel, d=d),
        grid=(bsz, t // rb),
        in_specs=[
            pl.BlockSpec((1, rb, e), lambda b, r: (b, r, 0)),
            pl.BlockSpec((1, rb, d), lambda b, r: (b, r, 0)),
            pl.BlockSpec((e, d), lambda b, r: (0, 0)),
            pl.BlockSpec((MOD_ROWS, 3 * d), lambda b, r: (0, 0)),
            pl.BlockSpec((1, d), lambda b, r: (0, 0)),
        ],
        out_specs=pl.BlockSpec((1, rb, d), lambda b, r: (b, r, 0)),
        out_shape=jax.ShapeDtypeStruct((bsz, t, d), F32),
        compiler_params=pltpu.CompilerParams(
            dimension_semantics=("arbitrary", "arbitrary"), vmem_limit_bytes=VMEM_LIMIT),
        name="outproj1",
    )(yg, x1, w_out, mod, g.reshape(1, d))


def _shift_rows(xv, k):
    n = xv.shape[0]
    rid = lax.broadcasted_iota(jnp.int32, xv.shape, 0)
    rolled = pltpu.roll(xv, k % n, 0)
    valid = (rid >= k) if k > 0 else (rid < n + k)
    return jnp.where(valid, rolled, 0.0)


def _tile_scan(a, u, reverse):
    n = a.shape[0]
    rid = lax.broadcasted_iota(jnp.int32, a.shape, 0)
    s = 1
    while s < n:
        valid = (rid < n - s) if reverse else (rid >= s)
        sh = (n - s) if reverse else s
        a_sh = jnp.where(valid, pltpu.roll(a, sh, 0), 1.0)
        u_sh = jnp.where(valid, pltpu.roll(u, sh, 0), 0.0)
        u = u + a * u_sh
        a = a * a_sh
        s *= 2
    return a, u


def _rglru_kernel(hx_ref, hc_ref, wx_ref, wg_ref, cw_ref, cb_ref, wa_ref, wi_ref, ba_ref, bi_ref,
                  lam_ref, y_ref,
                  xp_s, xc_s, sg_s, af_s, uf_s, ab_s, ub_s,
                  *, tc, t):
    gw = RG_BLOCK
    gr = GRID_W
    n_tr = t // gr
    a_refs = (af_s, ab_s)
    u_refs = (uf_s, ub_s)
    cw = 0.5 * cw_ref[...]
    cb = 0.5 * cb_ref[...]

    def conv_taps(xm2, xm1, x0, xp1):
        return cw[0:1] * xm2 + cw[1:2] * xm1 + cw[2:3] * x0 + cw[3:4] * xp1 + cb

    xcx = _dot(hc_ref[0], wx_ref[...])
    xc_s[0:tc, :] = conv_taps(_shift_rows(xcx, 2), _shift_rows(xcx, 1), xcx, _shift_rows(xcx, -1))

    pad = 2 * gr
    nb = t // ROW_BLOCK
    hsp = (0.5 * RG_C) * jax.nn.softplus(-lam_ref[...])
    hba = 0.5 * ba_ref[...]
    hbi = 0.5 * bi_ref[...]

    def lat_r0(i):
        return i * ROW_BLOCK if isinstance(i, int) else pl.multiple_of(i * ROW_BLOCK, ROW_BLOCK)

    def proj_x(i):
        r0 = lat_r0(i)
        xp_s[pl.ds(pad + r0, ROW_BLOCK), :] = _dot(hx_ref[0, pl.ds(r0, ROW_BLOCK), :], wx_ref[...])

    def proj_g(i):
        r0 = lat_r0(i)
        sg_s[pl.ds(r0, ROW_BLOCK), :] = _silu_half(
            _dot(hx_ref[0, pl.ds(r0, ROW_BLOCK), :], wg_ref[...])).astype(BF16)

    def conv_block(i):
        for k in range(ROW_BLOCK // gr):
            r0 = lat_r0(i) + k * gr
            xc_s[pl.ds(tc + r0, gr), :] = conv_taps(
                xp_s[pl.ds(r0, gr), :], xp_s[pl.ds(r0 + gr, gr), :],
                xp_s[pl.ds(r0 + 2 * gr, gr), :], xp_s[pl.ds(r0 + 3 * gr, gr), :])

    tot = {"H": jnp.zeros((gr, gw), F32), "A": jnp.ones((gr, gw), F32),
           "E": jnp.zeros((gr, gw), F32), "G": jnp.ones((gr, gw), F32)}

    def gates(r0, d, latent):
        for part in range(ROW_BLOCK // GATE_ROWS):
            gates_part(r0 + part * GATE_ROWS, d, latent)

    def gates_part(r0, d, latent):
        rows = pl.ds(r0, GATE_ROWS)
        hxc = xc_s[rows, :]
        xcb = hxc.astype(BF16)
        tr = jnp.tanh(_dot(xcb, wa_ref[d, 0]) + hba[d:d + 1, :])
        ti = jnp.tanh(_dot(xcb, wi_ref[d, 0]) + hbi[d:d + 1, :])
        nla = hsp[d:d + 1, :] + hsp[d:d + 1, :] * tr
        a = jnp.exp(-nla)
        a_refs[d][rows, :] = a
        z = jnp.tanh(nla) * (a * a + 1.0)
        u = (z * lax.rsqrt(jnp.maximum(z, F32_TINY))) * (hxc + hxc * ti)
        u_refs[d][rows, :] = u
        if not latent:
            return
        for k in range(GATE_ROWS // gr):
            ak, uk = a[k * gr:(k + 1) * gr, :], u[k * gr:(k + 1) * gr, :]
            if d == 0:
                tot["H"] = ak * tot["H"] + uk
                tot["A"] = ak * tot["A"]
            else:
                tot["E"] = tot["E"] + tot["G"] * uk
                tot["G"] = tot["G"] * ak

    proj_x(nb - 1)
    for i in range(tc // ROW_BLOCK):
        gates(i * ROW_BLOCK, 0, False)
    proj_g(nb - 1)
    for i in range(tc // ROW_BLOCK):
        gates(i * ROW_BLOCK, 1, False)
    def ctx_final(d):
        n_ct = tc // 8
        tiles = [_tile_scan(a_refs[d][i * 8:(i + 1) * 8, :], u_refs[d][i * 8:(i + 1) * 8, :], d == 1)
                 for i in range(n_ct)]
        h = jnp.zeros((1, gw), F32)
        for i in (range(n_ct) if d == 0 else range(n_ct - 1, -1, -1)):
            a, u = tiles[i]
            h = a[7:8] * h + u[7:8] if d == 0 else a[0:1] * h + u[0:1]
        return h

    hf = ctx_final(0)
    hb = ctx_final(1)
    xp_s[0:gr, :] = _shift_rows(xp_s[pad + (n_tr - 2) * gr:pad + (n_tr - 1) * gr, :], 1)
    xp_s[gr:pad, :] = _shift_rows(xp_s[pad + (n_tr - 1) * gr:pad + n_tr * gr, :], 1)
    proj_x(0)
    proj_g(0)
    xp_s[pad + n_tr * gr:pad + (n_tr + 1) * gr, :] = _shift_rows(xp_s[pad:pad + gr, :], -1)

    def pipe_body(k):
        proj_x(k + 1)
        conv_block(k)
        gates(tc + lat_r0(k), 0, True)
        proj_g(k + 1)
        gates(tc + lat_r0(k), 1, True)

    for k in range(nb - 2):
        pipe_body(k)
    for i in (nb - 2, nb - 1):
        conv_block(i)
        gates(tc + lat_r0(i), 0, True)
        gates(tc + lat_r0(i), 1, True)

    def column_carries(e_tot, p_tot, h0, reverse):
        rid = lax.broadcasted_iota(jnp.int32, (8, gw), 0)
        order = range(gr // 8 - 1, -1, -1) if reverse else range(gr // 8)
        carry, out = h0, [None] * (gr // 8)
        for i in order:
            a_t, u_t = _tile_scan(p_tot[i * 8:(i + 1) * 8, :], e_tot[i * 8:(i + 1) * 8, :], reverse)
            h = u_t + a_t * carry
            if reverse:
                out[i] = jnp.where(rid == 7, carry, pltpu.roll(h, 7, 0))
                carry = h[0:1]
            else:
                out[i] = jnp.where(rid == 0, carry, pltpu.roll(h, 1, 0))
                carry = h[7:8]
        return jnp.concatenate(out, axis=0)

    h = column_carries(tot["H"], tot["A"], hf, False)
    for r in range(n_tr):
        rows = slice(tc + r * gr, tc + (r + 1) * gr)
        h = af_s[rows, :] * h + uf_s[rows, :]
        uf_s[rows, :] = h
    h = column_carries(tot["E"], tot["G"], hb, True)
    for r in range(n_tr - 1, -1, -1):
        rows = slice(tc + r * gr, tc + (r + 1) * gr)
        h = ab_s[rows, :] * h + ub_s[rows, :]
        gate = sg_s[r * gr:(r + 1) * gr, :].astype(F32)
        y_ref[0, r * gr:(r + 1) * gr, :] = ((uf_s[rows, :] + h) * gate).astype(BF16)


def _rglru(hx, hc, w_in, conv_w, conv_b, w_a, b_a, w_x, b_x, lam):
    bsz, t, d = hx.shape
    tc = hc.shape[1]
    e = w_in.shape[1] // 2
    gw = RG_BLOCK
    nb = e // gw
    n_rows = tc + t
    seq_f32 = pltpu.VMEM((n_rows, gw), F32)
    vec = lambda rows: pl.BlockSpec((rows, gw), lambda b, j: (0, j))
    gate_w = pl.BlockSpec((2, 1, gw, gw), lambda b, j: (0, j, 0, 0))
    return pl.pallas_call(
        functools.partial(_rglru_kernel, tc=tc, t=t),
        grid=(bsz, nb),
        in_specs=[
            pl.BlockSpec((1, t, d), lambda b, j: (b, 0, 0)),
            pl.BlockSpec((1, tc, d), lambda b, j: (b, 0, 0)),
            pl.BlockSpec((d, gw), lambda b, j: (0, j)),
            pl.BlockSpec((d, gw), lambda b, j: (0, nb + j)),
            vec(conv_w.shape[0]), vec(1),
            gate_w, gate_w, vec(2), vec(2), vec(2),
        ],
        out_specs=pl.BlockSpec((1, t, gw), lambda b, j: (b, 0, j)),
        out_shape=jax.ShapeDtypeStruct((bsz, t, e), BF16),
        scratch_shapes=[
            pltpu.VMEM((t + 3 * GRID_W, gw), F32),
            seq_f32,
            pltpu.VMEM((t, gw), BF16),
            seq_f32, seq_f32, seq_f32, seq_f32,
        ],
        compiler_params=pltpu.CompilerParams(
            dimension_semantics=("arbitrary", "arbitrary"), vmem_limit_bytes=VMEM_LIMIT),
        name="rglru",
    )(hx, hc, w_in, w_in, conv_w, conv_b.reshape(1, e), w_a, w_x, b_a, b_x, lam)


def kernel(x, c, ctx, c_ctx, ada_w, ada_b, norm_g, final_norm_g, hg_w_in, hg_lower_bounds,
           hg_norm_g, hg_w_out, rg_w_in, rg_conv_w, rg_conv_b, rg_w_a, rg_b_a, rg_w_x, rg_b_x,
           rg_lambda, rg_w_out):
    bsz, t, d = x.shape
    assert ada_w.shape[0] == 2 and bsz < MOD_ROWS and t % GRID_W == 0
    cond = jnp.concatenate(
        [c, c_ctx[None, :], jnp.zeros((MOD_ROWS - bsz - 1, d), F32)], axis=0)
    mod = _ada(cond, ada_w, ada_b)

    hx, hc = _prenorm(x, ctx, mod[0], norm_g[0])
    e = hg_w_in.shape[-1] // 5
    halves = jnp.concatenate([jnp.full((e,), 0.5, F32), jnp.ones((e,), F32), jnp.full((3 * e,), 0.5, F32)])
    w_hg = (hg_w_in[0] * halves).astype(BF16)
    ogx, ogc, half_decay = _hgrn(hx, hc, w_hg, hg_lower_bounds, hg_norm_g[0], layer=0)
    ogx, ogc = lax.cond(
        jnp.all(half_decay < HG_MAX_HALF_DECAY),
        lambda: (ogx, ogc),
        lambda: tuple(_hgrn_exact(hx, hc, w_hg, hg_lower_bounds, hg_norm_g[0], layer=0)))
    x1, hx1, hc1 = _outproj_mid(ogx, ogc, x, ctx, hg_w_out[0].astype(BF16), mod[0], mod[1], norm_g[1])
    er = rg_w_in.shape[-1] // 2
    rg_halves = jnp.concatenate([jnp.ones((er,), F32), jnp.full((er,), 0.5, F32)])
    yg = _rglru(hx1, hc1, (rg_w_in[0] * rg_halves).astype(BF16), rg_conv_w[0], rg_conv_b[0],
                rg_w_a[0].astype(BF16), rg_b_a[0], rg_w_x[0].astype(BF16), rg_b_x[0], rg_lambda[0])
    return _outproj_final(yg, x1, rg_w_out[0].astype(BF16), mod[1], final_norm_g)
```

```python
import functools

import jax
import jax.numpy as jnp
from jax import lax
from jax.experimental import pallas as pl
from jax.experimental.pallas import tpu as pltpu

F32 = jnp.float32
BF16 = jnp.bfloat16

EPS = 1e-6
F32_TINY = 1.1754944e-38
GRID_W = 64
HG_KEY_DIM = 128
HG_CHUNK = 64
HG_MAX_HALF_DECAY = 80.0
RG_BLOCK = 256
RG_C = 8.0
MOD_ROWS = 16
ROW_BLOCK = 256
GATE_ROWS = 64
HG_BLOCK = 512
VMEM_LIMIT = 56 * 1024 * 1024

_NT = (((1,), (1,)), ((), ()))
_TN = (((0,), (0,)), ((), ()))


def _silu(z):
    return _silu_half(0.5 * z)


def _silu_half(hz):
    return hz + hz * jnp.tanh(hz)


def _dot(a, b):
    return jnp.dot(a, b, preferred_element_type=F32)


def _ada_kernel(c_ref, w_ref, b_ref, o_ref):
    s = _silu(c_ref[...])
    w = w_ref[0]
    s_hi = s.astype(BF16)
    s_lo = (s - s_hi.astype(F32)).astype(BF16)
    w_hi = w.astype(BF16)
    w_lo = (w - w_hi.astype(F32)).astype(BF16)
    both = _dot(jnp.concatenate([s_hi, s_lo], axis=0), w_hi)
    o_ref[0] = both[:MOD_ROWS] + both[MOD_ROWS:] + _dot(s_hi, w_lo) + b_ref[0]


def _ada(cond, ada_w, ada_b):
    depth, d, n3 = ada_w.shape
    bn = d
    return pl.pallas_call(
        _ada_kernel,
        grid=(depth, n3 // bn),
        in_specs=[
            pl.BlockSpec((MOD_ROWS, d), lambda i, n: (0, 0)),
            pl.BlockSpec((1, d, bn), lambda i, n: (i, 0, n)),
            pl.BlockSpec((1, 1, bn), lambda i, n: (i, 0, n)),
        ],
        out_specs=pl.BlockSpec((1, MOD_ROWS, bn), lambda i, n: (i, 0, n)),
        out_shape=jax.ShapeDtypeStruct((depth, MOD_ROWS, n3), F32),
        name="ada",
    )(cond, ada_w, ada_b.reshape(depth, 1, n3))


def _norm_mod(xv, g, shift, scale):
    ms = jnp.mean(xv * xv, axis=-1, keepdims=True)
    return xv * lax.rsqrt(ms + EPS) * g * (1.0 + scale) + shift


def _prenorm_kernel(x_ref, ctx_ref, mod_ref, g_ref, hx_ref, hc_ref, *, d, ctx_row):
    b = pl.program_id(0)
    g = g_ref[...]
    row = mod_ref[pl.ds(b, 1), :]
    hx_ref[0] = _norm_mod(x_ref[0], g, row[:, :d], row[:, d:2 * d]).astype(BF16)

    @pl.when(pl.program_id(1) == 0)
    def _():
        rowc = mod_ref[ctx_row:ctx_row + 1, :]
        hc_ref[0] = _norm_mod(ctx_ref[0], g, rowc[:, :d], rowc[:, d:2 * d]).astype(BF16)


def _prenorm(x, ctx, mod, g):
    bsz, t, d = x.shape
    tc = ctx.shape[1]
    rb = t
    return pl.pallas_call(
        functools.partial(_prenorm_kernel, d=d, ctx_row=bsz),
        grid=(bsz, t // rb),
        in_specs=[
            pl.BlockSpec((1, rb, d), lambda b, r: (b, r, 0)),
            pl.BlockSpec((1, tc, d), lambda b, r: (b, 0, 0)),
            pl.BlockSpec((MOD_ROWS, 3 * d), lambda b, r: (0, 0)),
            pl.BlockSpec((1, d), lambda b, r: (0, 0)),
        ],
        out_specs=[
            pl.BlockSpec((1, rb, d), lambda b, r: (b, r, 0)),
            pl.BlockSpec((1, tc, d), lambda b, r: (b, 0, 0)),
        ],
        out_shape=[jax.ShapeDtypeStruct((bsz, t, d), BF16),
                   jax.ShapeDtypeStruct((bsz, tc, d), BF16)],
        compiler_params=pltpu.CompilerParams(
            dimension_semantics=("arbitrary", "arbitrary"), vmem_limit_bytes=VMEM_LIMIT),
        name="prenorm",
    )(x, ctx, mod, g.reshape(1, d))


def _hgrn_kernel(hx_ref, hc_ref, wq_ref, wv_ref, wf_ref, wb_ref, wg_ref, lbp_ref, ng_ref, tri_ref,
                 ox_ref, oc_ref, decay_ref,
                 qa_b, va_b, lffa_b, lfba_b, kfa_b, kba_b, qb_b, vb_b, lffb_b, lfbb_b, kfb_b, kbb_b,
                 qtf_s, qtb_s, sg_s, o_s, oif_s, oib_s, ut_s, em_s, wq_s, wv_s, wf_s, wb_s, wg_s,
                 *, layer, tc, t, heads):
    n_rows = tc + t
    c = HG_CHUNK
    kd = HG_KEY_DIM
    qt_s = (qtf_s, qtb_s)
    oi_s = (oif_s, oib_s)

    @pl.when(pl.program_id(1) == 0)
    def _():
        for w_ref, w_s, scale in ((wq_ref, wq_s, 0.5), (wv_ref, wv_s, 1.0), (wf_ref, wf_s, 0.5),
                                  (wb_ref, wb_s, 0.5), (wg_ref, wg_s, 0.5)):
            w_s[...] = (scale * w_ref[...]).astype(BF16)

    lbp = lbp_ref[...]
    ex = jnp.exp(lbp - jnp.max(lbp, axis=0, keepdims=True))
    lb = jnp.sum(ex[:layer + 1], axis=0, keepdims=True) / jnp.sum(ex, axis=0, keepdims=True)
    f_mid = 0.5 * (1.0 + lb)
    f_half = 0.5 * (1.0 - lb)

    rid = lax.broadcasted_iota(jnp.int32, (c, c), 0)
    cid = lax.broadcasted_iota(jnp.int32, (c, c), 1)
    masks = (rid >= cid, rid <= cid)

    def split3(xv):
        hi = xv.astype(BF16)
        r1 = xv - hi.astype(F32)
        mid = r1.astype(BF16)
        lo = (r1 - mid.astype(F32)).astype(BF16)
        return jnp.concatenate([hi, mid, lo], axis=0)

    def proj_steps(hrows, buf, r0):
        nr = hrows.shape[0]
        q_b, v_b, lf_b, k_b = buf

        def step_q():
            q_b[0:nr, :] = _silu_half(_dot(hrows, wq_s[...]))

        def step_v():
            v_b[0:nr, :] = _dot(hrows, wv_s[...]).astype(BF16)

        def step_f(d):
            f = f_mid + f_half * jnp.tanh(_dot(hrows, (wf_s, wb_s)[d][...]))
            lf_b[d][0:nr, :] = jnp.log(f)
            k_b[d][0:nr, :] = 1.0 - f

        def step_g():
            sg_s[pl.ds(r0, nr), :] = _silu_half(_dot(hrows, wg_s[...])).astype(BF16)

        return [step_q, step_v, functools.partial(step_f, 0), functools.partial(step_f, 1), step_g]

    def chunk_steps(buf, nr, r0, c0):
        q_b, v_b, lf_b, k_b = buf
        crs = [slice(cc * c, (cc + 1) * c) for cc in range(nr // c)]
        vals = {}

        def step_cumsum():
            vals["b"] = [[_dot(tri_ref[d], split3(lf_b[d][cr, :])) for d in range(2)] for cr in crs]

        def step_scale():
            qts, kts = [], []
            for cc, cr in enumerate(crs):
                qv = q_b[cr, :]
                qt, kt = [], []
                for d in range(2):
                    bsum = vals["b"][cc][d]
                    m = 0.5 * (bsum[c - 1:c, :] if d == 0 else bsum[0:1, :])
                    half_decay[0] = jnp.maximum(half_decay[0], jnp.abs(m))
                    em_s[d, pl.ds(c0 + cc, 1), :] = jnp.exp(m)
                    qt.append((qv * jnp.exp(bsum - m)).astype(BF16))
                    kt.append((k_b[d][cr, :] * jnp.exp(m - bsum)).astype(BF16))
                    qt_s[d][pl.ds(r0 + cc * c, c), :] = qt[d]
                qts.append(qt)
                kts.append(kt)
            vals["qt"], vals["kt"] = qts, kts

        def step_scores():
            scs = []
            for cc in range(len(crs)):
                for h in range(heads):
                    cs = slice(h * kd, (h + 1) * kd)
                    qt, kt = vals["qt"][cc], vals["kt"][cc]
                    sf = lax.dot_general(qt[0][:, cs], kt[0][:, cs], _NT, preferred_element_type=F32)
                    sb = lax.dot_general(qt[1][:, cs], kt[1][:, cs], _NT, preferred_element_type=F32)
                    scs.append((jnp.where(masks[0], sf, 0.0) + jnp.where(masks[1], sb, 0.0)).astype(BF16))
            vals["sc"] = scs

        def step_out():
            for cc, cr in enumerate(crs):
                for h in range(heads):
                    cs = slice(h * kd, (h + 1) * kd)
                    vv = v_b[cr, cs]
                    kt = vals["kt"][cc]
                    o_s[pl.ds(r0 + cc * c, c), cs] = _dot(vals["sc"][cc * heads + h], vv)
                    kk = jnp.concatenate([kt[0][:, cs], kt[1][:, cs]], axis=1)
                    ut_s[c0 + cc, h] = lax.dot_general(vv, kk, _TN, preferred_element_type=F32)

        return [step_cumsum, step_scale, step_scores, step_out]

    half_decay = [jnp.zeros((1, heads * kd), F32)]

    bufs = ((qa_b, va_b, (lffa_b, lfba_b), (kfa_b, kba_b)),
            (qb_b, vb_b, (lffb_b, lfbb_b), (kfb_b, kbb_b)))
    blocks = [(hc_ref[0], 0, tc)]
    blocks += [(hx_ref[0, i * HG_BLOCK:(i + 1) * HG_BLOCK, :], tc + i * HG_BLOCK, HG_BLOCK)
               for i in range(t // HG_BLOCK)]
    units = [(d, h) for d in range(2) for h in range(heads)]
    st = {u: jnp.zeros((kd, kd), F32) for u in units}

    def scan_step(d, ci):
        rows = slice(ci * c, (ci + 1) * c)
        for h in range(heads):
            cs = slice(h * kd, (h + 1) * kd)
            em = em_s[d, ci:ci + 1, cs]
            sm = st[(d, h)] * em
            st[(d, h)] = em * (sm + ut_s[ci, h][:, d * kd:(d + 1) * kd])
            oi_s[d][rows, cs] = lax.dot_general(
                qt_s[d][rows, cs], sm.astype(BF16), _NT, preferred_element_type=F32)

    prev, fwd_now, fwd_next = [], [], []
    for n, (hrows, r0, nr) in enumerate(blocks):
        proj = proj_steps(hrows, bufs[n % 2], r0)
        share = -(-len(fwd_now) // len(proj))
        for k, step in enumerate(proj):
            step()
            if k < len(prev):
                prev[k]()
            for f in fwd_now[k * share:(k + 1) * share]:
                f()
        fwd_now = fwd_next
        fwd_next = [functools.partial(scan_step, 0, ci) for ci in range(r0 // c, (r0 + nr) // c)]
        prev = chunk_steps(bufs[n % 2], nr, r0, r0 // c)
    for f in fwd_now:
        f()
    for step in prev:
        step()
    for f in fwd_next:
        f()
    decay_ref[0, 0] = half_decay[0]

    ng = ng_ref[...]

    def readout(r0):
        rows = slice(r0, r0 + ROW_BLOCK)
        o = o_s[rows, :] + oif_s[rows, :] + oib_s[rows, :]
        parts = []
        for h in range(heads):
            cs = slice(h * kd, (h + 1) * kd)
            oh = o[:, cs]
            ms = jnp.mean(oh * oh, axis=-1, keepdims=True)
            parts.append(oh * lax.rsqrt(ms + EPS) * ng[:, cs])
        y = (jnp.concatenate(parts, axis=-1) * sg_s[rows, :].astype(F32)).astype(BF16)
        if r0 < tc:
            oc_ref[0, r0:r0 + ROW_BLOCK, :] = y
        else:
            ox_ref[0, r0 - tc:r0 - tc + ROW_BLOCK, :] = y

    n_ctx = tc // c
    n_all = n_rows // c
    cpb = ROW_BLOCK // c
    order = list(range(n_ctx - 1, -1, -1)) + list(range(n_all - 1, n_ctx - 1, -1))

    @pl.when(pl.program_id(1) >= 0)
    def _():
        for ci in order:
            scan_step(1, ci)
            if ci % cpb == 0:
                readout(ci * c)


def _hgrn(hx, hc, w_in, lower_bounds, norm_g, layer):
    bsz, t, d = hx.shape
    tc = hc.shape[1]
    e = w_in.shape[1] // 5
    heads = 2
    gw = heads * HG_KEY_DIM
    ng_blocks = e // gw
    n_rows = tc + t
    c = HG_CHUNK

    low = jnp.tril(jnp.ones((c, c), F32))
    tri = jnp.stack([jnp.tile(low, (1, 3)), jnp.tile(low.T, (1, 3))]).astype(BF16)

    def wspec(sec):
        return pl.BlockSpec((d, gw), lambda j, b, sec=sec: (0, sec * ng_blocks + j))

    seq_f32 = pltpu.VMEM((n_rows, gw), F32)
    seq_bf16 = pltpu.VMEM((n_rows, gw), BF16)
    blk_f32 = pltpu.VMEM((HG_BLOCK, gw), F32)
    blk_bf16 = pltpu.VMEM((HG_BLOCK, gw), BF16)
    n_chunks = n_rows // c
    w_bf16 = pltpu.VMEM((d, gw), BF16)
    return pl.pallas_call(
        functools.partial(_hgrn_kernel, layer=layer, tc=tc, t=t, heads=heads),
        grid=(ng_blocks, bsz),
        in_specs=[
            pl.BlockSpec((1, t, d), lambda j, b: (b, 0, 0)),
            pl.BlockSpec((1, tc, d), lambda j, b: (b, 0, 0)),
            wspec(0), wspec(1), wspec(2), wspec(3), wspec(4),
            pl.BlockSpec((lower_bounds.shape[0], gw), lambda j, b: (0, j)),
            pl.BlockSpec((1, gw), lambda j, b: (0, j)),
            pl.BlockSpec((2, c, 3 * c), lambda j, b: (0, 0, 0)),
        ],
        out_specs=[
            pl.BlockSpec((1, t, gw), lambda j, b: (b, 0, j)),
            pl.BlockSpec((1, tc, gw), lambda j, b: (b, 0, j)),
            pl.BlockSpec((1, 1, 1, gw), lambda j, b: (b, j, 0, 0)),
        ],
        out_shape=[jax.ShapeDtypeStruct((bsz, t, e), BF16),
                   jax.ShapeDtypeStruct((bsz, tc, e), BF16),
                   jax.ShapeDtypeStruct((bsz, ng_blocks, 1, gw), F32)],
        scratch_shapes=[
            blk_f32, blk_bf16, blk_f32, blk_f32, blk_f32, blk_f32,
            blk_f32, blk_bf16, blk_f32, blk_f32, blk_f32, blk_f32,
            seq_bf16, seq_bf16,
            seq_bf16,
            seq_f32, seq_f32, seq_f32,
            pltpu.VMEM((n_chunks, heads, HG_KEY_DIM, 2 * HG_KEY_DIM), F32),
            pltpu.VMEM((2, n_chunks, gw), F32),
            w_bf16, w_bf16, w_bf16, w_bf16, w_bf16,
        ],
        compiler_params=pltpu.CompilerParams(
            dimension_semantics=("arbitrary", "arbitrary"), vmem_limit_bytes=VMEM_LIMIT),
        name="hgrn2",
    )(hx, hc, w_in, w_in, w_in, w_in, w_in, lower_bounds, norm_g.reshape(1, e), tri)


def _hgrn_exact_kernel(hx_ref, hc_ref, wq_ref, wv_ref, wf_ref, wb_ref, wg_ref, lbp_ref, ng_ref,
                       ox_ref, oc_ref, q_s, v_s, ff_s, fb_s, sg_s, o_s, st_s,
                       *, layer, tc, t, heads):
    n_rows = tc + t
    kd = HG_KEY_DIM
    f_refs = (ff_s, fb_s)

    lbp = lbp_ref[...]
    ex = jnp.exp(lbp - jnp.max(lbp, axis=0, keepdims=True))
    lb = jnp.sum(ex[:layer + 1], axis=0, keepdims=True) / jnp.sum(ex, axis=0, keepdims=True)
    f_mid = 0.5 * (1.0 + lb)
    f_half = 0.5 * (1.0 - lb)

    def project(hrows, r0):
        rows = pl.ds(r0, ROW_BLOCK)
        q_s[rows, :] = _silu_half(_dot(hrows, wq_ref[...]))
        v_s[rows, :] = _dot(hrows, wv_ref[...])
        ff_s[rows, :] = f_mid + f_half * jnp.tanh(_dot(hrows, wf_ref[...]))
        fb_s[rows, :] = f_mid + f_half * jnp.tanh(_dot(hrows, wb_ref[...]))
        sg_s[rows, :] = _silu_half(_dot(hrows, wg_ref[...]))

    for i in range(tc // ROW_BLOCK):
        project(hc_ref[0, i * ROW_BLOCK:(i + 1) * ROW_BLOCK, :], i * ROW_BLOCK)

    def proj_body(i, carry):
        r0 = pl.multiple_of(i * ROW_BLOCK, ROW_BLOCK)
        project(hx_ref[0, pl.ds(r0, ROW_BLOCK), :], tc + r0)
        return carry

    lax.fori_loop(0, t // ROW_BLOCK, proj_body, 0)

    o_s[...] = jnp.zeros_like(o_s)
    st_s[...] = jnp.zeros_like(st_s)
    rid = lax.broadcasted_iota(jnp.int32, (8, kd), 0)
    n_ct = tc // 8
    n_tiles = n_rows // 8

    def tile_body(i, carry):
        tiles = (i, jnp.where(i < n_ct, n_ct - 1 - i, n_tiles + n_ct - 1 - i))
        for d in range(2):
            rows = pl.ds(pl.multiple_of(tiles[d] * 8, 8), 8)
            for h in range(heads):
                cs = slice(h * kd, (h + 1) * kd)
                f8, v8, q8 = f_refs[d][rows, cs], v_s[rows, cs], q_s[rows, cs]
                k8 = 1.0 - f8
                s = st_s[d, h]
                acc = jnp.zeros((8, kd), F32)
                for r in (range(8) if d == 0 else range(7, -1, -1)):
                    sel = rid == r
                    outer = lax.dot_general(jnp.where(sel, v8, 0.0), k8, _TN, preferred_element_type=F32)
                    s = s * f8[r:r + 1, :] + outer
                    acc = acc + lax.dot_general(jnp.where(sel, q8, 0.0), s, _NT,
                                                preferred_element_type=F32)
                st_s[d, h] = s
                o_s[rows, cs] = o_s[rows, cs] + acc
        return carry

    lax.fori_loop(0, n_tiles, tile_body, 0)

    ng = ng_ref[...]

    def readout(r0):
        rows = pl.ds(r0, ROW_BLOCK)
        o = o_s[rows, :]
        parts = []
        for h in range(heads):
            cs = slice(h * kd, (h + 1) * kd)
            oh = o[:, cs]
            ms = jnp.mean(oh * oh, axis=-1, keepdims=True)
            parts.append(oh * lax.rsqrt(ms + EPS) * ng[:, cs])
        return (jnp.concatenate(parts, axis=-1) * sg_s[rows, :]).astype(BF16)

    for i in range(tc // ROW_BLOCK):
        oc_ref[0, i * ROW_BLOCK:(i + 1) * ROW_BLOCK, :] = readout(i * ROW_BLOCK)

    def out_body(i, carry):
        r0 = pl.multiple_of(i * ROW_BLOCK, ROW_BLOCK)
        ox_ref[0, pl.ds(r0, ROW_BLOCK), :] = readout(tc + r0)
        return carry

    lax.fori_loop(0, t // ROW_BLOCK, out_body, 0)


def _hgrn_exact(hx, hc, w_in, lower_bounds, norm_g, layer):
    bsz, t, d = hx.shape
    tc = hc.shape[1]
    e = w_in.shape[1] // 5
    heads = 2
    gw = heads * HG_KEY_DIM
    ng_blocks = e // gw
    seq_f32 = pltpu.VMEM((tc + t, gw), F32)

    def wspec(sec):
        return pl.BlockSpec((d, gw), lambda b, j, sec=sec: (0, sec * ng_blocks + j))

    return pl.pallas_call(
        functools.partial(_hgrn_exact_kernel, layer=layer, tc=tc, t=t, heads=heads),
        grid=(bsz, ng_blocks),
        in_specs=[
            pl.BlockSpec((1, t, d), lambda b, j: (b, 0, 0)),
            pl.BlockSpec((1, tc, d), lambda b, j: (b, 0, 0)),
            wspec(0), wspec(1), wspec(2), wspec(3), wspec(4),
            pl.BlockSpec((lower_bounds.shape[0], gw), lambda b, j: (0, j)),
            pl.BlockSpec((1, gw), lambda b, j: (0, j)),
        ],
        out_specs=[
            pl.BlockSpec((1, t, gw), lambda b, j: (b, 0, j)),
            pl.BlockSpec((1, tc, gw), lambda b, j: (b, 0, j)),
        ],
        out_shape=[jax.ShapeDtypeStruct((bsz, t, e), BF16),
                   jax.ShapeDtypeStruct((bsz, tc, e), BF16)],
        scratch_shapes=[
            seq_f32, seq_f32, seq_f32, seq_f32, seq_f32,
            seq_f32,
            pltpu.VMEM((2, heads, HG_KEY_DIM, HG_KEY_DIM), F32),
        ],
        compiler_params=pltpu.CompilerParams(
            dimension_semantics=("arbitrary", "arbitrary"), vmem_limit_bytes=VMEM_LIMIT),
        name="hgrn2_exact",
    )(hx, hc, w_in, w_in, w_in, w_in, w_in, lower_bounds, norm_g.reshape(1, e))


def _outproj_mid_kernel(ogx_ref, ogc_ref, x_ref, ctx_ref, w_ref, mod0_ref, mod1_ref, g1_ref,
                        x1_ref, hx_ref, hc_ref, *, d, ctx_row):
    b = pl.program_id(0)
    g1 = g1_ref[...]
    row0 = mod0_ref[pl.ds(b, 1), :]
    row1 = mod1_ref[pl.ds(b, 1), :]
    x1 = x_ref[0] + row0[:, 2 * d:] * _dot(ogx_ref[0], w_ref[...])
    x1_ref[0] = x1
    hx_ref[0] = _norm_mod(x1, g1, row1[:, :d], row1[:, d:2 * d]).astype(BF16)

    @pl.when(pl.program_id(1) == 0)
    def _():
        c0 = mod0_ref[ctx_row:ctx_row + 1, :]
        c1 = mod1_ref[ctx_row:ctx_row + 1, :]
        ctx1 = ctx_ref[0] + c0[:, 2 * d:] * _dot(ogc_ref[0], w_ref[...])
        hc_ref[0] = _norm_mod(ctx1, g1, c1[:, :d], c1[:, d:2 * d]).astype(BF16)


def _outproj_mid(ogx, ogc, x, ctx, w_out, mod0, mod1, g1):
    bsz, t, d = x.shape
    tc = ctx.shape[1]
    e = w_out.shape[0]
    rb = 1024
    return pl.pallas_call(
        functools.partial(_outproj_mid_kernel, d=d, ctx_row=bsz),
        grid=(bsz, t // rb),
        in_specs=[
            pl.BlockSpec((1, rb, e), lambda b, r: (b, r, 0)),
            pl.BlockSpec((1, tc, e), lambda b, r: (b, 0, 0)),
            pl.BlockSpec((1, rb, d), lambda b, r: (b, r, 0)),
            pl.BlockSpec((1, tc, d), lambda b, r: (b, 0, 0)),
            pl.BlockSpec((e, d), lambda b, r: (0, 0)),
            pl.BlockSpec((MOD_ROWS, 3 * d), lambda b, r: (0, 0)),
            pl.BlockSpec((MOD_ROWS, 3 * d), lambda b, r: (0, 0)),
            pl.BlockSpec((1, d), lambda b, r: (0, 0)),
        ],
        out_specs=[
            pl.BlockSpec((1, rb, d), lambda b, r: (b, r, 0)),
            pl.BlockSpec((1, rb, d), lambda b, r: (b, r, 0)),
            pl.BlockSpec((1, tc, d), lambda b, r: (b, 0, 0)),
        ],
        out_shape=[jax.ShapeDtypeStruct((bsz, t, d), F32),
                   jax.ShapeDtypeStruct((bsz, t, d), BF16),
                   jax.ShapeDtypeStruct((bsz, tc, d), BF16)],
        compiler_params=pltpu.CompilerParams(
            dimension_semantics=("arbitrary", "arbitrary"), vmem_limit_bytes=VMEM_LIMIT),
        name="outproj0",
    )(ogx, ogc, x, ctx, w_out, mod0, mod1, g1.reshape(1, d))


def _outproj_final_kernel(yg_ref, x_ref, w_ref, mod_ref, g_ref, o_ref, *, d):
    b = pl.program_id(0)
    row = mod_ref[pl.ds(b, 1), :]
    x2 = x_ref[0] + row[:, 2 * d:] * _dot(yg_ref[0], w_ref[...])
    ms = jnp.mean(x2 * x2, axis=-1, keepdims=True)
    o_ref[0] = x2 * lax.rsqrt(ms + EPS) * g_ref[...]


def _outproj_final(yg, x1, w_out, mod, g):
    bsz, t, d = x1.shape
    e = w_out.shape[0]
    rb = 1024
    return pl.pallas_call(
        functools.partial(_outproj_final_kernel, d=d),
        grid=(bsz, t // rb),
        in_specs=[
            pl.BlockSpec((1, rb, e), lambda b, r: (b, r, 0)),
            pl.BlockSpec((1, rb, d), lambda b, r: (b, r, 0)),
            pl.BlockSpec((e, d), lambda b, r: (0, 0)),
            pl.BlockSpec((MOD_ROWS, 3 * d), lambda b, r: (0, 0)),
            pl.BlockSpec((1, d), lambda b, r: (0, 0)),
        ],
        out_specs=pl.BlockSpec((1, rb, d), lambda b, r: (b, r, 0)),
        out_shape=jax.ShapeDtypeStruct((bsz, t, d), F32),
        compiler_params=pltpu.CompilerParams(
            dimension_semantics=("arbitrary", "arbitrary"), vmem_limit_bytes=VMEM_LIMIT),
        name="outproj1",
    )(yg, x1, w_out, mod, g.reshape(1, d))


def _shift_rows(xv, k):
    n = xv.shape[0]
    rid = lax.broadcasted_iota(jnp.int32, xv.shape, 0)
    rolled = pltpu.roll(xv, k % n, 0)
    valid = (rid >= k) if k > 0 else (rid < n + k)
    return jnp.where(valid, rolled, 0.0)


def _tile_scan(a, u, reverse):
    n = a.shape[0]
    rid = lax.broadcasted_iota(jnp.int32, a.shape, 0)
    s = 1
    while s < n:
        valid = (rid < n - s) if reverse else (rid >= s)
        sh = (n - s) if reverse else s
        a_sh = jnp.where(valid, pltpu.roll(a, sh, 0), 1.0)
        u_sh = jnp.where(valid, pltpu.roll(u, sh, 0), 0.0)
        u = u + a * u_sh
        a = a * a_sh
        s *= 2
    return a, u


def _rglru_kernel(hx_ref, hc_ref, wx_ref, wg_ref, cw_ref, cb_ref, wa_ref, wi_ref, ba_ref, bi_ref,
                  lam_ref, y_ref,
                  xp_s, xc_s, sg_s, af_s, uf_s, ab_s, ub_s,
                  *, tc, t):
    gw = RG_BLOCK
    gr = GRID_W
    n_tr = t // gr
    a_refs = (af_s, ab_s)
    u_refs = (uf_s, ub_s)
    cw = 0.5 * cw_ref[...]
    cb = 0.5 * cb_ref[...]

    def conv_taps(xm2, xm1, x0, xp1):
        return cw[0:1] * xm2 + cw[1:2] * xm1 + cw[2:3] * x0 + cw[3:4] * xp1 + cb

    xcx = _dot(hc_ref[0], wx_ref[...])
    xc_s[0:tc, :] = conv_taps(_shift_rows(xcx, 2), _shift_rows(xcx, 1), xcx, _shift_rows(xcx, -1))

    pad = 2 * gr
    nb = t // ROW_BLOCK
    hsp = (0.5 * RG_C) * jax.nn.softplus(-lam_ref[...])
    hba = 0.5 * ba_ref[...]
    hbi = 0.5 * bi_ref[...]

    def lat_r0(i):
        return i * ROW_BLOCK if isinstance(i, int) else pl.multiple_of(i * ROW_BLOCK, ROW_BLOCK)

    def proj_x(i):
        r0 = lat_r0(i)
        xp_s[pl.ds(pad + r0, ROW_BLOCK), :] = _dot(hx_ref[0, pl.ds(r0, ROW_BLOCK), :], wx_ref[...])

    def proj_g(i):
        r0 = lat_r0(i)
        sg_s[pl.ds(r0, ROW_BLOCK), :] = _silu_half(
            _dot(hx_ref[0, pl.ds(r0, ROW_BLOCK), :], wg_ref[...])).astype(BF16)

    def conv_block(i):
        for k in range(ROW_BLOCK // gr):
            r0 = lat_r0(i) + k * gr
            xc_s[pl.ds(tc + r0, gr), :] = conv_taps(
                xp_s[pl.ds(r0, gr), :], xp_s[pl.ds(r0 + gr, gr), :],
                xp_s[pl.ds(r0 + 2 * gr, gr), :], xp_s[pl.ds(r0 + 3 * gr, gr), :])

    tot = {"H": jnp.zeros((gr, gw), F32), "A": jnp.ones((gr, gw), F32),
           "E": jnp.zeros((gr, gw), F32), "G": jnp.ones((gr, gw), F32)}

    def gates(r0, d, latent):
        for part in range(ROW_BLOCK // GATE_ROWS):
            gates_part(r0 + part * GATE_ROWS, d, latent)

    def gates_part(r0, d, latent):
        rows = pl.ds(r0, GATE_ROWS)
        hxc = xc_s[rows, :]
        xcb = hxc.astype(BF16)
        tr = jnp.tanh(_dot(xcb, wa_ref[d, 0]) + hba[d:d + 1, :])
        ti = jnp.tanh(_dot(xcb, wi_ref[d, 0]) + hbi[d:d + 1, :])
        nla = hsp[d:d + 1, :] + hsp[d:d + 1, :] * tr
        a = jnp.exp(-nla)
        a_refs[d][rows, :] = a
        z = jnp.tanh(nla) * (a * a + 1.0)
        u = (z * lax.rsqrt(jnp.maximum(z, F32_TINY))) * (hxc + hxc * ti)
        u_refs[d][rows, :] = u
        if not latent:
            return
        for k in range(GATE_ROWS // gr):
            ak, uk = a[k * gr:(k + 1) * gr, :], u[k * gr:(k + 1) * gr, :]
            if d == 0:
                tot["H"] = ak * tot["H"] + uk
                tot["A"] = ak * tot["A"]
            else:
                tot["E"] = tot["E"] + tot["G"] * uk
                tot["G"] = tot["G"] * ak

    proj_x(nb - 1)
    for i in range(tc // ROW_BLOCK):
        gates(i * ROW_BLOCK, 0, False)
    proj_g(nb - 1)
    for i in range(tc // ROW_BLOCK):
        gates(i * ROW_BLOCK, 1, False)
    def ctx_final(d):
        n_ct = tc // 8
        tiles = [_tile_scan(a_refs[d][i * 8:(i + 1) * 8, :], u_refs[d][i * 8:(i + 1) * 8, :], d == 1)
                 for i in range(n_ct)]
        h = jnp.zeros((1, gw), F32)
        for i in (range(n_ct) if d == 0 else range(n_ct - 1, -1, -1)):
            a, u = tiles[i]
            h = a[7:8] * h + u[7:8] if d == 0 else a[0:1] * h + u[0:1]
        return h

    hf = ctx_final(0)
    hb = ctx_final(1)
    xp_s[0:gr, :] = _shift_rows(xp_s[pad + (n_tr - 2) * gr:pad + (n_tr - 1) * gr, :], 1)
    xp_s[gr:pad, :] = _shift_rows(xp_s[pad + (n_tr - 1) * gr:pad + n_tr * gr, :], 1)
    proj_x(0)
    proj_g(0)
    xp_s[pad + n_tr * gr:pad + (n_tr + 1) * gr, :] = _shift_rows(xp_s[pad:pad + gr, :], -1)

    def pipe_body(k):
        proj_x(k + 1)
        conv_block(k)
        gates(tc + lat_r0(k), 0, True)
        proj_g(k + 1)
        gates(tc + lat_r0(k), 1, True)

    for k in range(nb - 2):
        pipe_body(k)
    for i in (nb - 2, nb - 1):
        conv_block(i)
        gates(tc + lat_r0(i), 0, True)
        gates(tc + lat_r0(i), 1, True)

    def column_carries(e_tot, p_tot, h0, reverse):
        rid = lax.broadcasted_iota(jnp.int32, (8, gw), 0)
        order = range(gr // 8 - 1, -1, -1) if reverse else range(gr // 8)
        carry, out = h0, [None] * (gr // 8)
        for i in order:
            a_t, u_t = _tile_scan(p_tot[i * 8:(i + 1) * 8, :], e_tot[i * 8:(i + 1) * 8, :], reverse)
            h = u_t + a_t * carry
            if reverse:
                out[i] = jnp.where(rid == 7, carry, pltpu.roll(h, 7, 0))
                carry = h[0:1]
            else:
                out[i] = jnp.where(rid == 0, carry, pltpu.roll(h, 1, 0))
                carry = h[7:8]
        return jnp.concatenate(out, axis=0)

    h = column_carries(tot["H"], tot["A"], hf, False)
    for r in range(n_tr):
        rows = slice(tc + r * gr, tc + (r + 1) * gr)
        h = af_s[rows, :] * h + uf_s[rows, :]
        uf_s[rows, :] = h
    h = column_carries(tot["E"], tot["G"], hb, True)
    for r in range(n_tr - 1, -1, -1):
        rows = slice(tc + r * gr, tc + (r + 1) * gr)
        h = ab_s[rows, :] * h + ub_s[rows, :]
        gate = sg_s[r * gr:(r + 1) * gr, :].astype(F32)
        y_ref[0, r * gr:(r + 1) * gr, :] = ((uf_s[rows, :] + h) * gate).astype(BF16)


def _rglru(hx, hc, w_in, conv_w, conv_b, w_a, b_a, w_x, b_x, lam):
    bsz, t, d = hx.shape
    tc = hc.shape[1]
    e = w_in.shape[1] // 2
    gw = RG_BLOCK
    nb = e // gw
    n_rows = tc + t
    seq_f32 = pltpu.VMEM((n_rows, gw), F32)
    vec = lambda rows: pl.BlockSpec((rows, gw), lambda b, j: (0, j))
    gate_w = pl.BlockSpec((2, 1, gw, gw), lambda b, j: (0, j, 0, 0))
    return pl.pallas_call(
        functools.partial(_rglru_kernel, tc=tc, t=t),
        grid=(bsz, nb),
        in_specs=[
            pl.BlockSpec((1, t, d), lambda b, j: (b, 0, 0)),
            pl.BlockSpec((1, tc, d), lambda b, j: (b, 0, 0)),
            pl.BlockSpec((d, gw), lambda b, j: (0, j)),
            pl.BlockSpec((d, gw), lambda b, j: (0, nb + j)),
            vec(conv_w.shape[0]), vec(1),
            gate_w, gate_w, vec(2), vec(2), vec(2),
        ],
        out_specs=pl.BlockSpec((1, t, gw), lambda b, j: (b, 0, j)),
        out_shape=jax.ShapeDtypeStruct((bsz, t, e), BF16),
        scratch_shapes=[
            pltpu.VMEM((t + 3 * GRID_W, gw), F32),
            seq_f32,
            pltpu.VMEM((t, gw), BF16),
            seq_f32, seq_f32, seq_f32, seq_f32,
        ],
        compiler_params=pltpu.CompilerParams(
            dimension_semantics=("arbitrary", "arbitrary"), vmem_limit_bytes=VMEM_LIMIT),
        name="rglru",
    )(hx, hc, w_in, w_in, conv_w, conv_b.reshape(1, e), w_a, w_x, b_a, b_x, lam)


def kernel(x, c, ctx, c_ctx, ada_w, ada_b, norm_g, final_norm_g, hg_w_in, hg_lower_bounds,
           hg_norm_g, hg_w_out, rg_w_in, rg_conv_w, rg_conv_b, rg_w_a, rg_b_a, rg_w_x, rg_b_x,
           rg_lambda, rg_w_out):
    bsz, t, d = x.shape
    assert ada_w.shape[0] == 2 and bsz < MOD_ROWS and t % GRID_W == 0
    cond = jnp.concatenate(
        [c, c_ctx[None, :], jnp.zeros((MOD_ROWS - bsz - 1, d), F32)], axis=0)
    mod = _ada(cond, ada_w, ada_b)

    hx, hc = _prenorm(x, ctx, mod[0], norm_g[0])
    ogx, ogc, half_decay = _hgrn(hx, hc, hg_w_in[0], hg_lower_bounds, hg_norm_g[0], layer=0)

    def exact_layer():
        e = hg_w_in.shape[-1] // 5
        halves = jnp.concatenate([jnp.full((e,), 0.5, F32), jnp.ones((e,), F32), jnp.full((3 * e,), 0.5, F32)])
        w_hg = (hg_w_in[0] * halves).astype(BF16)
        return tuple(_hgrn_exact(hx, hc, w_hg, hg_lower_bounds, hg_norm_g[0], layer=0))

    ogx, ogc = lax.cond(
        jnp.all(half_decay < HG_MAX_HALF_DECAY),
        lambda: (ogx, ogc),
        exact_layer)
    x1, hx1, hc1 = _outproj_mid(ogx, ogc, x, ctx, hg_w_out[0].astype(BF16), mod[0], mod[1], norm_g[1])
    er = rg_w_in.shape[-1] // 2
    rg_halves = jnp.concatenate([jnp.ones((er,), F32), jnp.full((er,), 0.5, F32)])
    yg = _rglru(hx1, hc1, (rg_w_in[0] * rg_halves).astype(BF16), rg_conv_w[0], rg_conv_b[0],
                rg_w_a[0].astype(BF16), rg_b_a[0], rg_w_x[0].astype(BF16), rg_b_x[0], rg_lambda[0])
    return _outproj_final(yg, x1, rg_w_out[0].astype(BF16), mod[1], final_norm_g)
```
